```python
import math
import jax, jax.numpy as jnp
from jax import lax
import numpy as np

D_MODEL = 1024
BATCH = 8
SEQ = 4096
DEPTH = 2

HEAD_DIM = 64
NSA_HEADS = 8
NSA_KV_GROUPS = 2
NSA_WIDTH = NSA_HEADS * HEAD_DIM
ROPE_DIM = HEAD_DIM // 4
ROPE_THETA = 500000.0
CMP_LEN = 32
CMP_STRIDE = 16
CMP_HIDDEN = 128
SLC_LEN = 64
SLC_TOP = 16
WINDOW = 512
Q_BLOCK = 64
FORCE_SCORE = 1.0e4
NEG_INF = -1.0e30
SGU_WIDTH = 256
SGU_GROUPS = 4
SGU_CHUNK = 128
S5_WIDTH = 256
S5_GROUP_CH = 16
S5_GROUPS = S5_WIDTH // S5_GROUP_CH
S5_STATE = 64
MOE_GROUPS = 4
MOE_EXPERTS_PER_GROUP = 8
MOE_EXPERTS = MOE_GROUPS * MOE_EXPERTS_PER_GROUP
MOE_TOPK = 2
EXPERT_HIDDEN = 512
MOE_BLOCK = 128

RMS_EPS = 1e-6
KV_WIDTH = NSA_KV_GROUPS * HEAD_DIM
IN_PROJ_WIDTH = NSA_WIDTH + 6 * KV_WIDTH + 3 * NSA_HEADS + 2 * SGU_WIDTH + S5_WIDTH + 3 * D_MODEL

kernel_name = "hybrid_nsa_sgu_s5_hiermoe"


def _in_proj_split_points():
    sizes = (NSA_WIDTH, KV_WIDTH, KV_WIDTH, KV_WIDTH, KV_WIDTH, KV_WIDTH, KV_WIDTH,
             3 * NSA_HEADS, 2 * SGU_WIDTH, S5_WIDTH, 3 * D_MODEL)
    return [int(v) for v in np.cumsum(sizes)[:-1]]


def rms_norm(x, g):
    xf = x.astype(jnp.float32)
    y = xf * lax.rsqrt(jnp.mean(xf * xf, axis=-1, keepdims=True) + RMS_EPS)
    return (y * g.astype(jnp.float32)).astype(x.dtype)


def partial_rope(x, positions):
    half = ROPE_DIM // 2
    inv_freq = ROPE_THETA ** (-jnp.arange(half, dtype=jnp.float32) / half)
    ang = positions.astype(jnp.float32)[..., None] * inv_freq
    shp = ang.shape[:2] + (1,) * (x.ndim - 3) + (half,)
    cos, sin = jnp.cos(ang).reshape(shp), jnp.sin(ang).reshape(shp)
    xr = x[..., :ROPE_DIM].astype(jnp.float32)
    x1, x2 = xr[..., :half], xr[..., half:]
    rot = jnp.concatenate([x1 * cos - x2 * sin, x2 * cos + x1 * sin], axis=-1).astype(x.dtype)
    return jnp.concatenate([rot, x[..., ROPE_DIM:]], axis=-1)


def masked_softmax(scores, mask):
    s = jnp.where(mask, scores.astype(jnp.float32), NEG_INF)
    m = jnp.max(s, axis=-1, keepdims=True)
    e = jnp.where(mask, jnp.exp(s - m), 0.0)
    return e / jnp.maximum(jnp.sum(e, axis=-1, keepdims=True), 1e-30)


def compress_kv(kv, pos_emb, w1, w2):
    B, S, G, dh = kv.shape
    n_cmp = (S - CMP_LEN) // CMP_STRIDE + 1
    idx = np.arange(n_cmp)[:, None] * CMP_STRIDE + np.arange(CMP_LEN)[None, :]
    blocks = kv[:, idx] + pos_emb[None, None, :, None, :]
    flat = blocks.transpose(0, 1, 3, 2, 4).reshape(B, n_cmp, G, CMP_LEN * dh)
    return jax.nn.gelu(flat @ w1) @ w2


def _cmp_to_slc_overlap(n_cmp, n_slc):
    c0 = np.arange(n_cmp)[:, None] * CMP_STRIDE
    s0 = np.arange(n_slc)[None, :] * SLC_LEN
    ov = np.clip(np.minimum(c0 + CMP_LEN, s0 + SLC_LEN) - np.maximum(c0, s0), 0, None)
    return (ov / CMP_LEN).astype(np.float32)


def nsa_attention(q, k_cmp, v_cmp, k_slc, v_slc, k_win, v_win, gates, positions,
                  cmp_pos_k, cmp_pos_v, cmp_k_w1, cmp_k_w2, cmp_v_w1, cmp_v_w2):
    B, S, G, R, dh = q.shape
    scale = dh ** -0.5
    q_rot = partial_rope(q, positions)
    k_slc = partial_rope(k_slc, positions)
    k_win = partial_rope(k_win, positions)
    kc = compress_kv(k_cmp, cmp_pos_k, cmp_k_w1, cmp_k_w2)
    vc = compress_kv(v_cmp, cmp_pos_v, cmp_v_w1, cmp_v_w2)
    n_cmp = kc.shape[1]
    n_slc = S // SLC_LEN
    n_top = min(SLC_TOP, n_slc)
    cmp_end = jnp.asarray(np.arange(n_cmp) * CMP_STRIDE + CMP_LEN - 1, dtype=jnp.int32)
    overlap = jnp.asarray(_cmp_to_slc_overlap(n_cmp, n_slc))
    k_blocks = k_slc.reshape(B, n_slc, SLC_LEN, G, dh).transpose(0, 3, 1, 2, 4)
    v_blocks = v_slc.reshape(B, n_slc, SLC_LEN, G, dh).transpose(0, 3, 1, 2, 4)
    k_win_p = jnp.concatenate([jnp.zeros((B, WINDOW, G, dh), k_win.dtype), k_win], axis=1)
    v_win_p = jnp.concatenate([jnp.zeros((B, WINDOW, G, dh), v_win.dtype), v_win], axis=1)
    b_idx = jnp.arange(B)[:, None, None, None]
    g_idx = jnp.arange(G)[None, None, :, None]
    slc_ids = jnp.arange(n_slc)

    def query_block(i):
        t0 = i * Q_BLOCK
        t = t0 + jnp.arange(Q_BLOCK)
        qb = lax.dynamic_slice_in_dim(q, t0, Q_BLOCK, 1)
        qrb = lax.dynamic_slice_in_dim(q_rot, t0, Q_BLOCK, 1)
        gb = lax.dynamic_slice_in_dim(gates, t0, Q_BLOCK, 1)
        s_c = jnp.einsum('bqgrd,bcgd->bqgrc', qb, kc) * scale
        m_c = (cmp_end[None, :] <= t[:, None])[None, :, None, None, :]
        p_c = masked_softmax(s_c, m_c)
        o_c = jnp.einsum('bqgrc,bcgd->bqgrd', p_c.astype(vc.dtype), vc)
        imp = jnp.einsum('bqgrc,cj->bqgj', p_c, overlap)
        blk_t = (t // SLC_LEN)[:, None]
        causal = slc_ids[None, :] <= blk_t
        forced = (slc_ids[None, :] == 0) | (slc_ids[None, :] == blk_t) | (slc_ids[None, :] == blk_t - 1)
        imp = jnp.where(forced[None, :, None, :], FORCE_SCORE, imp)
        imp = jnp.where(causal[None, :, None, :], imp, NEG_INF)
        top_v, top_i = lax.top_k(imp, n_top)
        kg = k_blocks[b_idx, g_idx, top_i]
        vg = v_blocks[b_idx, g_idx, top_i]
        key_pos = top_i[..., None] * SLC_LEN + jnp.arange(SLC_LEN)
        m_s = (top_v > 0.5 * NEG_INF)[..., None] & (key_pos <= t[None, :, None, None, None])
        s_s = jnp.einsum('bqgrd,bqgnld->bqgrnl', qrb, kg) * scale
        p_s = masked_softmax(s_s.reshape(B, Q_BLOCK, G, R, n_top * SLC_LEN),
                             m_s.reshape(B, Q_BLOCK, G, 1, n_top * SLC_LEN))
        o_s = jnp.einsum('bqgrnl,bqgnld->bqgrd',
                         p_s.reshape(B, Q_BLOCK, G, R, n_top, SLC_LEN).astype(vg.dtype), vg)
        kw = lax.dynamic_slice_in_dim(k_win_p, t0, WINDOW + Q_BLOCK, 1)
        vw = lax.dynamic_slice_in_dim(v_win_p, t0, WINDOW + Q_BLOCK, 1)
        s_pos = t0 - WINDOW + jnp.arange(WINDOW + Q_BLOCK)
        diff = t[:, None] - s_pos[None, :]
        m_w = ((s_pos[None, :] >= 0) & (diff >= 0) & (diff < WINDOW))[None, :, None, None, :]
        s_w = jnp.einsum('bqgrd,bkgd->bqgrk', qrb, kw) * scale
        p_w = masked_softmax(s_w, m_w)
        o_w = jnp.einsum('bqgrk,bkgd->bqgrd', p_w.astype(vw.dtype), vw)
        return gb[..., 0:1] * o_c + gb[..., 1:2] * o_s + gb[..., 2:3] * o_w

    out = lax.map(query_block, jnp.arange(S // Q_BLOCK))
    return jnp.moveaxis(out, 0, 1).reshape(B, S, G * R * dh)


def spatial_gating(uv, norm_g, w_s, b_s):
    z = jax.nn.gelu(uv)
    u, v = jnp.split(z, 2, axis=-1)
    v = rms_norm(v, norm_g)
    B, S, _ = v.shape
    vg = v.reshape(B, S // SGU_CHUNK, SGU_CHUNK, SGU_GROUPS, SGU_WIDTH // SGU_GROUPS)
    tril = jnp.asarray(np.tril(np.ones((SGU_CHUNK, SGU_CHUNK), dtype=bool)))
    w = jnp.where(tril[None], w_s, 0.0).astype(v.dtype)
    mixed = jnp.einsum('gts,bcsgd->bctgd', w, vg) + b_s.T[None, None, :, :, None].astype(v.dtype)
    return u * mixed.reshape(B, S, SGU_WIDTH)


def s5_layer(xs, a_re, a_im, log_step, b_re, b_im, c_re, c_im, d, glu_w, glu_b):
    B, S, _ = xs.shape
    f32 = jnp.float32
    xg = xs.reshape(B, S, S5_GROUPS, S5_GROUP_CH).astype(f32)
    a_re, a_im = a_re.astype(f32), a_im.astype(f32)
    step = jnp.exp(log_step.astype(f32))[:, None]
    mag = jnp.exp(a_re * step)
    abar_re, abar_im = mag * jnp.cos(a_im * step), mag * jnp.sin(a_im * step)
    den = a_re * a_re + a_im * a_im
    nr, ni = abar_re - 1.0, abar_im
    coef_re = (nr * a_re + ni * a_im) / den
    coef_im = (ni * a_re - nr * a_im) / den
    b_re, b_im = b_re.astype(f32), b_im.astype(f32)
    bbar_re = coef_re[..., None] * b_re - coef_im[..., None] * b_im
    bbar_im = coef_re[..., None] * b_im + coef_im[..., None] * b_re
    bu_re = jnp.einsum('bsgh,gph->bsgp', xg, bbar_re)
    bu_im = jnp.einsum('bsgh,gph->bsgp', xg, bbar_im)
    ar = jnp.broadcast_to(abar_re, bu_re.shape)
    ai = jnp.broadcast_to(abar_im, bu_re.shape)

    def combine(e1, e2):
        a1r, a1i, b1r, b1i = e1
        a2r, a2i, b2r, b2i = e2
        return (a2r * a1r - a2i * a1i, a2r * a1i + a2i * a1r,
                a2r * b1r - a2i * b1i + b2r, a2r * b1i + a2i * b1r + b2i)

    _, _, h_re, h_im = lax.associative_scan(combine, (ar, ai, bu_re, bu_im), axis=1)
    y = (jnp.einsum('bsgp,ghp->bsgh', h_re, c_re.astype(f32))
         - jnp.einsum('bsgp,ghp->bsgh', h_im, c_im.astype(f32)))
    y = y + d.astype(f32).reshape(S5_GROUPS, S5_GROUP_CH) * xg
    y = jax.nn.gelu(y.reshape(B, S, S5_WIDTH))
    y = y * jax.nn.sigmoid(y @ glu_w.astype(f32) + glu_b.astype(f32))
    return y.astype(xs.dtype)


def routed_experts(xf, expert, weight, w_gate, w_up, w_down):
    T, D = xf.shape
    TK = expert.shape[0]
    n_exp = w_gate.shape[0]
    order = jnp.argsort(expert)
    e_sorted = expert[order]
    tok_sorted = order // MOE_TOPK
    counts = jnp.bincount(expert, length=n_exp)
    padded = (counts + MOE_BLOCK - 1) // MOE_BLOCK * MOE_BLOCK
    start = jnp.cumsum(counts) - counts
    end_p = jnp.cumsum(padded)
    start_p = end_p - padded
    dest = start_p[e_sorted] + jnp.arange(TK) - start[e_sorted]
    n_rows = TK + n_exp * MOE_BLOCK
    n_blocks = n_rows // MOE_BLOCK
    x_buf = jnp.zeros((n_rows, D), xf.dtype).at[dest].set(xf[tok_sorted])
    blk_exp = jnp.minimum(jnp.searchsorted(end_p, jnp.arange(n_blocks) * MOE_BLOCK, side='right'), n_exp - 1)

    def expert_block(args):
        xb, e = args
        hid = jax.nn.silu(xb @ w_gate[e]) * (xb @ w_up[e])
        return hid @ w_down[e]

    y_buf = lax.map(expert_block, (x_buf.reshape(n_blocks, MOE_BLOCK, D), blk_exp)).reshape(n_rows, D)
    y = y_buf[dest] * weight[order][:, None].astype(y_buf.dtype)
    return jax.ops.segment_sum(y, tok_sorted, num_segments=T)


def hierarchical_moe(h, router_group_w, router_group_b, router_expert_w, router_expert_b,
                     expert_w_gate, expert_w_up, expert_w_down):
    B, S, D = h.shape
    T = B * S
    xf = h.reshape(T, D)
    grp_logits = (xf @ router_group_w).astype(jnp.float32) + router_group_b.astype(jnp.float32)
    grp_prob = jax.nn.softmax(grp_logits, axis=-1)
    grp = jnp.argmax(grp_logits, axis=-1).astype(jnp.int32)
    grp_w = jnp.take_along_axis(grp_prob, grp[:, None], axis=-1)
    exp_logits = ((xf @ router_expert_w).astype(jnp.float32) + router_expert_b.astype(jnp.float32)
                  ).reshape(T, MOE_GROUPS, MOE_EXPERTS_PER_GROUP)
    exp_logits = jnp.take_along_axis(exp_logits, grp[:, None, None], axis=1)[:, 0]
    exp_prob = jax.nn.softmax(exp_logits, axis=-1)
    top_p, top_j = lax.top_k(exp_prob, MOE_TOPK)
    combine = grp_w * top_p / jnp.sum(top_p, axis=-1, keepdims=True)
    expert = grp[:, None] * MOE_EXPERTS_PER_GROUP + top_j.astype(jnp.int32)
    y = routed_experts(xf, expert.reshape(-1), combine.reshape(-1), expert_w_gate, expert_w_up, expert_w_down)
    return y.reshape(B, S, D)


def setup_inputs(seed: int = 0) -> dict:
    key = jax.random.key(seed)
    ks = iter(jax.random.split(key, 48))
    L = DEPTH

    def nrm(shape, scale):
        return jax.random.normal(next(ks), shape, jnp.float32) * scale

    x = nrm((BATCH, SEQ, D_MODEL), 1.0)
    offset = jax.random.randint(next(ks), (BATCH, 1), 0, 1024, dtype=jnp.int32)
    positions = offset + jnp.arange(SEQ, dtype=jnp.int32)[None, :]
    return {
        "x": x,
        "positions": positions,
        "norm_mix_g": 1.0 + nrm((L, D_MODEL), 0.02),
        "w_in": nrm((L, D_MODEL, IN_PROJ_WIDTH), D_MODEL ** -0.5),
        "cmp_pos_k": nrm((L, CMP_LEN, HEAD_DIM), 0.02),
        "cmp_pos_v": nrm((L, CMP_LEN, HEAD_DIM), 0.02),
        "cmp_k_w1": nrm((L, CMP_LEN * HEAD_DIM, CMP_HIDDEN), (CMP_LEN * HEAD_DIM) ** -0.5),
        "cmp_k_w2": nrm((L, CMP_HIDDEN, HEAD_DIM), CMP_HIDDEN ** -0.5),
        "cmp_v_w1": nrm((L, CMP_LEN * HEAD_DIM, CMP_HIDDEN), (CMP_LEN * HEAD_DIM) ** -0.5),
        "cmp_v_w2": nrm((L, CMP_HIDDEN, HEAD_DIM), CMP_HIDDEN ** -0.5),
        "w_attn_o": nrm((L, NSA_WIDTH, D_MODEL), NSA_WIDTH ** -0.5),
        "sgu_norm_g": 1.0 + nrm((L, SGU_WIDTH), 0.02),
        "sgu_w": nrm((L, SGU_GROUPS, SGU_CHUNK, SGU_CHUNK), 0.5 * SGU_CHUNK ** -0.5),
        "sgu_b": 1.0 + nrm((L, SGU_GROUPS, SGU_CHUNK), 0.02),
        "w_sgu_o": nrm((L, SGU_WIDTH, D_MODEL), SGU_WIDTH ** -0.5),
        "s5_a_re": -0.5 + nrm((L, S5_GROUPS, S5_STATE), 0.01),
        "s5_a_im": math.pi * jnp.arange(S5_STATE, dtype=jnp.float32) + nrm((L, S5_GROUPS, S5_STATE), 0.01),
        "s5_log_step": jax.random.uniform(next(ks), (L, S5_GROUPS), jnp.float32, math.log(1e-3), math.log(1e-1)),
        "s5_b_re": nrm((L, S5_GROUPS, S5_STATE, S5_GROUP_CH), (2 * S5_GROUP_CH) ** -0.5),
        "s5_b_im": nrm((L, S5_GROUPS, S5_STATE, S5_GROUP_CH), (2 * S5_GROUP_CH) ** -0.5),
        "s5_c_re": nrm((L, S5_GROUPS, S5_GROUP_CH, S5_STATE), (2 * S5_STATE) ** -0.5),
        "s5_c_im": nrm((L, S5_GROUPS, S5_GROUP_CH, S5_STATE), (2 * S5_STATE) ** -0.5),
        "s5_d": nrm((L, S5_WIDTH), 0.5),
        "s5_glu_w": nrm((L, S5_WIDTH, S5_WIDTH), S5_WIDTH ** -0.5),
        "s5_glu_b": nrm((L, S5_WIDTH), 0.01),
        "w_s5_o": nrm((L, S5_WIDTH, D_MODEL), S5_WIDTH ** -0.5),
        "w_mix_o": nrm((L, D_MODEL, D_MODEL), D_MODEL ** -0.5),
        "norm_ffn_g": 1.0 + nrm((L, D_MODEL), 0.02),
        "router_group_w": nrm((L, D_MODEL, MOE_GROUPS), D_MODEL ** -0.5),
        "router_group_b": nrm((L, MOE_GROUPS), 0.01),
        "router_expert_w": nrm((L, D_MODEL, MOE_EXPERTS), D_MODEL ** -0.5),
        "router_expert_b": nrm((L, MOE_EXPERTS), 0.01),
        "expert_w_gate": nrm((L, MOE_EXPERTS, D_MODEL, EXPERT_HIDDEN), D_MODEL ** -0.5),
        "expert_w_up": nrm((L, MOE_EXPERTS, D_MODEL, EXPERT_HIDDEN), D_MODEL ** -0.5),
        "expert_w_down": nrm((L, MOE_EXPERTS, EXPERT_HIDDEN, D_MODEL), EXPERT_HIDDEN ** -0.5),
        "norm_final_g": 1.0 + nrm((D_MODEL,), 0.02),
    }


def reference(x, positions, norm_mix_g, w_in, cmp_pos_k, cmp_pos_v, cmp_k_w1, cmp_k_w2,
              cmp_v_w1, cmp_v_w2, w_attn_o, sgu_norm_g, sgu_w, sgu_b, w_sgu_o,
              s5_a_re, s5_a_im, s5_log_step, s5_b_re, s5_b_im, s5_c_re, s5_c_im, s5_d,
              s5_glu_w, s5_glu_b, w_s5_o, w_mix_o, norm_ffn_g, router_group_w, router_group_b,
              router_expert_w, router_expert_b, expert_w_gate, expert_w_up, expert_w_down,
              norm_final_g):
    B, S, D = x.shape
    G, R = NSA_KV_GROUPS, NSA_HEADS // NSA_KV_GROUPS
    split_points = _in_proj_split_points()
    for l in range(DEPTH):
        h = rms_norm(x, norm_mix_g[l])
        parts = jnp.split(h @ w_in[l], split_points, axis=-1)
        q, k_c, v_c, k_s, v_s, k_w, v_w, nsa_g, sgu_in, s5_in, branch_g = parts
        q = q.reshape(B, S, G, R, HEAD_DIM)
        kv = [t.reshape(B, S, G, HEAD_DIM) for t in (k_c, v_c, k_s, v_s, k_w, v_w)]
        nsa_gates = jax.nn.sigmoid(nsa_g.astype(jnp.float32)).astype(q.dtype).reshape(B, S, G, R, 3)
        y_a = nsa_attention(q, kv[0], kv[1], kv[2], kv[3], kv[4], kv[5], nsa_gates, positions,
                            cmp_pos_k[l], cmp_pos_v[l], cmp_k_w1[l], cmp_k_w2[l],
                            cmp_v_w1[l], cmp_v_w2[l]) @ w_attn_o[l]
        y_b = spatial_gating(sgu_in, sgu_norm_g[l], sgu_w[l], sgu_b[l]) @ w_sgu_o[l]
        y_c = s5_layer(s5_in, s5_a_re[l], s5_a_im[l], s5_log_step[l], s5_b_re[l], s5_b_im[l],
                       s5_c_re[l], s5_c_im[l], s5_d[l], s5_glu_w[l], s5_glu_b[l]) @ w_s5_o[l]
        g_a, g_b, g_c = jnp.split(jax.nn.sigmoid(branch_g.astype(jnp.float32)), 3, axis=-1)
        merged = (g_a * y_a + g_b * y_b + g_c * y_c).astype(h.dtype)
        x = x + merged @ w_mix_o[l]
        h = rms_norm(x, norm_ffn_g[l])
        x = x + hierarchical_moe(h, router_group_w[l], router_group_b[l], router_expert_w[l],
                                 router_expert_b[l], expert_w_gate[l], expert_w_up[l], expert_w_down[l])
    return rms_norm(x, norm_final_g)
```

```python
import functools
import math

import numpy as np
import jax
import jax.numpy as jnp
from jax import lax
from jax.experimental import pallas as pl
from jax.experimental.pallas import tpu as pltpu

F32 = jnp.float32
BF16 = jnp.bfloat16

D_MODEL = 1024
HEAD_DIM = 64
NSA_HEADS = 8
NSA_KV_GROUPS = 2
HEADS_PER_GROUP = NSA_HEADS // NSA_KV_GROUPS
NSA_WIDTH = NSA_HEADS * HEAD_DIM
KV_WIDTH = NSA_KV_GROUPS * HEAD_DIM
ROPE_DIM = HEAD_DIM // 4
ROPE_HALF = ROPE_DIM // 2
ROPE_THETA = 500000.0
CMP_LEN = 32
CMP_STRIDE = 16
CMP_HIDDEN = 128
SLC_LEN = 64
SLC_TOP = 16
WINDOW = 512
FORCE_SCORE = 1.0e4
NEG_INF = -1.0e30
SGU_WIDTH = 256
SGU_GROUPS = 4
SGU_CHUNK = 128
S5_WIDTH = 256
S5_GROUP_CH = 16
S5_GROUPS = S5_WIDTH // S5_GROUP_CH
S5_STATE = 64
S5_LANES = S5_GROUPS * S5_STATE
MOE_GROUPS = 4
MOE_EXPERTS_PER_GROUP = 8
MOE_EXPERTS = MOE_GROUPS * MOE_EXPERTS_PER_GROUP
MOE_TOPK = 2
EXPERT_HIDDEN = 512
RMS_EPS = 1e-6
ATTN_SCALE = HEAD_DIM ** -0.5

LANES = 128
SUBLANES = 8
VMEM_LIMIT_BYTES = 56 * 1024 * 1024

GATE_COLS = HEADS_PER_GROUP * 3

_OFF_Q = 0
_OFF_KVC = _OFF_Q + NSA_WIDTH
_OFF_KS = _OFF_KVC + 2 * KV_WIDTH
_OFF_VS = _OFF_KS + KV_WIDTH
_OFF_KW = _OFF_VS + KV_WIDTH
_OFF_VW = _OFF_KW + KV_WIDTH
_OFF_GATE = _OFF_VW + KV_WIDTH
_OFF_SGU = _OFF_GATE + NSA_KV_GROUPS * LANES
_OFF_S5 = _OFF_SGU + 2 * SGU_WIDTH
_OFF_BG = _OFF_S5 + S5_WIDTH
_IN_W = _OFF_BG + 3 * D_MODEL


def _cparams(*sem):
    return pltpu.CompilerParams(dimension_semantics=sem, vmem_limit_bytes=VMEM_LIMIT_BYTES)


def _gelu(x):
    return 0.5 * x * (1.0 + jnp.tanh(math.sqrt(2.0 / math.pi) * (x + 0.044715 * (x * x * x))))


def _sigmoid(x):
    return 1.0 / (1.0 + jnp.exp(-x))


def _dot(a, b):
    return jnp.dot(a, b, preferred_element_type=F32)


def _dot_nt(a, b):
    return lax.dot_general(a, b, (((1,), (1,)), ((), ())), preferred_element_type=F32)


def _split_bf16(x):
    hi = x.astype(BF16)
    lo = (x - hi.astype(F32)).astype(BF16)
    return hi, lo


def _rms(x, g):
    return x * lax.rsqrt(jnp.mean(x * x, axis=-1, keepdims=True) + RMS_EPS) * g


def _masked_softmax(s, mask):
    s = jnp.where(mask, s, NEG_INF)
    m = jnp.max(s, axis=-1, keepdims=True)
    e = jnp.where(mask, jnp.exp(s - m), 0.0)
    return e / jnp.maximum(jnp.sum(e, axis=-1, keepdims=True), 1e-30)


def _rope(x, c, s_lo, s_hi):
    n = x.shape[-1]
    return x * c + pltpu.roll(x, n - ROPE_HALF, 1) * s_lo + pltpu.roll(x, ROPE_HALF, 1) * s_hi


def _in_proj_kernel(x_ref, g_ref, w_ref, c_ref, slo_ref, shi_ref,
                    q_ref, qrot_ref, kvc_ref, kv4_ref, gate_ref, sgu_ref, s5_ref, bg_ref):
    x = x_ref[...]
    hb = _rms(x, g_ref[...]).astype(BF16)

    def proj(off, width):
        return _dot(hb, w_ref[:, off:off + width])

    c, s_lo, s_hi = c_ref[...], slo_ref[...], shi_ref[...]
    rep = NSA_WIDTH // LANES
    q = proj(_OFF_Q, NSA_WIDTH)
    q_ref[...] = q * ATTN_SCALE
    qrot = _rope(q, jnp.concatenate([c] * rep, axis=1), jnp.concatenate([s_lo] * rep, axis=1),
                 jnp.concatenate([s_hi] * rep, axis=1))
    qrot_ref[...] = (qrot * ATTN_SCALE).astype(BF16)
    kvc_ref[...] = proj(_OFF_KVC, 2 * KV_WIDTH)
    ks = _rope(proj(_OFF_KS, KV_WIDTH), c, s_lo, s_hi)
    vs = proj(_OFF_VS, KV_WIDTH)
    kw = _rope(proj(_OFF_KW, KV_WIDTH), c, s_lo, s_hi)
    vw = proj(_OFF_VW, KV_WIDTH)
    for p, piece in enumerate((ks, vs, kw, vw)):
        for g in range(NSA_KV_GROUPS):
            kv4_ref[p * NSA_KV_GROUPS + g] = piece[:, g * HEAD_DIM:(g + 1) * HEAD_DIM].astype(BF16)
    gate_ref[...] = _sigmoid(proj(_OFF_GATE, NSA_KV_GROUPS * LANES))
    sgu_ref[...] = proj(_OFF_SGU, 2 * SGU_WIDTH)
    s5_ref[...] = proj(_OFF_S5, S5_WIDTH)
    for k in range(3):
        bg_ref[:, k * D_MODEL:(k + 1) * D_MODEL] = _sigmoid(
            proj(_OFF_BG + k * D_MODEL, D_MODEL)).astype(BF16)


def _prep_in_proj(w_in):
    sizes = (NSA_WIDTH,) + (KV_WIDTH,) * 6 + (3 * NSA_HEADS, 2 * SGU_WIDTH, S5_WIDTH, 3 * D_MODEL)
    offs = np.concatenate([[0], np.cumsum(sizes)])
    q, kc, vc, ks, vs, kw, vw, ng, sgu, s5, bg = [w_in[:, offs[i]:offs[i + 1]] for i in range(11)]
    pad = jnp.zeros((D_MODEL, LANES - GATE_COLS), w_in.dtype)
    gates = [jnp.concatenate([ng[:, g * GATE_COLS:(g + 1) * GATE_COLS], pad], axis=1)
             for g in range(NSA_KV_GROUPS)]
    return jnp.concatenate([q, kc, vc, ks, vs, kw, vw] + gates + [sgu, s5, bg], axis=1).astype(BF16)


def _rope_tables(positions):
    inv_freq = ROPE_THETA ** (-jnp.arange(ROPE_HALF, dtype=F32) / ROPE_HALF)
    ang = positions.astype(F32).reshape(-1, 1) * inv_freq
    cos, sin = jnp.cos(ang), jnp.sin(ang)
    t = ang.shape[0]
    rest = HEAD_DIM - ROPE_DIM
    c = jnp.concatenate([cos, cos, jnp.ones((t, rest), F32)], axis=1)
    s_lo = jnp.concatenate([-sin, jnp.zeros((t, HEAD_DIM - ROPE_HALF), F32)], axis=1)
    s_hi = jnp.concatenate([jnp.zeros((t, ROPE_HALF), F32), sin, jnp.zeros((t, rest), F32)], axis=1)
    rep = LANES // HEAD_DIM
    return tuple(jnp.concatenate([a] * rep, axis=1) for a in (c, s_lo, s_hi))


def _in_proj(x2, g, w, tables, tm):
    t = x2.shape[0]
    row = lambda width: pl.BlockSpec((tm, width), lambda i: (i, 0))
    full = lambda a: pl.BlockSpec(a.shape, lambda i: (0,) * a.ndim)
    out_shape = (
        jax.ShapeDtypeStruct((t, NSA_WIDTH), F32),
        jax.ShapeDtypeStruct((t, NSA_WIDTH), BF16),
        jax.ShapeDtypeStruct((t, 2 * KV_WIDTH), F32),
        jax.ShapeDtypeStruct((4 * NSA_KV_GROUPS, t, HEAD_DIM), BF16),
        jax.ShapeDtypeStruct((t, NSA_KV_GROUPS * LANES), F32),
        jax.ShapeDtypeStruct((t, 2 * SGU_WIDTH), F32),
        jax.ShapeDtypeStruct((t, S5_WIDTH), F32),
        jax.ShapeDtypeStruct((t, 3 * D_MODEL), BF16),
    )
    out_specs = (row(NSA_WIDTH), row(NSA_WIDTH), row(2 * KV_WIDTH),
                 pl.BlockSpec((4 * NSA_KV_GROUPS, tm, HEAD_DIM), lambda i: (0, i, 0)),
                 row(NSA_KV_GROUPS * LANES), row(2 * SGU_WIDTH), row(S5_WIDTH), row(3 * D_MODEL))
    return pl.pallas_call(
        _in_proj_kernel, grid=(t // tm,),
        in_specs=[row(D_MODEL), full(g), full(w), row(LANES), row(LANES), row(LANES)],
        out_specs=out_specs, out_shape=out_shape,
        compiler_params=_cparams("parallel"), name="in_proj",
    )(x2, g, w, *tables)


def _compress_kernel(r_ref, wa_ref, wb_ref, pos_ref, w1_ref, w2_ref, o_ref):
    rb = r_ref[0].astype(BF16)
    n = rb.shape[0]
    a = _dot(rb, wa_ref[...])
    b = _dot(rb, wb_ref[...])
    b = pltpu.roll(b, n - 1, 0)
    rows = lax.broadcasted_iota(jnp.int32, b.shape, 0)
    b = jnp.where(rows == n - 1, 0.0, b)
    for m in range(2 * NSA_KV_GROUPS):
        typ = m // NSA_KV_GROUPS
        bias = _dot(pos_ref[typ].astype(BF16), w1_ref[typ])
        hid = a[:, m * CMP_HIDDEN:(m + 1) * CMP_HIDDEN] + b[:, m * CMP_HIDDEN:(m + 1) * CMP_HIDDEN] + bias
        o_ref[0, m] = _dot(_gelu(hid).astype(BF16), w2_ref[typ])


def _prep_compress(cmp_pos_k, cmp_pos_v, k_w1, k_w2, v_w1, v_w2):
    streams = 2 * NSA_KV_GROUPS
    w1 = jnp.stack([k_w1, v_w1]).reshape(2, CMP_LEN, HEAD_DIM, CMP_HIDDEN)

    def expand(half):
        out = jnp.zeros((CMP_STRIDE, streams, HEAD_DIM, streams, CMP_HIDDEN), F32)
        for m in range(streams):
            out = out.at[:, m, :, m, :].set(w1[m // NSA_KV_GROUPS, half * CMP_STRIDE:(half + 1) * CMP_STRIDE])
        return out.reshape(CMP_STRIDE * streams * HEAD_DIM, streams * CMP_HIDDEN).astype(BF16)

    pos = jnp.stack([cmp_pos_k, cmp_pos_v]).reshape(2, 1, CMP_LEN * HEAD_DIM)
    return (expand(0), expand(1), pos, jnp.stack([k_w1, v_w1]).astype(BF16),
            jnp.stack([k_w2, v_w2]).astype(BF16))


def _compress(kvc, prep, batch, seq):
    wa, wb, pos, w1, w2 = prep
    n = seq // CMP_STRIDE
    r = kvc.reshape(batch, n, CMP_STRIDE * 2 * KV_WIDTH)
    full = lambda a: pl.BlockSpec(a.shape, lambda b: (0,) * a.ndim)
    return pl.pallas_call(
        _compress_kernel, grid=(batch,),
        in_specs=[pl.BlockSpec((1, n, r.shape[-1]), lambda b: (b, 0, 0)),
                  full(wa), full(wb), full(pos), full(w1), full(w2)],
        out_specs=pl.BlockSpec((1, 2 * NSA_KV_GROUPS, n, HEAD_DIM), lambda b: (b, 0, 0, 0)),
        out_shape=jax.ShapeDtypeStruct((batch, 2 * NSA_KV_GROUPS, n, HEAD_DIM), F32),
        compiler_params=_cparams("parallel"), name="compress_kv",
    )(r, wa, wb, pos, w1, w2)


def _attn_kernel(q_ref, qrot_ref, gate_ref, kc_ref, vc_ref, ks_ref, vs_ref, kw_ref, vw_ref, ov_ref,
                 o_ref, m_sc, l_sc, acc_sc, *, tq, tk, seq):
    n_cmp = kc_ref.shape[2]
    n_slc = seq // SLC_LEN
    t0 = pl.program_id(2) * tq
    t_col = t0 + lax.broadcasted_iota(jnp.int32, (tq, 1), 0)
    gates = gate_ref[...]

    kc_hi, kc_lo = _split_bf16(kc_ref[0, 0])
    vcb = vc_ref[0, 0].astype(BF16)
    cmp_end = lax.broadcasted_iota(jnp.int32, (1, n_cmp), 1) * CMP_STRIDE + (CMP_LEN - 1)
    m_c = cmp_end <= t_col
    p_sum = jnp.zeros((tq, n_cmp), F32)
    o_cmp = []
    for r in range(HEADS_PER_GROUP):
        q_hi, q_lo = _split_bf16(q_ref[:, r * HEAD_DIM:(r + 1) * HEAD_DIM])
        s = _dot_nt(q_hi, kc_hi) + _dot_nt(q_hi, kc_lo) + _dot_nt(q_lo, kc_hi)
        p = _masked_softmax(s, m_c)
        p_sum = p_sum + p
        o_cmp.append(_dot(p.astype(BF16), vcb))
    ps_hi, ps_lo = _split_bf16(p_sum)
    imp = _dot(ps_hi, ov_ref[...]) + _dot(ps_lo, ov_ref[...])

    j_idx = lax.broadcasted_iota(jnp.int32, (1, n_slc), 1)
    blk_t = lax.shift_right_logical(t_col, int(math.log2(SLC_LEN)))
    causal = j_idx <= blk_t
    forced = (j_idx == 0) | (j_idx == blk_t) | (j_idx == blk_t - 1)
    imp = jnp.where(forced, FORCE_SCORE, imp)
    imp = jnp.where(causal, imp, NEG_INF)
    rank = jnp.zeros((tq, n_slc), jnp.int32)
    for i in range(n_slc):
        col = imp[:, i:i + 1]
        beats = (col > imp) | ((col == imp) & (j_idx > i))
        rank = rank + beats.astype(jnp.int32)
    sel = ((rank < SLC_TOP) & causal).astype(F32).astype(BF16)

    m_sc[...] = jnp.full(m_sc.shape, NEG_INF, F32)
    l_sc[...] = jnp.zeros(l_sc.shape, F32)
    acc_sc[...] = jnp.zeros(acc_sc.shape, F32)
    q_rot = [qrot_ref[:, r * HEAD_DIM:(r + 1) * HEAD_DIM] for r in range(HEADS_PER_GROUP)]
    blk_row = lax.broadcasted_iota(jnp.int32, (n_slc, tk), 0)
    key_off = lax.broadcasted_iota(jnp.int32, (n_slc, tk), 1)
    key_off_row = lax.broadcasted_iota(jnp.int32, (1, tk), 1)

    def kv_step(j, carry):
        k0 = pl.multiple_of(j * tk, tk)
        k = ks_ref[0, pl.ds(k0, tk), :]
        v = vs_ref[0, pl.ds(k0, tk), :]
        expand = (lax.shift_right_logical(k0 + key_off, int(math.log2(SLC_LEN))) == blk_row)
        picked = _dot(sel, expand.astype(F32).astype(BF16)) > 0.5
        mask = picked & ((k0 + key_off_row) <= t_col)
        for r in range(HEADS_PER_GROUP):
            s = jnp.where(mask, _dot_nt(q_rot[r], k), NEG_INF)
            m_old = m_sc[r]
            m_new = jnp.maximum(m_old, jnp.max(s, axis=-1, keepdims=True))
            alpha = jnp.exp(m_old - m_new)
            p = jnp.where(mask, jnp.exp(s - m_new), 0.0)
            l_sc[r] = alpha * l_sc[r] + jnp.sum(p, axis=-1, keepdims=True)
            acc_sc[r] = alpha * acc_sc[r] + _dot(p.astype(BF16), v)
            m_sc[r] = m_new
        return carry

    lax.fori_loop(0, (t0 + tq + tk - 1) // tk, kv_step, 0)

    slab = min(WINDOW + tq, seq)
    st = pl.multiple_of(jnp.clip(t0 - WINDOW, 0, seq - slab), tq)
    kwin = kw_ref[0, pl.ds(st, slab), :]
    vwin = vw_ref[0, pl.ds(st, slab), :]
    diff = t_col - (st + lax.broadcasted_iota(jnp.int32, (1, slab), 1))
    m_w = (diff >= 0) & (diff < WINDOW)

    outs = []
    for r in range(HEADS_PER_GROUP):
        p_w = _masked_softmax(_dot_nt(q_rot[r], kwin), m_w)
        o_win = _dot(p_w.astype(BF16), vwin)
        o_slc = acc_sc[r] / jnp.maximum(l_sc[r], 1e-30)
        g0 = gates[:, 3 * r:3 * r + 1]
        g1 = gates[:, 3 * r + 1:3 * r + 2]
        g2 = gates[:, 3 * r + 2:3 * r + 3]
        outs.append(g0 * o_cmp[r] + g1 * o_slc + g2 * o_win)
    o_ref[...] = jnp.concatenate(outs, axis=1).astype(o_ref.dtype)


def _overlap_matrix(n_cmp, n_slc):
    c0 = np.arange(n_cmp)[:, None] * CMP_STRIDE
    s0 = np.arange(n_slc)[None, :] * SLC_LEN
    ov = np.clip(np.minimum(c0 + CMP_LEN, s0 + SLC_LEN) - np.maximum(c0, s0), 0, None) / CMP_LEN
    ov[n_cmp - 1] = 0.0
    return jnp.asarray(ov, BF16)


def _attention(q, qrot, gates, cmp_kv, kv4, batch, seq, tq, tk):
    n_cmp = seq // CMP_STRIDE
    n_slc = seq // SLC_LEN
    nq = seq // tq
    kv4 = kv4.reshape(4 * NSA_KV_GROUPS, batch, seq, HEAD_DIM)
    ov = _overlap_matrix(n_cmp, n_slc)
    qspec = pl.BlockSpec((tq, HEADS_PER_GROUP * HEAD_DIM), lambda b, g, i: (b * nq + i, g))
    cmp_spec = lambda m0: pl.BlockSpec((1, 1, n_cmp, HEAD_DIM), lambda b, g, i: (b, m0 + g, 0, 0))
    kv_spec = lambda p: pl.BlockSpec((None, 1, seq, HEAD_DIM), lambda b, g, i: (p * NSA_KV_GROUPS + g, b, 0, 0))
    kernel = functools.partial(_attn_kernel, tq=tq, tk=tk, seq=seq)
    return pl.pallas_call(
        kernel, grid=(batch, NSA_KV_GROUPS, nq),
        in_specs=[qspec, qspec, pl.BlockSpec((tq, LANES), lambda b, g, i: (b * nq + i, g)),
                  cmp_spec(0), cmp_spec(NSA_KV_GROUPS),
                  kv_spec(0), kv_spec(1), kv_spec(2), kv_spec(3),
                  pl.BlockSpec(ov.shape, lambda b, g, i: (0, 0))],
        out_specs=qspec,
        out_shape=jax.ShapeDtypeStruct((batch * seq, NSA_WIDTH), BF16),
        scratch_shapes=[pltpu.VMEM((HEADS_PER_GROUP, tq, 1), F32),
                        pltpu.VMEM((HEADS_PER_GROUP, tq, 1), F32),
                        pltpu.VMEM((HEADS_PER_GROUP, tq, HEAD_DIM), F32)],
        compiler_params=_cparams("parallel", "parallel", "arbitrary"), name="nsa_attention",
    )(q, qrot, gates, cmp_kv, cmp_kv, kv4, kv4, kv4, kv4, ov)


def _sgu_kernel(uv_ref, g_ref, w_ref, b_ref, o_ref, *, chunks):
    z = _gelu(uv_ref[...])
    u = z[:, :SGU_WIDTH]
    v = _rms(z[:, SGU_WIDTH:], g_ref[...])
    gw = SGU_WIDTH // SGU_GROUPS
    rows = lax.broadcasted_iota(jnp.int32, (SGU_CHUNK, SGU_GROUPS * SGU_CHUNK), 0)
    cols = lax.broadcasted_iota(jnp.int32, (SGU_CHUNK, SGU_GROUPS * SGU_CHUNK), 1)
    w = jnp.where((cols & (SGU_CHUNK - 1)) <= rows, w_ref[...], 0.0).astype(BF16)
    grp_r = lax.broadcasted_iota(jnp.int32, (SGU_GROUPS * SGU_CHUNK, SGU_WIDTH), 0) // SGU_CHUNK
    grp_c = lax.broadcasted_iota(jnp.int32, (SGU_GROUPS * SGU_CHUNK, SGU_WIDTH), 1) // gw
    for c in range(chunks):
        vc = v[c * SGU_CHUNK:(c + 1) * SGU_CHUNK].astype(BF16)
        v_bd = jnp.where(grp_r == grp_c, jnp.concatenate([vc] * SGU_GROUPS, axis=0), jnp.zeros((), BF16))
        mixed = _dot(w, v_bd) + b_ref[...]
        o_ref[c * SGU_CHUNK:(c + 1) * SGU_CHUNK, :] = (u[c * SGU_CHUNK:(c + 1) * SGU_CHUNK] * mixed).astype(o_ref.dtype)


def _sgu(sgu_in, norm_g, w_s, b_s, tm):
    t = sgu_in.shape[0]
    w_cat = jnp.transpose(w_s, (1, 0, 2)).reshape(SGU_CHUNK, SGU_GROUPS * SGU_CHUNK)
    bias = jnp.repeat(b_s.T, SGU_WIDTH // SGU_GROUPS, axis=1)
    g = norm_g.reshape(1, SGU_WIDTH)
    full = lambda a: pl.BlockSpec(a.shape, lambda i: (0,) * a.ndim)
    return pl.pallas_call(
        functools.partial(_sgu_kernel, chunks=tm // SGU_CHUNK), grid=(t // tm,),
        in_specs=[pl.BlockSpec((tm, 2 * SGU_WIDTH), lambda i: (i, 0)), full(g), full(w_cat), full(bias)],
        out_specs=pl.BlockSpec((tm, SGU_WIDTH), lambda i: (i, 0)),
        out_shape=jax.ShapeDtypeStruct((t, SGU_WIDTH), BF16),
        compiler_params=_cparams("parallel"), name="sgu",
    )(sgu_in, g, w_cat, bias)


def _s5_kernel(x_ref, bre_ref, bim_ref, are_ref, aim_ref, cre_ref, cim_ref, d_ref, gw_ref, gb_ref,
               o_ref, hre_sc, him_sc, ure_sc, uim_sc, *, steps, batch):
    @pl.when(pl.program_id(0) == 0)
    def _():
        hre_sc[...] = jnp.zeros(hre_sc.shape, F32)
        him_sc[...] = jnp.zeros(him_sc.shape, F32)

    x = x_ref[...]
    xb = x.astype(BF16)
    ure_sc[...] = _dot(xb, bre_ref[...])
    uim_sc[...] = _dot(xb, bim_ref[...])
    a_re = jnp.broadcast_to(are_ref[...], (batch, S5_LANES))
    a_im = jnp.broadcast_to(aim_ref[...], (batch, S5_LANES))

    def step(t, carry):
        h_re, h_im = carry
        rows = pl.ds(pl.multiple_of(t * batch, batch), batch)
        n_re = a_re * h_re - a_im * h_im + ure_sc[rows, :]
        n_im = a_re * h_im + a_im * h_re + uim_sc[rows, :]
        ure_sc[rows, :] = n_re
        uim_sc[rows, :] = n_im
        return n_re, n_im

    h_re, h_im = lax.fori_loop(0, steps, step, (hre_sc[...], him_sc[...]))
    hre_sc[...] = h_re
    him_sc[...] = h_im
    y = _dot(ure_sc[...].astype(BF16), cre_ref[...]) - _dot(uim_sc[...].astype(BF16), cim_ref[...])
    y = _gelu(y + d_ref[...] * x)
    y = y * _sigmoid(_dot(y.astype(BF16), gw_ref[...]) + gb_ref[...])
    o_ref[...] = y.astype(o_ref.dtype)


def _block_diag(blocks):
    g, r, c = blocks.shape
    eye = jnp.eye(g, dtype=blocks.dtype)
    return (blocks[:, :, None, :] * eye[:, None, :, None]).reshape(g * r, g * c)


def _prep_s5(a_re, a_im, log_step, b_re, b_im, c_re, c_im):
    step = jnp.exp(log_step)[:, None]
    mag = jnp.exp(a_re * step)
    abar_re, abar_im = mag * jnp.cos(a_im * step), mag * jnp.sin(a_im * step)
    den = a_re * a_re + a_im * a_im
    nr, ni = abar_re - 1.0, abar_im
    coef_re = (nr * a_re + ni * a_im) / den
    coef_im = (ni * a_re - nr * a_im) / den
    bbar_re = coef_re[..., None] * b_re - coef_im[..., None] * b_im
    bbar_im = coef_re[..., None] * b_im + coef_im[..., None] * b_re
    to_in = lambda b: _block_diag(jnp.transpose(b, (0, 2, 1))).astype(BF16)
    to_out = lambda c: _block_diag(jnp.transpose(c, (0, 2, 1))).astype(BF16)
    return (to_in(bbar_re), to_in(bbar_im), abar_re.reshape(1, S5_LANES), abar_im.reshape(1, S5_LANES),
            to_out(c_re), to_out(c_im))


def _s5(xs, prep, d, glu_w, glu_b, batch, seq, steps):
    bre, bim, are, aim, cre, cim = prep
    x_tm = jnp.transpose(xs.reshape(batch, seq, S5_WIDTH), (1, 0, 2)).reshape(seq * batch, S5_WIDTH)
    rows = steps * batch
    d2, gb2, gwb = d.reshape(1, S5_WIDTH), glu_b.reshape(1, S5_WIDTH), glu_w.astype(BF16)
    full = lambda a: pl.BlockSpec(a.shape, lambda i: (0,) * a.ndim)
    y = pl.pallas_call(
        functools.partial(_s5_kernel, steps=steps, batch=batch), grid=(seq // steps,),
        in_specs=[pl.BlockSpec((rows, S5_WIDTH), lambda i: (i, 0)), full(bre), full(bim), full(are), full(aim),
                  full(cre), full(cim), full(d2), full(gwb), full(gb2)],
        out_specs=pl.BlockSpec((rows, S5_WIDTH), lambda i: (i, 0)),
        out_shape=jax.ShapeDtypeStruct((seq * batch, S5_WIDTH), BF16),
        scratch_shapes=[pltpu.VMEM((batch, S5_LANES), F32), pltpu.VMEM((batch, S5_LANES), F32),
                        pltpu.VMEM((rows, S5_LANES), F32), pltpu.VMEM((rows, S5_LANES), F32)],
        compiler_params=_cparams("arbitrary"), name="s5_scan",
    )(x_tm, bre, bim, are, aim, cre, cim, d2, gwb, gb2)
    return jnp.transpose(y.reshape(seq, batch, S5_WIDTH), (1, 0, 2)).reshape(batch * seq, S5_WIDTH)


def _merge_kernel(x_ref, a_ref, b_ref, c_ref, bg_ref, wa_ref, wb_ref, wc_ref, wm_ref, g_ref,
                  wrh_ref, wrl_ref, br_ref, xo_ref, h_ref, route_ref):
    y_a = _dot(a_ref[...], wa_ref[...])
    y_b = _dot(b_ref[...], wb_ref[...])
    y_c = _dot(c_ref[...], wc_ref[...])
    merged = (bg_ref[:, :D_MODEL].astype(F32) * y_a + bg_ref[:, D_MODEL:2 * D_MODEL].astype(F32) * y_b
              + bg_ref[:, 2 * D_MODEL:].astype(F32) * y_c)
    x = x_ref[...] + _dot(merged.astype(BF16), wm_ref[...])
    xo_ref[...] = x
    h = _rms(x, g_ref[...])
    h_ref[...] = h

    h_hi, h_lo = _split_bf16(h)
    logits = _dot(h_hi, wrh_ref[...]) + _dot(h_hi, wrl_ref[...]) + _dot(h_lo, wrh_ref[...]) + br_ref[...]
    lane = lax.broadcasted_iota(jnp.int32, (1, LANES), 1)
    big = jnp.int32(LANES)
    is_grp = lane < MOE_GROUPS
    gl = jnp.where(is_grp, logits, -jnp.inf)
    gmax = jnp.max(gl, axis=-1, keepdims=True)
    grp = jnp.min(jnp.where(gl == gmax, lane, big), axis=-1, keepdims=True)
    grp_w = 1.0 / jnp.sum(jnp.where(is_grp, jnp.exp(logits - gmax), 0.0), axis=-1, keepdims=True)
    e_lane = lane - MOE_GROUPS
    in_grp = (e_lane >= 0) & (lax.shift_right_logical(jnp.maximum(e_lane, 0), 3) == grp) & (e_lane < MOE_EXPERTS)
    emax = jnp.max(jnp.where(in_grp, logits, -jnp.inf), axis=-1, keepdims=True)
    ee = jnp.where(in_grp, jnp.exp(logits - emax), 0.0)
    prob = jnp.where(in_grp, ee / jnp.sum(ee, axis=-1, keepdims=True), -1.0)
    p1 = jnp.max(prob, axis=-1, keepdims=True)
    j1 = jnp.min(jnp.where(prob == p1, lane, big), axis=-1, keepdims=True)
    prob2 = jnp.where(lane == j1, -1.0, prob)
    p2 = jnp.max(prob2, axis=-1, keepdims=True)
    j2 = jnp.min(jnp.where(prob2 == p2, lane, big), axis=-1, keepdims=True)
    psum = p1 + p2
    route = jnp.where(lane == 0, grp_w * p1 / psum, 0.0)
    route = jnp.where(lane == 1, grp_w * p2 / psum, route)
    route = jnp.where(lane == 2, (j1 - MOE_GROUPS).astype(F32), route)
    route = jnp.where(lane == 3, (j2 - MOE_GROUPS).astype(F32), route)
    route_ref[...] = jnp.broadcast_to(route, route_ref.shape)


def _merge(x2, attn, sgu, s5, bg, w_attn_o, w_sgu_o, w_s5_o, w_mix_o, norm_g, rgw, rgb, rew, reb, tm):
    t = x2.shape[0]
    wr = jnp.concatenate([rgw, rew, jnp.zeros((D_MODEL, LANES - MOE_GROUPS - MOE_EXPERTS), F32)], axis=1)
    wr_hi = wr.astype(BF16)
    wr_lo = (wr - wr_hi.astype(F32)).astype(BF16)
    br = jnp.concatenate([rgb, reb, jnp.zeros((LANES - MOE_GROUPS - MOE_EXPERTS,), F32)]).reshape(1, LANES)
    ws = [w.astype(BF16) for w in (w_attn_o, w_sgu_o, w_s5_o, w_mix_o)]
    g = norm_g.reshape(1, D_MODEL)
    row = lambda width: pl.BlockSpec((tm, width), lambda i: (i, 0))
    full = lambda a: pl.BlockSpec(a.shape, lambda i: (0,) * a.ndim)
    return pl.pallas_call(
        _merge_kernel, grid=(t // tm,),
        in_specs=[row(D_MODEL), row(NSA_WIDTH), row(SGU_WIDTH), row(S5_WIDTH), row(3 * D_MODEL)]
                 + [full(w) for w in ws] + [full(g), full(wr_hi), full(wr_lo), full(br)],
        out_specs=(row(D_MODEL), row(D_MODEL), row(LANES)),
        out_shape=(jax.ShapeDtypeStruct((t, D_MODEL), F32), jax.ShapeDtypeStruct((t, D_MODEL), F32),
                   jax.ShapeDtypeStruct((t, LANES), F32)),
        compiler_params=_cparams("parallel"), name="merge_route",
    )(x2, attn, sgu, s5, bg, *ws, g, wr_hi, wr_lo, br)


def _moe_kernel(blk_exp_ref, blk_cnt_ref, slot_ref, h_hbm, wg_ref, wu_ref, wd_ref, y_hbm,
                xbuf, ybuf, gsem, ssem, *, bm):
    del blk_exp_ref
    cnt = blk_cnt_ref[pl.program_id(0)]

    def gather_copy(r, tok):
        return pltpu.make_async_copy(h_hbm.at[pl.ds(tok, 1)], xbuf.at[pl.ds(r, 1)], gsem)

    def scatter_copy(r, slot):
        return pltpu.make_async_copy(ybuf.at[pl.ds(r, 1)], y_hbm.at[pl.ds(slot, 1)], ssem)

    @pl.when(cnt > 0)
    def _():
        def start_gather(r, carry):
            slot = jnp.maximum(slot_ref[0, 0, r], 0)
            gather_copy(r, lax.shift_right_logical(slot, 1)).start()
            return carry

        lax.fori_loop(0, bm, start_gather, 0)

        def wait_gather(r, carry):
            gather_copy(r, 0).wait()
            return carry

        lax.fori_loop(0, bm, wait_gather, 0)

        xb = xbuf[...].astype(BF16)
        gate = _dot(xb, wg_ref[0])
        hid = gate * _sigmoid(gate) * _dot(xb, wu_ref[0])
        ybuf[...] = _dot(hid.astype(BF16), wd_ref[0])

        def start_scatter(r, carry):
            scatter_copy(r, slot_ref[0, 0, r]).start()
            return carry

        lax.fori_loop(0, cnt, start_scatter, 0)

        def wait_scatter(r, carry):
            scatter_copy(r, 0).wait()
            return carry

        lax.fori_loop(0, cnt, wait_scatter, 0)


def _dispatch_plan(expert, bm):
    tk = expert.shape[0]
    nb = tk // bm + MOE_EXPERTS
    onehot = (expert[:, None] == jnp.arange(MOE_EXPERTS, dtype=jnp.int32)[None, :]).astype(jnp.int32)
    csum = jnp.cumsum(onehot, axis=0)
    rank = jnp.sum((csum - onehot) * onehot, axis=1)
    counts = csum[-1]
    nblk = (counts + bm - 1) // bm
    blk_end = jnp.cumsum(nblk)
    blk_start = blk_end - nblk
    dest = blk_start[expert] * bm + rank
    slots = jnp.full((nb * bm,), -1, jnp.int32).at[dest].set(jnp.arange(tk, dtype=jnp.int32))
    b = jnp.arange(nb, dtype=jnp.int32)
    blk_exp = jnp.minimum(jnp.searchsorted(blk_end, b, side="right").astype(jnp.int32), MOE_EXPERTS - 1)
    blk_cnt = jnp.clip(counts[blk_exp] - (b - blk_start[blk_exp]) * bm, 0, bm)
    blk_cnt = jnp.where(b < blk_end[-1], blk_cnt, 0).astype(jnp.int32)
    return blk_exp, blk_cnt, slots.reshape(nb, 1, bm)


def _experts(h, route, w_gate, w_up, w_down, bm):
    t = h.shape[0]
    expert = route[:, 2:4].astype(jnp.int32).reshape(-1)
    blk_exp, blk_cnt, slots = _dispatch_plan(expert, bm)
    nb = slots.shape[0]
    wspec = lambda shape: pl.BlockSpec((1,) + shape, lambda i, be, bc: (be[i], 0, 0))
    grid_spec = pltpu.PrefetchScalarGridSpec(
        num_scalar_prefetch=2, grid=(nb,),
        in_specs=[pl.BlockSpec((1, 1, bm), lambda i, be, bc: (i, 0, 0), memory_space=pltpu.SMEM),
                  pl.BlockSpec(memory_space=pl.ANY),
                  wspec((D_MODEL, EXPERT_HIDDEN)), wspec((D_MODEL, EXPERT_HIDDEN)),
                  wspec((EXPERT_HIDDEN, D_MODEL))],
        out_specs=pl.BlockSpec(memory_space=pl.ANY),
        scratch_shapes=[pltpu.VMEM((bm, D_MODEL), F32), pltpu.VMEM((bm, D_MODEL), F32),
                        pltpu.SemaphoreType.DMA(()), pltpu.SemaphoreType.DMA(())])
    return pl.pallas_call(
        functools.partial(_moe_kernel, bm=bm), grid_spec=grid_spec,
        out_shape=jax.ShapeDtypeStruct((t * MOE_TOPK, D_MODEL), F32),
        compiler_params=_cparams("arbitrary"), name="moe_experts",
    )(blk_exp, blk_cnt, slots, h, w_gate.astype(BF16), w_up.astype(BF16), w_down.astype(BF16))


def _combine_kernel(x_ref, y_ref, route_ref, g_ref, o_ref, *, final_norm):
    route = route_ref[...]
    x = x_ref[...] + route[:, 0:1] * y_ref[:, :D_MODEL] + route[:, 1:2] * y_ref[:, D_MODEL:]
    o_ref[...] = _rms(x, g_ref[...]) if final_norm else x


def _combine(x2, y_slots, route, g, final_norm, tm):
    t = x2.shape[0]
    y2 = y_slots.reshape(t, MOE_TOPK * D_MODEL)
    g2 = g.reshape(1, D_MODEL)
    row = lambda width: pl.BlockSpec((tm, width), lambda i: (i, 0))
    return pl.pallas_call(
        functools.partial(_combine_kernel, final_norm=final_norm), grid=(t // tm,),
        in_specs=[row(D_MODEL), row(MOE_TOPK * D_MODEL), row(LANES), pl.BlockSpec((1, D_MODEL), lambda i: (0, 0))],
        out_specs=row(D_MODEL), out_shape=jax.ShapeDtypeStruct((t, D_MODEL), F32),
        compiler_params=_cparams("parallel"), name="moe_combine",
    )(x2, y2, route, g2)


def _tiles(batch, seq):
    t = batch * seq
    return dict(
        tm_proj=min(256, t), tq=min(128, seq), tk=min(256, seq), tm_sgu=min(512, t),
        s5_steps=min(128, seq), tm_merge=min(256, t), bm=256, tm_comb=min(512, t))


def kernel(x, positions, norm_mix_g, w_in, cmp_pos_k, cmp_pos_v, cmp_k_w1, cmp_k_w2, cmp_v_w1, cmp_v_w2, w_attn_o, sgu_norm_g, sgu_w, sgu_b, w_sgu_o, s5_a_re, s5_a_im, s5_log_step, s5_b_re, s5_b_im, s5_c_re, s5_c_im, s5_d, s5_glu_w, s5_glu_b, w_s5_o, w_mix_o, norm_ffn_g, router_group_w, router_group_b, router_expert_w, router_expert_b, expert_w_gate, expert_w_up, expert_w_down, norm_final_g):
    batch, seq, _ = x.shape
    depth = w_in.shape[0]
    cfg = _tiles(batch, seq)
    tables = _rope_tables(positions)
    x2 = x.reshape(batch * seq, D_MODEL)
    for l in range(depth):
        q, qrot, kvc, kv4, gates, sgu_in, s5_in, bg = _in_proj(
            x2, norm_mix_g[l].reshape(1, D_MODEL), _prep_in_proj(w_in[l]), tables, cfg["tm_proj"])
        cmp_kv = _compress(kvc, _prep_compress(cmp_pos_k[l], cmp_pos_v[l], cmp_k_w1[l], cmp_k_w2[l],
                                               cmp_v_w1[l], cmp_v_w2[l]), batch, seq)
        attn = _attention(q, qrot, gates, cmp_kv, kv4, batch, seq, cfg["tq"], cfg["tk"])
        sgu = _sgu(sgu_in, sgu_norm_g[l], sgu_w[l], sgu_b[l], cfg["tm_sgu"])
        s5 = _s5(s5_in, _prep_s5(s5_a_re[l], s5_a_im[l], s5_log_step[l], s5_b_re[l], s5_b_im[l],
                                 s5_c_re[l], s5_c_im[l]),
                 s5_d[l], s5_glu_w[l], s5_glu_b[l], batch, seq, cfg["s5_steps"])
        x_mid, h, route = _merge(x2, attn, sgu, s5, bg, w_attn_o[l], w_sgu_o[l], w_s5_o[l], w_mix_o[l],
                                 norm_ffn_g[l], router_group_w[l], router_group_b[l],
                                 router_expert_w[l], router_expert_b[l], cfg["tm_merge"])
        y_slots = _experts(h, route, expert_w_gate[l], expert_w_up[l], expert_w_down[l], cfg["bm"])
        x2 = _combine(x_mid, y_slots, route, norm_final_g, l == depth - 1, cfg["tm_comb"])
    return x2.reshape(batch, seq, D_MODEL)
```

```python
import functools
import math

import numpy as np
import jax
import jax.numpy as jnp
from jax import lax
from jax.experimental import pallas as pl
from jax.experimental.pallas import tpu as pltpu

F32 = jnp.float32
BF16 = jnp.bfloat16

D_MODEL = 1024
HEAD_DIM = 64
NSA_HEADS = 8
NSA_KV_GROUPS = 2
HEADS_PER_GROUP = NSA_HEADS // NSA_KV_GROUPS
GROUP_WIDTH = HEADS_PER_GROUP * HEAD_DIM
NSA_WIDTH = NSA_HEADS * HEAD_DIM
KV_WIDTH = NSA_KV_GROUPS * HEAD_DIM
ROPE_DIM = HEAD_DIM // 4
ROPE_HALF = ROPE_DIM // 2
ROPE_THETA = 500000.0
CMP_LEN = 32
CMP_STRIDE = 16
CMP_HIDDEN = 128
SLC_LEN = 64
SLC_SHIFT = int(math.log2(SLC_LEN))
SLC_TOP = 16
WINDOW = 512
FORCE_SCORE = 1.0e4
NEG_INF = -1.0e30
SGU_WIDTH = 256
SGU_GROUPS = 4
SGU_CHUNK = 128
S5_WIDTH = 256
S5_GROUP_CH = 16
S5_GROUPS = S5_WIDTH // S5_GROUP_CH
S5_STATE = 64
S5_LANES = S5_GROUPS * S5_STATE
MOE_GROUPS = 4
MOE_EXPERTS_PER_GROUP = 8
MOE_EXPERTS = MOE_GROUPS * MOE_EXPERTS_PER_GROUP
MOE_TOPK = 2
EXPERT_HIDDEN = 512
RMS_EPS = 1e-6
ATTN_SCALE = HEAD_DIM ** -0.5

LANES = 128
SUBLANES = 8
VMEM_LIMIT_BYTES = 56 * 1024 * 1024

GATE_COLS = HEADS_PER_GROUP * 3
GATE_ROWS = 16

_OFF_Q = 0
_OFF_KVC = _OFF_Q + NSA_WIDTH
_OFF_KS = _OFF_KVC + 2 * KV_WIDTH
_OFF_VS = _OFF_KS + KV_WIDTH
_OFF_KW = _OFF_VS + KV_WIDTH
_OFF_VW = _OFF_KW + KV_WIDTH
_OFF_GATE = _OFF_VW + KV_WIDTH
_OFF_SGU = _OFF_GATE + LANES
_OFF_S5 = _OFF_SGU + 2 * SGU_WIDTH
_OFF_BG = _OFF_S5 + S5_WIDTH
_IN_W = _OFF_BG + 3 * D_MODEL


def _cparams(*sem):
    return pltpu.CompilerParams(dimension_semantics=sem, vmem_limit_bytes=VMEM_LIMIT_BYTES)


def _gelu(x):
    return 0.5 * x * (1.0 + jnp.tanh(math.sqrt(2.0 / math.pi) * (x + 0.044715 * (x * x * x))))


def _sigmoid(x):
    return 1.0 / (1.0 + jnp.exp(-x))


def _dot(a, b):
    return jnp.dot(a, b, preferred_element_type=F32)


def _dot_nt(a, b):
    return lax.dot_general(a, b, (((1,), (1,)), ((), ())), preferred_element_type=F32)


def _split_bf16(x):
    hi = x.astype(BF16)
    lo = (x - hi.astype(F32)).astype(BF16)
    return hi, lo


def _rms(x, g):
    return x * lax.rsqrt(jnp.mean(x * x, axis=-1, keepdims=True) + RMS_EPS) * g


def _rope(x, c, s_lo, s_hi):
    n = x.shape[-1]
    return x * c + pltpu.roll(x, n - ROPE_HALF, 1) * s_lo + pltpu.roll(x, ROPE_HALF, 1) * s_hi


def _in_proj_kernel(x_ref, g_ref, w_ref, c_ref, slo_ref, shi_ref,
                    qt_ref, qrt_ref, kvc_ref, krow_ref, vt_ref, gt_ref, sgu_ref, s5_ref, bg_ref):
    tm = x_ref.shape[0]
    x = x_ref[...]
    hb = _rms(x, g_ref[...]).astype(BF16)

    def proj(off, width):
        return _dot(hb, w_ref[:, off:off + width])

    c, s_lo, s_hi = c_ref[...], slo_ref[...], shi_ref[...]
    rep = NSA_WIDTH // LANES
    q = proj(_OFF_Q, NSA_WIDTH)
    qt_ref[...] = (q * ATTN_SCALE).T
    qrot = _rope(q, jnp.concatenate([c] * rep, axis=1), jnp.concatenate([s_lo] * rep, axis=1),
                 jnp.concatenate([s_hi] * rep, axis=1))
    qrt_ref[...] = (qrot * ATTN_SCALE).T.astype(BF16)
    kvc_ref[...] = proj(_OFF_KVC, 2 * KV_WIDTH)
    ks = _rope(proj(_OFF_KS, KV_WIDTH), c, s_lo, s_hi)
    kw = _rope(proj(_OFF_KW, KV_WIDTH), c, s_lo, s_hi)
    for p, piece in enumerate((ks, kw)):
        for g in range(NSA_KV_GROUPS):
            krow_ref[p * NSA_KV_GROUPS + g] = piece[:, g * HEAD_DIM:(g + 1) * HEAD_DIM].astype(BF16)
    for p, off in enumerate((_OFF_VS, _OFF_VW)):
        vt = proj(off, KV_WIDTH).T.astype(BF16)
        for g in range(NSA_KV_GROUPS):
            for ch in range(tm // LANES):
                vt_ref[p * NSA_KV_GROUPS + g, ch] = vt[g * HEAD_DIM:(g + 1) * HEAD_DIM, ch * LANES:(ch + 1) * LANES]
    gt_ref[...] = _sigmoid(proj(_OFF_GATE, LANES)).T[:NSA_KV_GROUPS * GATE_ROWS]
    sgu_ref[...] = proj(_OFF_SGU, 2 * SGU_WIDTH)
    s5_ref[...] = proj(_OFF_S5, S5_WIDTH)
    for k in range(3):
        bg_ref[:, k * D_MODEL:(k + 1) * D_MODEL] = _sigmoid(
            proj(_OFF_BG + k * D_MODEL, D_MODEL)).astype(BF16)


def _prep_in_proj(w_in):
    sizes = (NSA_WIDTH,) + (KV_WIDTH,) * 6 + (3 * NSA_HEADS, 2 * SGU_WIDTH, S5_WIDTH, 3 * D_MODEL)
    offs = np.concatenate([[0], np.cumsum(sizes)])
    q, kc, vc, ks, vs, kw, vw, ng, sgu, s5, bg = [w_in[:, offs[i]:offs[i + 1]] for i in range(11)]
    pad = lambda n: jnp.zeros((D_MODEL, n), w_in.dtype)
    gates = []
    for g in range(NSA_KV_GROUPS):
        gates += [ng[:, g * GATE_COLS:(g + 1) * GATE_COLS], pad(GATE_ROWS - GATE_COLS)]
    gates.append(pad(LANES - NSA_KV_GROUPS * GATE_ROWS))
    return jnp.concatenate([q, kc, vc, ks, vs, kw, vw] + gates + [sgu, s5, bg], axis=1).astype(BF16)


def _rope_tables(positions):
    inv_freq = ROPE_THETA ** (-jnp.arange(ROPE_HALF, dtype=F32) / ROPE_HALF)
    ang = positions.astype(F32).reshape(-1, 1) * inv_freq
    cos, sin = jnp.cos(ang), jnp.sin(ang)
    t = ang.shape[0]
    rest = HEAD_DIM - ROPE_DIM
    c = jnp.concatenate([cos, cos, jnp.ones((t, rest), F32)], axis=1)
    s_lo = jnp.concatenate([-sin, jnp.zeros((t, HEAD_DIM - ROPE_HALF), F32)], axis=1)
    s_hi = jnp.concatenate([jnp.zeros((t, ROPE_HALF), F32), sin, jnp.zeros((t, rest), F32)], axis=1)
    rep = LANES // HEAD_DIM
    return tuple(jnp.concatenate([a] * rep, axis=1) for a in (c, s_lo, s_hi))


def _in_proj(x2, g, w, tables, tm):
    t = x2.shape[0]
    row = lambda width: pl.BlockSpec((tm, width), lambda i: (i, 0))
    col = lambda height: pl.BlockSpec((height, tm), lambda i: (0, i))
    full = lambda a: pl.BlockSpec(a.shape, lambda i: (0,) * a.ndim)
    n_kv = 2 * NSA_KV_GROUPS
    out_shape = (
        jax.ShapeDtypeStruct((NSA_WIDTH, t), F32),
        jax.ShapeDtypeStruct((NSA_WIDTH, t), BF16),
        jax.ShapeDtypeStruct((t, 2 * KV_WIDTH), F32),
        jax.ShapeDtypeStruct((n_kv, t, HEAD_DIM), BF16),
        jax.ShapeDtypeStruct((n_kv, t // LANES, HEAD_DIM, LANES), BF16),
        jax.ShapeDtypeStruct((NSA_KV_GROUPS * GATE_ROWS, t), F32),
        jax.ShapeDtypeStruct((t, 2 * SGU_WIDTH), F32),
        jax.ShapeDtypeStruct((t, S5_WIDTH), F32),
        jax.ShapeDtypeStruct((t, 3 * D_MODEL), BF16),
    )
    out_specs = (col(NSA_WIDTH), col(NSA_WIDTH), row(2 * KV_WIDTH),
                 pl.BlockSpec((n_kv, tm, HEAD_DIM), lambda i: (0, i, 0)),
                 pl.BlockSpec((n_kv, tm // LANES, HEAD_DIM, LANES), lambda i: (0, i, 0, 0)),
                 col(NSA_KV_GROUPS * GATE_ROWS), row(2 * SGU_WIDTH), row(S5_WIDTH), row(3 * D_MODEL))
    return pl.pallas_call(
        _in_proj_kernel, grid=(t // tm,),
        in_specs=[row(D_MODEL), full(g), full(w), row(LANES), row(LANES), row(LANES)],
        out_specs=out_specs, out_shape=out_shape,
        compiler_params=_cparams("parallel"), name="in_proj",
    )(x2, g, w, *tables)


def _compress_kernel(r_ref, wa_ref, wb_ref, pos_ref, w1_ref, w2_ref, w2t_ref, kc_ref, vct_ref):
    rb = r_ref[0].astype(BF16)
    n = rb.shape[0]
    a = _dot(rb, wa_ref[...])
    b = _dot(rb, wb_ref[...])
    b = pltpu.roll(b, n - 1, 0)
    rows = lax.broadcasted_iota(jnp.int32, b.shape, 0)
    b = jnp.where(rows == n - 1, 0.0, b)
    for m in range(2 * NSA_KV_GROUPS):
        typ, g = divmod(m, NSA_KV_GROUPS)
        bias = _dot(pos_ref[typ].astype(BF16), w1_ref[typ])
        hid = a[:, m * CMP_HIDDEN:(m + 1) * CMP_HIDDEN] + b[:, m * CMP_HIDDEN:(m + 1) * CMP_HIDDEN] + bias
        act = _gelu(hid).astype(BF16)
        if typ == 0:
            kc_ref[0, g] = _dot(act, w2_ref[...])
        else:
            vct_ref[0, g] = _dot_nt(w2t_ref[...], act)


def _prep_compress(cmp_pos_k, cmp_pos_v, k_w1, k_w2, v_w1, v_w2):
    streams = 2 * NSA_KV_GROUPS
    w1 = jnp.stack([k_w1, v_w1]).reshape(2, CMP_LEN, HEAD_DIM, CMP_HIDDEN)

    def expand(half):
        out = jnp.zeros((CMP_STRIDE, streams, HEAD_DIM, streams, CMP_HIDDEN), F32)
        for m in range(streams):
            out = out.at[:, m, :, m, :].set(w1[m // NSA_KV_GROUPS, half * CMP_STRIDE:(half + 1) * CMP_STRIDE])
        return out.reshape(CMP_STRIDE * streams * HEAD_DIM, streams * CMP_HIDDEN).astype(BF16)

    pos = jnp.stack([cmp_pos_k, cmp_pos_v]).reshape(2, 1, CMP_LEN * HEAD_DIM)
    return (expand(0), expand(1), pos, jnp.stack([k_w1, v_w1]).astype(BF16),
            k_w2.astype(BF16), v_w2.T.astype(BF16))


def _compress(kvc, prep, batch, seq):
    n = seq // CMP_STRIDE
    r = kvc.reshape(batch, n, CMP_STRIDE * 2 * KV_WIDTH)
    full = lambda a: pl.BlockSpec(a.shape, lambda b: (0,) * a.ndim)
    return pl.pallas_call(
        _compress_kernel, grid=(batch,),
        in_specs=[pl.BlockSpec((1, n, r.shape[-1]), lambda b: (b, 0, 0))] + [full(a) for a in prep],
        out_specs=(pl.BlockSpec((1, NSA_KV_GROUPS, n, HEAD_DIM), lambda b: (b, 0, 0, 0)),
                   pl.BlockSpec((1, NSA_KV_GROUPS, HEAD_DIM, n), lambda b: (b, 0, 0, 0))),
        out_shape=(jax.ShapeDtypeStruct((batch, NSA_KV_GROUPS, n, HEAD_DIM), F32),
                   jax.ShapeDtypeStruct((batch, NSA_KV_GROUPS, HEAD_DIM, n), F32)),
        compiler_params=_cparams("parallel"), name="compress_kv",
    )(r, *prep)


def _attn_kernel(qt_ref, qrt_ref, gt_ref, kc_ref, vct_ref, ks_ref, vst_ref, kw_ref, vwt_ref, ovt_ref,
                 o_ref, bias_sc, m_sc, l_sc, acc_sc, *, tq, tk, seq):
    n_cmp = kc_ref.shape[2]
    n_slc = seq // SLC_LEN
    heads = HEADS_PER_GROUP
    t0 = pl.program_id(2) * tq
    t_lane = t0 + lax.broadcasted_iota(jnp.int32, (1, tq), 1)

    def lane_tile(a):
        return jnp.concatenate([a] * heads, axis=1)

    def head_cols(ref):
        return jnp.concatenate([ref[r * HEAD_DIM:(r + 1) * HEAD_DIM, :] for r in range(heads)], axis=1)

    q_hi, q_lo = _split_bf16(head_cols(qt_ref))
    qr = head_cols(qrt_ref)

    kc_hi, kc_lo = _split_bf16(kc_ref[0, 0])
    s = _dot(kc_hi, q_hi) + _dot(kc_lo, q_hi) + _dot(kc_hi, q_lo)
    cmp_end = lax.broadcasted_iota(jnp.int32, (n_cmp, 1), 0) * CMP_STRIDE + (CMP_LEN - 1)
    m_c = cmp_end <= lane_tile(t_lane)
    s = jnp.where(m_c, s, NEG_INF)
    e = jnp.where(m_c, jnp.exp(s - jnp.max(s, axis=0, keepdims=True)), 0.0)
    p = e / jnp.maximum(jnp.sum(e, axis=0, keepdims=True), 1e-30)
    o_cmp = _dot(vct_ref[0, 0].astype(BF16), p.astype(BF16))
    p_sum = p[:, 0:tq]
    for r in range(1, heads):
        p_sum = p_sum + p[:, r * tq:(r + 1) * tq]
    ps_hi, ps_lo = _split_bf16(p_sum)
    imp = _dot(ovt_ref[...], ps_hi) + _dot(ovt_ref[...], ps_lo)

    j_idx = lax.broadcasted_iota(jnp.int32, (n_slc, 1), 0)
    blk_t = lax.shift_right_logical(t_lane, SLC_SHIFT)
    causal = j_idx <= blk_t
    forced = (j_idx == 0) | (j_idx == blk_t) | (j_idx == blk_t - 1)
    imp = jnp.where(forced, FORCE_SCORE, imp)
    imp = jnp.where(causal, imp, NEG_INF)
    rank = jnp.zeros((n_slc, tq), F32)
    for i in range(n_slc):
        row = imp[i:i + 1, :]
        beats = (row > imp) | ((row == imp) & (j_idx > i))
        rank = rank + jnp.where(beats, 1.0, 0.0)
    bias_sc[...] = jnp.where((rank < SLC_TOP) & causal, 0.0, NEG_INF)

    m_sc[...] = jnp.full(m_sc.shape, NEG_INF, F32)
    l_sc[...] = jnp.zeros(l_sc.shape, F32)
    acc_sc[...] = jnp.zeros(acc_sc.shape, F32)
    blocks_per_tile = tk // SLC_LEN
    lanes_per_tile = tk // LANES

    def kv_step(j, diagonal):
        k0 = pl.multiple_of(j * tk, tk)
        s = _dot(ks_ref[0, pl.ds(k0, tk), :], qr)
        bias = jnp.concatenate(
            [jnp.broadcast_to(bias_sc[pl.ds(j * blocks_per_tile + b, 1), :], (SLC_LEN, tq))
             for b in range(blocks_per_tile)], axis=0)
        if diagonal:
            key_pos = k0 + lax.broadcasted_iota(jnp.int32, (tk, 1), 0)
            bias = jnp.where(key_pos <= t_lane, bias, NEG_INF)
        s = s + lane_tile(bias)
        m_old = m_sc[...]
        m_new = jnp.maximum(m_old, jnp.max(s, axis=0, keepdims=True))
        alpha = jnp.exp(m_old - m_new)
        p = jnp.exp(s - m_new)
        l_sc[...] = alpha * l_sc[...] + jnp.sum(p, axis=0, keepdims=True)
        v_t = jnp.concatenate([vst_ref[0, j * lanes_per_tile + c] for c in range(lanes_per_tile)], axis=1)
        acc_sc[...] = alpha * acc_sc[...] + _dot(v_t, p.astype(BF16))
        m_sc[...] = m_new

    n_kv = (t0 + tq + tk - 1) // tk

    def full_tile(j, carry):
        kv_step(j, False)
        return carry

    lax.fori_loop(0, n_kv - 1, full_tile, 0)
    kv_step(n_kv - 1, True)
    o_slc = acc_sc[...] / l_sc[...]

    slab = min(WINDOW + tq, seq)
    st = pl.multiple_of(jnp.clip(t0 - WINDOW, 0, seq - slab), LANES)
    s = _dot(kw_ref[0, pl.ds(st, slab), :], qr)
    diff = t_lane - (st + lax.broadcasted_iota(jnp.int32, (slab, 1), 0))
    s = s + lane_tile(jnp.where((diff >= 0) & (diff < WINDOW), 0.0, NEG_INF))
    p = jnp.exp(s - jnp.max(s, axis=0, keepdims=True))
    st_tile = st // LANES
    vw_t = jnp.concatenate([vwt_ref[0, st_tile + c] for c in range(slab // LANES)], axis=1)
    o_win = _dot(vw_t, p.astype(BF16)) / jnp.sum(p, axis=0, keepdims=True)

    gates = gt_ref[...]
    outs = []
    for r in range(heads):
        cols = slice(r * tq, (r + 1) * tq)
        outs.append(gates[3 * r:3 * r + 1, :] * o_cmp[:, cols] + gates[3 * r + 1:3 * r + 2, :] * o_slc[:, cols]
                    + gates[3 * r + 2:3 * r + 3, :] * o_win[:, cols])
    o_ref[...] = jnp.concatenate(outs, axis=0).T.astype(o_ref.dtype)


def _overlap_matrix_t(n_cmp, n_slc):
    c0 = np.arange(n_cmp)[None, :] * CMP_STRIDE
    s0 = np.arange(n_slc)[:, None] * SLC_LEN
    ov = np.clip(np.minimum(c0 + CMP_LEN, s0 + SLC_LEN) - np.maximum(c0, s0), 0, None) / CMP_LEN
    ov[:, n_cmp - 1] = 0.0
    return jnp.asarray(ov, BF16)


def _attention(qt, qrt, gt, kc, vct, krow, vt, batch, seq, tq, tk):
    n_cmp = seq // CMP_STRIDE
    n_slc = seq // SLC_LEN
    nq = seq // tq
    krow = krow.reshape(2 * NSA_KV_GROUPS, batch, seq, HEAD_DIM)
    vt = vt.reshape(2 * NSA_KV_GROUPS, batch, seq // LANES, HEAD_DIM, LANES)
    ovt = _overlap_matrix_t(n_cmp, n_slc)
    qspec = pl.BlockSpec((GROUP_WIDTH, tq), lambda b, g, i: (g, b * nq + i))
    k_spec = lambda p: pl.BlockSpec((None, 1, seq, HEAD_DIM), lambda b, g, i: (p * NSA_KV_GROUPS + g, b, 0, 0))
    v_spec = lambda p: pl.BlockSpec((None, 1, seq // LANES, HEAD_DIM, LANES),
                                    lambda b, g, i: (p * NSA_KV_GROUPS + g, b, 0, 0, 0))
    kernel = functools.partial(_attn_kernel, tq=tq, tk=tk, seq=seq)
    return pl.pallas_call(
        kernel, grid=(batch, NSA_KV_GROUPS, nq),
        in_specs=[qspec, qspec, pl.BlockSpec((GATE_ROWS, tq), lambda b, g, i: (g, b * nq + i)),
                  pl.BlockSpec((1, 1, n_cmp, HEAD_DIM), lambda b, g, i: (b, g, 0, 0)),
                  pl.BlockSpec((1, 1, HEAD_DIM, n_cmp), lambda b, g, i: (b, g, 0, 0)),
                  k_spec(0), v_spec(0), k_spec(1), v_spec(1),
                  pl.BlockSpec(ovt.shape, lambda b, g, i: (0, 0))],
        out_specs=pl.BlockSpec((tq, GROUP_WIDTH), lambda b, g, i: (b * nq + i, g)),
        out_shape=jax.ShapeDtypeStruct((batch * seq, NSA_WIDTH), BF16),
        scratch_shapes=[pltpu.VMEM((n_slc, tq), F32),
                        pltpu.VMEM((1, HEADS_PER_GROUP * tq), F32),
                        pltpu.VMEM((1, HEADS_PER_GROUP * tq), F32),
                        pltpu.VMEM((HEAD_DIM, HEADS_PER_GROUP * tq), F32)],
        compiler_params=_cparams("parallel", "parallel", "arbitrary"), name="nsa_attention",
    )(qt, qrt, gt, kc, vct, krow, vt, krow, vt, ovt)


def _sgu_kernel(uv_ref, g_ref, w_ref, b_ref, o_ref, *, chunks):
    z = _gelu(uv_ref[...])
    u = z[:, :SGU_WIDTH]
    v = _rms(z[:, SGU_WIDTH:], g_ref[...])
    gw = SGU_WIDTH // SGU_GROUPS
    rows = lax.broadcasted_iota(jnp.int32, (SGU_CHUNK, SGU_GROUPS * SGU_CHUNK), 0)
    cols = lax.broadcasted_iota(jnp.int32, (SGU_CHUNK, SGU_GROUPS * SGU_CHUNK), 1)
    w = jnp.where((cols & (SGU_CHUNK - 1)) <= rows, w_ref[...], 0.0).astype(BF16)
    grp_r = lax.broadcasted_iota(jnp.int32, (SGU_GROUPS * SGU_CHUNK, SGU_WIDTH), 0) // SGU_CHUNK
    grp_c = lax.broadcasted_iota(jnp.int32, (SGU_GROUPS * SGU_CHUNK, SGU_WIDTH), 1) // gw
    for c in range(chunks):
        vc = v[c * SGU_CHUNK:(c + 1) * SGU_CHUNK].astype(BF16)
        v_bd = jnp.where(grp_r == grp_c, jnp.concatenate([vc] * SGU_GROUPS, axis=0), jnp.zeros((), BF16))
        mixed = _dot(w, v_bd) + b_ref[...]
        o_ref[c * SGU_CHUNK:(c + 1) * SGU_CHUNK, :] = (u[c * SGU_CHUNK:(c + 1) * SGU_CHUNK] * mixed).astype(o_ref.dtype)


def _sgu(sgu_in, norm_g, w_s, b_s, tm):
    t = sgu_in.shape[0]
    w_cat = jnp.transpose(w_s, (1, 0, 2)).reshape(SGU_CHUNK, SGU_GROUPS * SGU_CHUNK)
    bias = jnp.repeat(b_s.T, SGU_WIDTH // SGU_GROUPS, axis=1)
    g = norm_g.reshape(1, SGU_WIDTH)
    full = lambda a: pl.BlockSpec(a.shape, lambda i: (0,) * a.ndim)
    return pl.pallas_call(
        functools.partial(_sgu_kernel, chunks=tm // SGU_CHUNK), grid=(t // tm,),
        in_specs=[pl.BlockSpec((tm, 2 * SGU_WIDTH), lambda i: (i, 0)), full(g), full(w_cat), full(bias)],
        out_specs=pl.BlockSpec((tm, SGU_WIDTH), lambda i: (i, 0)),
        out_shape=jax.ShapeDtypeStruct((t, SGU_WIDTH), BF16),
        compiler_params=_cparams("parallel"), name="sgu",
    )(sgu_in, g, w_cat, bias)


def _s5_kernel(x_ref, bre_ref, bim_ref, are_ref, aim_ref, cre_ref, cim_ref, d_ref, gw_ref, gb_ref,
               o_ref, hre_sc, him_sc, ure_sc, uim_sc, *, steps, batch):
    @pl.when(pl.program_id(0) == 0)
    def _():
        hre_sc[...] = jnp.zeros(hre_sc.shape, F32)
        him_sc[...] = jnp.zeros(him_sc.shape, F32)

    x = x_ref[...]
    xb = x.astype(BF16)
    ure_sc[...] = _dot(xb, bre_ref[...])
    uim_sc[...] = _dot(xb, bim_ref[...])
    a_re = jnp.broadcast_to(are_ref[...], (batch, S5_LANES))
    a_im = jnp.broadcast_to(aim_ref[...], (batch, S5_LANES))

    def step(t, carry):
        h_re, h_im = carry
        rows = pl.ds(pl.multiple_of(t * batch, batch), batch)
        n_re = a_re * h_re - a_im * h_im + ure_sc[rows, :]
        n_im = a_re * h_im + a_im * h_re + uim_sc[rows, :]
        ure_sc[rows, :] = n_re
        uim_sc[rows, :] = n_im
        return n_re, n_im

    h_re, h_im = lax.fori_loop(0, steps, step, (hre_sc[...], him_sc[...]))
    hre_sc[...] = h_re
    him_sc[...] = h_im
    y = _dot(ure_sc[...].astype(BF16), cre_ref[...]) - _dot(uim_sc[...].astype(BF16), cim_ref[...])
    y = _gelu(y + d_ref[...] * x)
    y = y * _sigmoid(_dot(y.astype(BF16), gw_ref[...]) + gb_ref[...])
    o_ref[...] = y.astype(o_ref.dtype)


def _block_diag(blocks):
    g, r, c = blocks.shape
    eye = jnp.eye(g, dtype=blocks.dtype)
    return (blocks[:, :, None, :] * eye[:, None, :, None]).reshape(g * r, g * c)


def _prep_s5(a_re, a_im, log_step, b_re, b_im, c_re, c_im):
    step = jnp.exp(log_step)[:, None]
    mag = jnp.exp(a_re * step)
    abar_re, abar_im = mag * jnp.cos(a_im * step), mag * jnp.sin(a_im * step)
    den = a_re * a_re + a_im * a_im
    nr, ni = abar_re - 1.0, abar_im
    coef_re = (nr * a_re + ni * a_im) / den
    coef_im = (ni * a_re - nr * a_im) / den
    bbar_re = coef_re[..., None] * b_re - coef_im[..., None] * b_im
    bbar_im = coef_re[..., None] * b_im + coef_im[..., None] * b_re
    to_in = lambda b: _block_diag(jnp.transpose(b, (0, 2, 1))).astype(BF16)
    to_out = lambda c: _block_diag(jnp.transpose(c, (0, 2, 1))).astype(BF16)
    return (to_in(bbar_re), to_in(bbar_im), abar_re.reshape(1, S5_LANES), abar_im.reshape(1, S5_LANES),
            to_out(c_re), to_out(c_im))


def _s5(xs, prep, d, glu_w, glu_b, batch, seq, steps):
    bre, bim, are, aim, cre, cim = prep
    x_tm = jnp.transpose(xs.reshape(batch, seq, S5_WIDTH), (1, 0, 2)).reshape(seq * batch, S5_WIDTH)
    rows = steps * batch
    d2, gb2, gwb = d.reshape(1, S5_WIDTH), glu_b.reshape(1, S5_WIDTH), glu_w.astype(BF16)
    full = lambda a: pl.BlockSpec(a.shape, lambda i: (0,) * a.ndim)
    y = pl.pallas_call(
        functools.partial(_s5_kernel, steps=steps, batch=batch), grid=(seq // steps,),
        in_specs=[pl.BlockSpec((rows, S5_WIDTH), lambda i: (i, 0)), full(bre), full(bim), full(are), full(aim),
                  full(cre), full(cim), full(d2), full(gwb), full(gb2)],
        out_specs=pl.BlockSpec((rows, S5_WIDTH), lambda i: (i, 0)),
        out_shape=jax.ShapeDtypeStruct((seq * batch, S5_WIDTH), BF16),
        scratch_shapes=[pltpu.VMEM((batch, S5_LANES), F32), pltpu.VMEM((batch, S5_LANES), F32),
                        pltpu.VMEM((rows, S5_LANES), F32), pltpu.VMEM((rows, S5_LANES), F32)],
        compiler_params=_cparams("arbitrary"), name="s5_scan",
    )(x_tm, bre, bim, are, aim, cre, cim, d2, gwb, gb2)
    return jnp.transpose(y.reshape(seq, batch, S5_WIDTH), (1, 0, 2)).reshape(batch * seq, S5_WIDTH)


def _merge_kernel(x_ref, a_ref, b_ref, c_ref, bg_ref, wa_ref, wb_ref, wc_ref, wm_ref, g_ref,
                  wrh_ref, wrl_ref, br_ref, xo_ref, h_ref, route_ref):
    y_a = _dot(a_ref[...], wa_ref[...])
    y_b = _dot(b_ref[...], wb_ref[...])
    y_c = _dot(c_ref[...], wc_ref[...])
    merged = (bg_ref[:, :D_MODEL].astype(F32) * y_a + bg_ref[:, D_MODEL:2 * D_MODEL].astype(F32) * y_b
              + bg_ref[:, 2 * D_MODEL:].astype(F32) * y_c)
    x = x_ref[...] + _dot(merged.astype(BF16), wm_ref[...])
    xo_ref[...] = x
    h = _rms(x, g_ref[...])
    h_ref[...] = h

    h_hi, h_lo = _split_bf16(h)
    logits = _dot(h_hi, wrh_ref[...]) + _dot(h_hi, wrl_ref[...]) + _dot(h_lo, wrh_ref[...]) + br_ref[...]
    lane = lax.broadcasted_iota(jnp.int32, (1, LANES), 1)
    big = jnp.int32(LANES)
    is_grp = lane < MOE_GROUPS
    gl = jnp.where(is_grp, logits, -jnp.inf)
    gmax = jnp.max(gl, axis=-1, keepdims=True)
    grp = jnp.min(jnp.where(gl == gmax, lane, big), axis=-1, keepdims=True)
    grp_w = 1.0 / jnp.sum(jnp.where(is_grp, jnp.exp(logits - gmax), 0.0), axis=-1, keepdims=True)
    e_lane = lane - MOE_GROUPS
    in_grp = (e_lane >= 0) & (lax.shift_right_logical(jnp.maximum(e_lane, 0), 3) == grp) & (e_lane < MOE_EXPERTS)
    emax = jnp.max(jnp.where(in_grp, logits, -jnp.inf), axis=-1, keepdims=True)
    ee = jnp.where(in_grp, jnp.exp(logits - emax), 0.0)
    prob = jnp.where(in_grp, ee / jnp.sum(ee, axis=-1, keepdims=True), -1.0)
    p1 = jnp.max(prob, axis=-1, keepdims=True)
    j1 = jnp.min(jnp.where(prob == p1, lane, big), axis=-1, keepdims=True)
    prob2 = jnp.where(lane == j1, -1.0, prob)
    p2 = jnp.max(prob2, axis=-1, keepdims=True)
    j2 = jnp.min(jnp.where(prob2 == p2, lane, big), axis=-1, keepdims=True)
    psum = p1 + p2
    route = jnp.where(lane == 0, grp_w * p1 / psum, 0.0)
    route = jnp.where(lane == 1, grp_w * p2 / psum, route)
    route = jnp.where(lane == 2, (j1 - MOE_GROUPS).astype(F32), route)
    route = jnp.where(lane == 3, (j2 - MOE_GROUPS).astype(F32), route)
    route_ref[...] = jnp.broadcast_to(route, route_ref.shape)


def _merge(x2, attn, sgu, s5, bg, w_attn_o, w_sgu_o, w_s5_o, w_mix_o, norm_g, rgw, rgb, rew, reb, tm):
    t = x2.shape[0]
    wr = jnp.concatenate([rgw, rew, jnp.zeros((D_MODEL, LANES - MOE_GROUPS - MOE_EXPERTS), F32)], axis=1)
    wr_hi = wr.astype(BF16)
    wr_lo = (wr - wr_hi.astype(F32)).astype(BF16)
    br = jnp.concatenate([rgb, reb, jnp.zeros((LANES - MOE_GROUPS - MOE_EXPERTS,), F32)]).reshape(1, LANES)
    ws = [w.astype(BF16) for w in (w_attn_o, w_sgu_o, w_s5_o, w_mix_o)]
    g = norm_g.reshape(1, D_MODEL)
    row = lambda width: pl.BlockSpec((tm, width), lambda i: (i, 0))
    full = lambda a: pl.BlockSpec(a.shape, lambda i: (0,) * a.ndim)
    return pl.pallas_call(
        _merge_kernel, grid=(t // tm,),
        in_specs=[row(D_MODEL), row(NSA_WIDTH), row(SGU_WIDTH), row(S5_WIDTH), row(3 * D_MODEL)]
                 + [full(w) for w in ws] + [full(g), full(wr_hi), full(wr_lo), full(br)],
        out_specs=(row(D_MODEL), row(D_MODEL), row(LANES)),
        out_shape=(jax.ShapeDtypeStruct((t, D_MODEL), F32), jax.ShapeDtypeStruct((t, D_MODEL), F32),
                   jax.ShapeDtypeStruct((t, LANES), F32)),
        compiler_params=_cparams("parallel"), name="merge_route",
    )(x2, attn, sgu, s5, bg, *ws, g, wr_hi, wr_lo, br)


def _start_row_gather(src_hbm, rows_of, dst_buf, sem, n):
    for r in range(n):
        pltpu.make_async_copy(src_hbm.at[pl.ds(rows_of(r), 1)], dst_buf.at[pl.ds(r, 1)], sem).start()


def _wait_row_gather(src_hbm, dst_buf, sem, n):
    pltpu.make_async_copy(src_hbm.at[pl.ds(0, n)], dst_buf, sem).wait()


def _moe_kernel(blk_exp_ref, first_ref, next_ref, h_hbm, wg_ref, wu_ref, wd_ref, y_ref, xbuf, gsem, *, bm):
    del blk_exp_ref
    i = pl.program_id(0)
    cur = lax.rem(i, 2)
    nxt = 1 - cur

    def token_of(ref):
        return lambda r: lax.shift_right_logical(jnp.maximum(ref[0, 0, r], 0), 1)

    @pl.when(i == 0)
    def _():
        _start_row_gather(h_hbm, token_of(first_ref), xbuf.at[0], gsem.at[0], bm)

    _start_row_gather(h_hbm, token_of(next_ref), xbuf.at[nxt], gsem.at[nxt], bm)
    _wait_row_gather(h_hbm, xbuf.at[cur], gsem.at[cur], bm)
    xb = xbuf[cur].astype(BF16)
    gate = _dot(xb, wg_ref[0])
    hid = gate * _sigmoid(gate) * _dot(xb, wu_ref[0])
    y_ref[...] = _dot(hid.astype(BF16), wd_ref[0])

    @pl.when(i == pl.num_programs(0) - 1)
    def _():
        _wait_row_gather(h_hbm, xbuf.at[nxt], gsem.at[nxt], bm)


def _dispatch_plan(expert, bm):
    tk = expert.shape[0]
    nb = tk // bm + MOE_EXPERTS
    onehot = (expert[:, None] == jnp.arange(MOE_EXPERTS, dtype=jnp.int32)[None, :]).astype(jnp.int32)
    csum = jnp.cumsum(onehot, axis=0)
    rank = jnp.sum((csum - onehot) * onehot, axis=1)
    counts = csum[-1]
    nblk = (counts + bm - 1) // bm
    blk_end = jnp.cumsum(nblk)
    blk_start = blk_end - nblk
    dest = blk_start[expert] * bm + rank
    slots = jnp.full((nb * bm,), -1, jnp.int32).at[dest].set(jnp.arange(tk, dtype=jnp.int32))
    b = jnp.arange(nb, dtype=jnp.int32)
    blk_exp = jnp.minimum(jnp.searchsorted(blk_end, b, side="right").astype(jnp.int32), MOE_EXPERTS - 1)
    return blk_exp, slots.reshape(nb, 1, bm), dest.astype(jnp.int32)


def _experts(h, route, w_gate, w_up, w_down, bm):
    expert = route[:, 2:4].astype(jnp.int32).reshape(-1)
    blk_exp, slots, dest = _dispatch_plan(expert, bm)
    nb = slots.shape[0]
    wspec = lambda shape: pl.BlockSpec((1,) + shape, lambda i, be: (be[i], 0, 0))
    smem_rows = lambda index_map: pl.BlockSpec((1, 1, bm), index_map, memory_space=pltpu.SMEM)
    grid_spec = pltpu.PrefetchScalarGridSpec(
        num_scalar_prefetch=1, grid=(nb,),
        in_specs=[smem_rows(lambda i, be: (0, 0, 0)),
                  smem_rows(lambda i, be: (jnp.minimum(i + 1, nb - 1), 0, 0)),
                  pl.BlockSpec(memory_space=pl.ANY),
                  wspec((D_MODEL, EXPERT_HIDDEN)), wspec((D_MODEL, EXPERT_HIDDEN)),
                  wspec((EXPERT_HIDDEN, D_MODEL))],
        out_specs=pl.BlockSpec((bm, D_MODEL), lambda i, be: (i, 0)),
        scratch_shapes=[pltpu.VMEM((2, bm, D_MODEL), F32), pltpu.SemaphoreType.DMA((2,))])
    y_blocks = pl.pallas_call(
        functools.partial(_moe_kernel, bm=bm), grid_spec=grid_spec,
        out_shape=jax.ShapeDtypeStruct((nb * bm, D_MODEL), F32),
        compiler_params=_cparams("arbitrary"), name="moe_experts",
    )(blk_exp, slots, slots, h, w_gate.astype(BF16), w_up.astype(BF16), w_down.astype(BF16))
    return y_blocks, dest


def _combine_kernel(first_ref, next_ref, x_ref, route_ref, g_ref, y_hbm, o_ref, ybuf, sem, *, tm, final_norm):
    i = pl.program_id(0)
    cur = lax.rem(i, 2)
    nxt = 1 - cur
    n = MOE_TOPK * tm

    def row_of(ref):
        return lambda r: ref[0, 0, r]

    @pl.when(i == 0)
    def _():
        _start_row_gather(y_hbm, row_of(first_ref), ybuf.at[0], sem.at[0], n)

    _start_row_gather(y_hbm, row_of(next_ref), ybuf.at[nxt], sem.at[nxt], n)
    _wait_row_gather(y_hbm, ybuf.at[cur], sem.at[cur], n)
    route = route_ref[...]
    x = x_ref[...] + route[:, 0:1] * ybuf[cur, 0:tm, :] + route[:, 1:2] * ybuf[cur, tm:n, :]
    o_ref[...] = _rms(x, g_ref[...]) if final_norm else x

    @pl.when(i == pl.num_programs(0) - 1)
    def _():
        _wait_row_gather(y_hbm, ybuf.at[nxt], sem.at[nxt], n)


def _combine(x2, y_blocks, dest, route, g, final_norm, tm):
    t = x2.shape[0]
    steps = t // tm
    rows = jnp.transpose(dest.reshape(steps, tm, MOE_TOPK), (0, 2, 1)).reshape(steps, 1, MOE_TOPK * tm)
    g2 = g.reshape(1, D_MODEL)
    row = lambda width: pl.BlockSpec((tm, width), lambda i: (i, 0))
    smem_rows = lambda index_map: pl.BlockSpec((1, 1, MOE_TOPK * tm), index_map, memory_space=pltpu.SMEM)
    return pl.pallas_call(
        functools.partial(_combine_kernel, tm=tm, final_norm=final_norm), grid=(steps,),
        in_specs=[smem_rows(lambda i: (0, 0, 0)), smem_rows(lambda i: (jnp.minimum(i + 1, steps - 1), 0, 0)),
                  row(D_MODEL), row(LANES), pl.BlockSpec((1, D_MODEL), lambda i: (0, 0)),
                  pl.BlockSpec(memory_space=pl.ANY)],
        out_specs=row(D_MODEL), out_shape=jax.ShapeDtypeStruct((t, D_MODEL), F32),
        scratch_shapes=[pltpu.VMEM((2, MOE_TOPK * tm, D_MODEL), F32), pltpu.SemaphoreType.DMA((2,))],
        compiler_params=_cparams("arbitrary"), name="moe_combine",
    )(rows, rows, x2, route, g2, y_blocks)


def _tiles(batch, seq):
    t = batch * seq
    return dict(
        tm_proj=min(256, t), tq=min(128, seq), tk=min(256, seq), tm_sgu=min(512, t),
        s5_steps=min(128, seq), tm_merge=min(256, t), bm=256, tm_comb=min(256, t))


def kernel(x, positions, norm_mix_g, w_in, cmp_pos_k, cmp_pos_v, cmp_k_w1, cmp_k_w2, cmp_v_w1, cmp_v_w2, w_attn_o, sgu_norm_g, sgu_w, sgu_b, w_sgu_o, s5_a_re, s5_a_im, s5_log_step, s5_b_re, s5_b_im, s5_c_re, s5_c_im, s5_d, s5_glu_w, s5_glu_b, w_s5_o, w_mix_o, norm_ffn_g, router_group_w, router_group_b, router_expert_w, router_expert_b, expert_w_gate, expert_w_up, expert_w_down, norm_final_g):
    batch, seq, _ = x.shape
    depth = w_in.shape[0]
    cfg = _tiles(batch, seq)
    tables = _rope_tables(positions)
    x2 = x.reshape(batch * seq, D_MODEL)
    for l in range(depth):
        qt, qrt, kvc, krow, vt, gt, sgu_in, s5_in, bg = _in_proj(
            x2, norm_mix_g[l].reshape(1, D_MODEL), _prep_in_proj(w_in[l]), tables, cfg["tm_proj"])
        kc, vct = _compress(kvc, _prep_compress(cmp_pos_k[l], cmp_pos_v[l], cmp_k_w1[l], cmp_k_w2[l],
                                                cmp_v_w1[l], cmp_v_w2[l]), batch, seq)
        attn = _attention(qt, qrt, gt, kc, vct, krow, vt, batch, seq, cfg["tq"], cfg["tk"])
        sgu = _sgu(sgu_in, sgu_norm_g[l], sgu_w[l], sgu_b[l], cfg["tm_sgu"])
        s5 = _s5(s5_in, _prep_s5(s5_a_re[l], s5_a_im[l], s5_log_step[l], s5_b_re[l], s5_b_im[l],
                                 s5_c_re[l], s5_c_im[l]),
                 s5_d[l], s5_glu_w[l], s5_glu_b[l], batch, seq, cfg["s5_steps"])
        x_mid, h, route = _merge(x2, attn, sgu, s5, bg, w_attn_o[l], w_sgu_o[l], w_s5_o[l], w_mix_o[l],
                                 norm_ffn_g[l], router_group_w[l], router_group_b[l],
                                 router_expert_w[l], router_expert_b[l], cfg["tm_merge"])
        y_blocks, dest = _experts(h, route, expert_w_gate[l], expert_w_up[l], expert_w_down[l], cfg["bm"])
        x2 = _combine(x_mid, y_blocks, dest, route, norm_final_g, l == depth - 1, cfg["tm_comb"])
    return x2.reshape(batch, seq, D_MODEL)
```

```python
import functools
import math

import numpy as np
import jax
import jax.numpy as jnp
from jax import lax
from jax.experimental import pallas as pl
from jax.experimental.pallas import tpu as pltpu

F32 = jnp.float32
BF16 = jnp.bfloat16

D_MODEL = 1024
HEAD_DIM = 64
NSA_HEADS = 8
NSA_KV_GROUPS = 2
HEADS_PER_GROUP = NSA_HEADS // NSA_KV_GROUPS
GROUP_WIDTH = HEADS_PER_GROUP * HEAD_DIM
NSA_WIDTH = NSA_HEADS * HEAD_DIM
KV_WIDTH = NSA_KV_GROUPS * HEAD_DIM
ROPE_DIM = HEAD_DIM // 4
ROPE_HALF = ROPE_DIM // 2
ROPE_THETA = 500000.0
CMP_LEN = 32
CMP_STRIDE = 16
CMP_HIDDEN = 128
SLC_LEN = 64
SLC_SHIFT = int(math.log2(SLC_LEN))
SLC_TOP = 16
WINDOW = 512
FORCE_SCORE = 1.0e4
NEG_INF = -1.0e30
SGU_WIDTH = 256
SGU_GROUPS = 4
SGU_CHUNK = 128
S5_WIDTH = 256
S5_GROUP_CH = 16
S5_GROUPS = S5_WIDTH // S5_GROUP_CH
S5_STATE = 64
S5_LANES = S5_GROUPS * S5_STATE
MOE_GROUPS = 4
MOE_EXPERTS_PER_GROUP = 8
MOE_EXPERTS = MOE_GROUPS * MOE_EXPERTS_PER_GROUP
MOE_TOPK = 2
EXPERT_HIDDEN = 512
RMS_EPS = 1e-6
ATTN_SCALE = HEAD_DIM ** -0.5
LOG2_E = math.log2(math.e)

LANES = 128
SUBLANES = 8
VMEM_LIMIT_BYTES = 56 * 1024 * 1024

GATE_COLS = HEADS_PER_GROUP * 3
GATE_ROWS = 16

_OFF_Q = 0
_OFF_KVC = _OFF_Q + NSA_WIDTH
_OFF_KS = _OFF_KVC + 2 * KV_WIDTH
_OFF_VS = _OFF_KS + KV_WIDTH
_OFF_KW = _OFF_VS + KV_WIDTH
_OFF_VW = _OFF_KW + KV_WIDTH
_OFF_GATE = _OFF_VW + KV_WIDTH
_OFF_SGU = _OFF_GATE + LANES
_OFF_S5 = _OFF_SGU + 2 * SGU_WIDTH
_OFF_BG = _OFF_S5 + S5_WIDTH
_IN_W = _OFF_BG + 3 * D_MODEL


def _cparams(*sem):
    return pltpu.CompilerParams(dimension_semantics=sem, vmem_limit_bytes=VMEM_LIMIT_BYTES)


def _gelu(x):
    return 0.5 * x * (1.0 + jnp.tanh(math.sqrt(2.0 / math.pi) * (x + 0.044715 * (x * x * x))))


def _sigmoid(x):
    return 1.0 / (1.0 + jnp.exp(-x))


def _dot(a, b):
    return jnp.dot(a, b, preferred_element_type=F32)


def _dot_nt(a, b):
    return lax.dot_general(a, b, (((1,), (1,)), ((), ())), preferred_element_type=F32)


def _split_bf16(x):
    hi = x.astype(BF16)
    lo = (x - hi.astype(F32)).astype(BF16)
    return hi, lo


def _rms(x, g):
    return x * lax.rsqrt(jnp.mean(x * x, axis=-1, keepdims=True) + RMS_EPS) * g


def _rope(x, c, s_lo, s_hi):
    n = x.shape[-1]
    return x * c + pltpu.roll(x, n - ROPE_HALF, 1) * s_lo + pltpu.roll(x, ROPE_HALF, 1) * s_hi


def _in_proj_kernel(x_ref, g_ref, w_ref, c_ref, slo_ref, shi_ref,
                    qt_ref, qrt_ref, kvc_ref, krow_ref, vt_ref, gt_ref, sgu_ref, s5_ref, bg_ref):
    tm = x_ref.shape[0]
    x = x_ref[...]
    hb = _rms(x, g_ref[...]).astype(BF16)

    def proj(off, width):
        return _dot(hb, w_ref[:, off:off + width])

    c, s_lo, s_hi = c_ref[...], slo_ref[...], shi_ref[...]
    rep = NSA_WIDTH // LANES
    q = proj(_OFF_Q, NSA_WIDTH)
    qt_ref[...] = (q * ATTN_SCALE).T
    qrot = _rope(q, jnp.concatenate([c] * rep, axis=1), jnp.concatenate([s_lo] * rep, axis=1),
                 jnp.concatenate([s_hi] * rep, axis=1))
    qrt_ref[...] = (qrot * (ATTN_SCALE * LOG2_E)).T.astype(BF16)
    kvc_ref[...] = proj(_OFF_KVC, 2 * KV_WIDTH)
    ks = _rope(proj(_OFF_KS, KV_WIDTH), c, s_lo, s_hi)
    kw = _rope(proj(_OFF_KW, KV_WIDTH), c, s_lo, s_hi)
    for p, piece in enumerate((ks, kw)):
        for g in range(NSA_KV_GROUPS):
            krow_ref[p * NSA_KV_GROUPS + g] = piece[:, g * HEAD_DIM:(g + 1) * HEAD_DIM].astype(BF16)
    for p, off in enumerate((_OFF_VS, _OFF_VW)):
        vt = proj(off, KV_WIDTH).T.astype(BF16)
        for g in range(NSA_KV_GROUPS):
            for ch in range(tm // LANES):
                vt_ref[p * NSA_KV_GROUPS + g, ch] = vt[g * HEAD_DIM:(g + 1) * HEAD_DIM, ch * LANES:(ch + 1) * LANES]
    gt_ref[...] = _sigmoid(proj(_OFF_GATE, LANES)).T[:NSA_KV_GROUPS * GATE_ROWS]
    sgu_ref[...] = proj(_OFF_SGU, 2 * SGU_WIDTH)
    s5_ref[...] = proj(_OFF_S5, S5_WIDTH)
    for k in range(3):
        bg_ref[:, k * D_MODEL:(k + 1) * D_MODEL] = _sigmoid(
            proj(_OFF_BG + k * D_MODEL, D_MODEL)).astype(BF16)


def _prep_in_proj(w_in):
    sizes = (NSA_WIDTH,) + (KV_WIDTH,) * 6 + (3 * NSA_HEADS, 2 * SGU_WIDTH, S5_WIDTH, 3 * D_MODEL)
    offs = np.concatenate([[0], np.cumsum(sizes)])
    q, kc, vc, ks, vs, kw, vw, ng, sgu, s5, bg = [w_in[:, offs[i]:offs[i + 1]] for i in range(11)]
    pad = lambda n: jnp.zeros((D_MODEL, n), w_in.dtype)
    gates = []
    for g in range(NSA_KV_GROUPS):
        gates += [ng[:, g * GATE_COLS:(g + 1) * GATE_COLS], pad(GATE_ROWS - GATE_COLS)]
    gates.append(pad(LANES - NSA_KV_GROUPS * GATE_ROWS))
    return jnp.concatenate([q, kc, vc, ks, vs, kw, vw] + gates + [sgu, s5, bg], axis=1).astype(BF16)


def _rope_tables(positions):
    inv_freq = ROPE_THETA ** (-jnp.arange(ROPE_HALF, dtype=F32) / ROPE_HALF)
    ang = positions.astype(F32).reshape(-1, 1) * inv_freq
    cos, sin = jnp.cos(ang), jnp.sin(ang)
    t = ang.shape[0]
    rest = HEAD_DIM - ROPE_DIM
    c = jnp.concatenate([cos, cos, jnp.ones((t, rest), F32)], axis=1)
    s_lo = jnp.concatenate([-sin, jnp.zeros((t, HEAD_DIM - ROPE_HALF), F32)], axis=1)
    s_hi = jnp.concatenate([jnp.zeros((t, ROPE_HALF), F32), sin, jnp.zeros((t, rest), F32)], axis=1)
    rep = LANES // HEAD_DIM
    return tuple(jnp.concatenate([a] * rep, axis=1) for a in (c, s_lo, s_hi))


def _in_proj(x2, g, w, tables, tm):
    t = x2.shape[0]
    row = lambda width: pl.BlockSpec((tm, width), lambda i: (i, 0))
    col = lambda height: pl.BlockSpec((height, tm), lambda i: (0, i))
    full = lambda a: pl.BlockSpec(a.shape, lambda i: (0,) * a.ndim)
    n_kv = 2 * NSA_KV_GROUPS
    out_shape = (
        jax.ShapeDtypeStruct((NSA_WIDTH, t), F32),
        jax.ShapeDtypeStruct((NSA_WIDTH, t), BF16),
        jax.ShapeDtypeStruct((t, 2 * KV_WIDTH), F32),
        jax.ShapeDtypeStruct((n_kv, t, HEAD_DIM), BF16),
        jax.ShapeDtypeStruct((n_kv, t // LANES, HEAD_DIM, LANES), BF16),
        jax.ShapeDtypeStruct((NSA_KV_GROUPS * GATE_ROWS, t), F32),
        jax.ShapeDtypeStruct((t, 2 * SGU_WIDTH), F32),
        jax.ShapeDtypeStruct((t, S5_WIDTH), F32),
        jax.ShapeDtypeStruct((t, 3 * D_MODEL), BF16),
    )
    out_specs = (col(NSA_WIDTH), col(NSA_WIDTH), row(2 * KV_WIDTH),
                 pl.BlockSpec((n_kv, tm, HEAD_DIM), lambda i: (0, i, 0)),
                 pl.BlockSpec((n_kv, tm // LANES, HEAD_DIM, LANES), lambda i: (0, i, 0, 0)),
                 col(NSA_KV_GROUPS * GATE_ROWS), row(2 * SGU_WIDTH), row(S5_WIDTH), row(3 * D_MODEL))
    return pl.pallas_call(
        _in_proj_kernel, grid=(t // tm,),
        in_specs=[row(D_MODEL), full(g), full(w), row(LANES), row(LANES), row(LANES)],
        out_specs=out_specs, out_shape=out_shape,
        compiler_params=_cparams("parallel"), name="in_proj",
    )(x2, g, w, *tables)


def _compress_kernel(r_ref, wa_ref, wb_ref, pos_ref, w1_ref, w2_ref, w2t_ref, kc_ref, vct_ref):
    rb = r_ref[0].astype(BF16)
    n = rb.shape[0]
    a = _dot(rb, wa_ref[...])
    b = _dot(rb, wb_ref[...])
    b = pltpu.roll(b, n - 1, 0)
    rows = lax.broadcasted_iota(jnp.int32, b.shape, 0)
    b = jnp.where(rows == n - 1, 0.0, b)
    for m in range(2 * NSA_KV_GROUPS):
        typ, g = divmod(m, NSA_KV_GROUPS)
        bias = _dot(pos_ref[typ].astype(BF16), w1_ref[typ])
        hid = a[:, m * CMP_HIDDEN:(m + 1) * CMP_HIDDEN] + b[:, m * CMP_HIDDEN:(m + 1) * CMP_HIDDEN] + bias
        act = _gelu(hid).astype(BF16)
        if typ == 0:
            kc_ref[0, g] = _dot(act, w2_ref[...])
        else:
            vct_ref[0, g] = _dot_nt(w2t_ref[...], act)


def _prep_compress(cmp_pos_k, cmp_pos_v, k_w1, k_w2, v_w1, v_w2):
    streams = 2 * NSA_KV_GROUPS
    w1 = jnp.stack([k_w1, v_w1]).reshape(2, CMP_LEN, HEAD_DIM, CMP_HIDDEN)

    def expand(half):
        out = jnp.zeros((CMP_STRIDE, streams, HEAD_DIM, streams, CMP_HIDDEN), F32)
        for m in range(streams):
            out = out.at[:, m, :, m, :].set(w1[m // NSA_KV_GROUPS, half * CMP_STRIDE:(half + 1) * CMP_STRIDE])
        return out.reshape(CMP_STRIDE * streams * HEAD_DIM, streams * CMP_HIDDEN).astype(BF16)

    pos = jnp.stack([cmp_pos_k, cmp_pos_v]).reshape(2, 1, CMP_LEN * HEAD_DIM)
    return (expand(0), expand(1), pos, jnp.stack([k_w1, v_w1]).astype(BF16),
            k_w2.astype(BF16), v_w2.T.astype(BF16))


def _compress(kvc, prep, batch, seq):
    n = seq // CMP_STRIDE
    r = kvc.reshape(batch, n, CMP_STRIDE * 2 * KV_WIDTH)
    full = lambda a: pl.BlockSpec(a.shape, lambda b: (0,) * a.ndim)
    return pl.pallas_call(
        _compress_kernel, grid=(batch,),
        in_specs=[pl.BlockSpec((1, n, r.shape[-1]), lambda b: (b, 0, 0))] + [full(a) for a in prep],
        out_specs=(pl.BlockSpec((1, NSA_KV_GROUPS, n, HEAD_DIM), lambda b: (b, 0, 0, 0)),
                   pl.BlockSpec((1, NSA_KV_GROUPS, HEAD_DIM, n), lambda b: (b, 0, 0, 0))),
        out_shape=(jax.ShapeDtypeStruct((batch, NSA_KV_GROUPS, n, HEAD_DIM), F32),
                   jax.ShapeDtypeStruct((batch, NSA_KV_GROUPS, HEAD_DIM, n), F32)),
        compiler_params=_cparams("parallel"), name="compress_kv",
    )(r, *prep)


def _attn_kernel(qt_ref, qrt_ref, gt_ref, kc_ref, vct_ref, ks_ref, vst_ref, kw_ref, vwt_ref, ovt_ref,
                 o_ref, bias_sc, m_sc, l_sc, acc_sc, ow_sc, s_sc, p_sc, a_sc, *, tq, tk, seq):
    n_cmp = kc_ref.shape[2]
    n_slc = seq // SLC_LEN
    heads = HEADS_PER_GROUP
    t0 = pl.program_id(2) * tq
    t_lane = t0 + lax.broadcasted_iota(jnp.int32, (1, tq), 1)

    def lane_tile(a):
        return jnp.concatenate([a] * heads, axis=1)

    def head_cols(ref):
        return jnp.concatenate([ref[r * HEAD_DIM:(r + 1) * HEAD_DIM, :] for r in range(heads)], axis=1)

    q_hi, q_lo = _split_bf16(head_cols(qt_ref))
    qr = head_cols(qrt_ref)

    slab = min(WINDOW + tq, seq)
    st = pl.multiple_of(jnp.clip(t0 - WINDOW, 0, seq - slab), LANES)
    s = _dot(kw_ref[0, pl.ds(st, slab), :], qr)
    diff = t_lane - (st + lax.broadcasted_iota(jnp.int32, (slab, 1), 0))
    s = s + lane_tile(jnp.where((diff >= 0) & (diff < WINDOW), 0.0, NEG_INF))
    p = jnp.exp2(s - jnp.max(s, axis=0, keepdims=True))
    st_tile = st // LANES
    vw_t = jnp.concatenate([vwt_ref[0, st_tile + c] for c in range(slab // LANES)], axis=1)
    ow_sc[...] = _dot(vw_t, p.astype(BF16)) / jnp.sum(p, axis=0, keepdims=True)

    kc_hi, kc_lo = _split_bf16(kc_ref[0, 0])
    s = _dot(kc_hi, q_hi) + _dot(kc_lo, q_hi) + _dot(kc_hi, q_lo)
    cmp_end = lax.broadcasted_iota(jnp.int32, (n_cmp, 1), 0) * CMP_STRIDE + (CMP_LEN - 1)
    m_c = cmp_end <= lane_tile(t_lane)
    s = jnp.where(m_c, s, NEG_INF)
    e = jnp.where(m_c, jnp.exp(s - jnp.max(s, axis=0, keepdims=True)), 0.0)
    p = e / jnp.maximum(jnp.sum(e, axis=0, keepdims=True), 1e-30)
    o_cmp = _dot(vct_ref[0, 0].astype(BF16), p.astype(BF16))
    p_sum = p[:, 0:tq]
    for r in range(1, heads):
        p_sum = p_sum + p[:, r * tq:(r + 1) * tq]
    ps_hi, ps_lo = _split_bf16(p_sum)
    imp = _dot(ovt_ref[...], ps_hi) + _dot(ovt_ref[...], ps_lo)

    j_idx = lax.broadcasted_iota(jnp.int32, (n_slc, 1), 0)
    blk_t = lax.shift_right_logical(t_lane, SLC_SHIFT)
    causal = j_idx <= blk_t
    forced = (j_idx == 0) | (j_idx == blk_t) | (j_idx == blk_t - 1)
    imp = jnp.where(forced, FORCE_SCORE, imp)
    imp = jnp.where(causal, imp, NEG_INF)
    picked = jnp.zeros((n_slc, tq), F32)
    j_f32 = j_idx.astype(F32)
    for _ in range(min(SLC_TOP, n_slc)):
        best = jnp.max(imp, axis=0, keepdims=True)
        first = jnp.min(jnp.where(imp == best, j_f32, float(n_slc)), axis=0, keepdims=True)
        hit = j_f32 == first
        picked = jnp.where(hit, 1.0, picked)
        imp = jnp.where(hit, -jnp.inf, imp)
    bias_sc[...] = jnp.where((picked > 0.5) & causal, 0.0, NEG_INF)

    m_sc[...] = jnp.full(m_sc.shape, NEG_INF, F32)
    l_sc[...] = jnp.zeros(l_sc.shape, F32)
    acc_sc[...] = jnp.zeros(acc_sc.shape, F32)
    p_sc[1] = jnp.zeros(p_sc.shape[1:], BF16)
    a_sc[1] = jnp.ones(a_sc.shape[1:], F32)
    blocks_per_tile = tk // SLC_LEN
    lanes_per_tile = tk // LANES

    def scores(j):
        k0 = j * tk if isinstance(j, int) else pl.multiple_of(j * tk, tk)
        s_sc[j & 1] = _dot(ks_ref[0, pl.ds(k0, tk), :], qr)

    def softmax(j):
        slot = j & 1
        bias = jnp.concatenate(
            [jnp.broadcast_to(bias_sc[pl.ds(j * blocks_per_tile + b, 1), :], (SLC_LEN, tq))
             for b in range(blocks_per_tile)], axis=0)
        key_pos = j * tk + lax.broadcasted_iota(jnp.int32, (tk, 1), 0)
        bias = jnp.where(key_pos <= t_lane, bias, NEG_INF)
        s = s_sc[slot] + lane_tile(bias)
        m_old = m_sc[...]
        m_new = jnp.maximum(m_old, jnp.max(s, axis=0, keepdims=True))
        alpha = jnp.exp2(m_old - m_new)
        p = jnp.exp2(s - m_new)
        l_sc[...] = alpha * l_sc[...] + jnp.sum(p, axis=0, keepdims=True)
        m_sc[...] = m_new
        a_sc[slot] = alpha
        p_sc[slot] = p.astype(BF16)

    def accumulate(j):
        slot = j & 1
        tile = jnp.maximum(j, 0) * lanes_per_tile
        v_t = jnp.concatenate([vst_ref[0, tile + c] for c in range(lanes_per_tile)], axis=1)
        acc_sc[...] = a_sc[slot] * acc_sc[...] + _dot(v_t, p_sc[slot])

    n_kv = (t0 + tq + tk - 1) // tk
    scores(0)

    def pipelined(i, carry):
        accumulate(i - 2)
        softmax(i - 1)
        scores(i)
        return carry

    lax.fori_loop(1, n_kv, pipelined, 0)
    accumulate(n_kv - 2)
    softmax(n_kv - 1)
    accumulate(n_kv - 1)
    o_slc = acc_sc[...] / l_sc[...]
    o_win = ow_sc[...]

    gates = gt_ref[...]
    outs = []
    for r in range(heads):
        cols = slice(r * tq, (r + 1) * tq)
        outs.append(gates[3 * r:3 * r + 1, :] * o_cmp[:, cols] + gates[3 * r + 1:3 * r + 2, :] * o_slc[:, cols]
                    + gates[3 * r + 2:3 * r + 3, :] * o_win[:, cols])
    o_ref[...] = jnp.concatenate(outs, axis=0).T.astype(o_ref.dtype)


def _overlap_matrix_t(n_cmp, n_slc):
    c0 = np.arange(n_cmp)[None, :] * CMP_STRIDE
    s0 = np.arange(n_slc)[:, None] * SLC_LEN
    ov = np.clip(np.minimum(c0 + CMP_LEN, s0 + SLC_LEN) - np.maximum(c0, s0), 0, None) / CMP_LEN
    ov[:, n_cmp - 1] = 0.0
    return jnp.asarray(ov, BF16)


def _attention(qt, qrt, gt, kc, vct, krow, vt, batch, seq, tq, tk):
    n_cmp = seq // CMP_STRIDE
    n_slc = seq // SLC_LEN
    nq = seq // tq
    krow = krow.reshape(2 * NSA_KV_GROUPS, batch, seq, HEAD_DIM)
    vt = vt.reshape(2 * NSA_KV_GROUPS, batch, seq // LANES, HEAD_DIM, LANES)
    ovt = _overlap_matrix_t(n_cmp, n_slc)
    qspec = pl.BlockSpec((GROUP_WIDTH, tq), lambda b, g, i: (g, b * nq + i))
    k_spec = lambda p: pl.BlockSpec((None, 1, seq, HEAD_DIM), lambda b, g, i: (p * NSA_KV_GROUPS + g, b, 0, 0))
    v_spec = lambda p: pl.BlockSpec((None, 1, seq // LANES, HEAD_DIM, LANES),
                                    lambda b, g, i: (p * NSA_KV_GROUPS + g, b, 0, 0, 0))
    kernel = functools.partial(_attn_kernel, tq=tq, tk=tk, seq=seq)
    return pl.pallas_call(
        kernel, grid=(batch, NSA_KV_GROUPS, nq),
        in_specs=[qspec, qspec, pl.BlockSpec((GATE_ROWS, tq), lambda b, g, i: (g, b * nq + i)),
                  pl.BlockSpec((1, 1, n_cmp, HEAD_DIM), lambda b, g, i: (b, g, 0, 0)),
                  pl.BlockSpec((1, 1, HEAD_DIM, n_cmp), lambda b, g, i: (b, g, 0, 0)),
                  k_spec(0), v_spec(0), k_spec(1), v_spec(1),
                  pl.BlockSpec(ovt.shape, lambda b, g, i: (0, 0))],
        out_specs=pl.BlockSpec((tq, GROUP_WIDTH), lambda b, g, i: (b * nq + i, g)),
        out_shape=jax.ShapeDtypeStruct((batch * seq, NSA_WIDTH), BF16),
        scratch_shapes=[pltpu.VMEM((n_slc, tq), F32),
                        pltpu.VMEM((1, HEADS_PER_GROUP * tq), F32),
                        pltpu.VMEM((1, HEADS_PER_GROUP * tq), F32),
                        pltpu.VMEM((HEAD_DIM, HEADS_PER_GROUP * tq), F32),
                        pltpu.VMEM((HEAD_DIM, HEADS_PER_GROUP * tq), F32),
                        pltpu.VMEM((2, tk, HEADS_PER_GROUP * tq), F32),
                        pltpu.VMEM((2, tk, HEADS_PER_GROUP * tq), BF16),
                        pltpu.VMEM((2, 1, HEADS_PER_GROUP * tq), F32)],
        compiler_params=_cparams("parallel", "parallel", "arbitrary"), name="nsa_attention",
    )(qt, qrt, gt, kc, vct, krow, vt, krow, vt, ovt)


def _sgu_kernel(uv_ref, g_ref, w_ref, b_ref, o_ref, *, chunks):
    z = _gelu(uv_ref[...])
    u = z[:, :SGU_WIDTH]
    v = _rms(z[:, SGU_WIDTH:], g_ref[...])
    gw = SGU_WIDTH // SGU_GROUPS
    rows = lax.broadcasted_iota(jnp.int32, (SGU_CHUNK, SGU_GROUPS * SGU_CHUNK), 0)
    cols = lax.broadcasted_iota(jnp.int32, (SGU_CHUNK, SGU_GROUPS * SGU_CHUNK), 1)
    w = jnp.where((cols & (SGU_CHUNK - 1)) <= rows, w_ref[...], 0.0).astype(BF16)
    grp_r = lax.broadcasted_iota(jnp.int32, (SGU_GROUPS * SGU_CHUNK, SGU_WIDTH), 0) // SGU_CHUNK
    grp_c = lax.broadcasted_iota(jnp.int32, (SGU_GROUPS * SGU_CHUNK, SGU_WIDTH), 1) // gw
    for c in range(chunks):
        vc = v[c * SGU_CHUNK:(c + 1) * SGU_CHUNK].astype(BF16)
        v_bd = jnp.where(grp_r == grp_c, jnp.concatenate([vc] * SGU_GROUPS, axis=0), jnp.zeros((), BF16))
        mixed = _dot(w, v_bd) + b_ref[...]
        o_ref[c * SGU_CHUNK:(c + 1) * SGU_CHUNK, :] = (u[c * SGU_CHUNK:(c + 1) * SGU_CHUNK] * mixed).astype(o_ref.dtype)


def _sgu(sgu_in, norm_g, w_s, b_s, tm):
    t = sgu_in.shape[0]
    w_cat = jnp.transpose(w_s, (1, 0, 2)).reshape(SGU_CHUNK, SGU_GROUPS * SGU_CHUNK)
    bias = jnp.repeat(b_s.T, SGU_WIDTH // SGU_GROUPS, axis=1)
    g = norm_g.reshape(1, SGU_WIDTH)
    full = lambda a: pl.BlockSpec(a.shape, lambda i: (0,) * a.ndim)
    return pl.pallas_call(
        functools.partial(_sgu_kernel, chunks=tm // SGU_CHUNK), grid=(t // tm,),
        in_specs=[pl.BlockSpec((tm, 2 * SGU_WIDTH), lambda i: (i, 0)), full(g), full(w_cat), full(bias)],
        out_specs=pl.BlockSpec((tm, SGU_WIDTH), lambda i: (i, 0)),
        out_shape=jax.ShapeDtypeStruct((t, SGU_WIDTH), BF16),
        compiler_params=_cparams("parallel"), name="sgu",
    )(sgu_in, g, w_cat, bias)


def _s5_kernel(x_ref, bre_ref, bim_ref, are_ref, aim_ref, cre_ref, cim_ref, d_ref, gw_ref, gb_ref,
               o_ref, hre_sc, him_sc, ure_sc, uim_sc, *, steps, batch):
    @pl.when(pl.program_id(0) == 0)
    def _():
        hre_sc[...] = jnp.zeros(hre_sc.shape, F32)
        him_sc[...] = jnp.zeros(him_sc.shape, F32)

    x = x_ref[...]
    xb = x.astype(BF16)
    ure_sc[...] = _dot(xb, bre_ref[...])
    uim_sc[...] = _dot(xb, bim_ref[...])
    a_re = jnp.broadcast_to(are_ref[...], (batch, S5_LANES))
    a_im = jnp.broadcast_to(aim_ref[...], (batch, S5_LANES))

    def step(t, carry):
        h_re, h_im = carry
        rows = pl.ds(pl.multiple_of(t * batch, batch), batch)
        n_re = a_re * h_re - a_im * h_im + ure_sc[rows, :]
        n_im = a_re * h_im + a_im * h_re + uim_sc[rows, :]
        ure_sc[rows, :] = n_re
        uim_sc[rows, :] = n_im
        return n_re, n_im

    h_re, h_im = lax.fori_loop(0, steps, step, (hre_sc[...], him_sc[...]))
    hre_sc[...] = h_re
    him_sc[...] = h_im
    y = _dot(ure_sc[...].astype(BF16), cre_ref[...]) - _dot(uim_sc[...].astype(BF16), cim_ref[...])
    y = _gelu(y + d_ref[...] * x)
    y = y * _sigmoid(_dot(y.astype(BF16), gw_ref[...]) + gb_ref[...])
    o_ref[...] = y.astype(o_ref.dtype)


def _block_diag(blocks):
    g, r, c = blocks.shape
    eye = jnp.eye(g, dtype=blocks.dtype)
    return (blocks[:, :, None, :] * eye[:, None, :, None]).reshape(g * r, g * c)


def _prep_s5(a_re, a_im, log_step, b_re, b_im, c_re, c_im):
    step = jnp.exp(log_step)[:, None]
    mag = jnp.exp(a_re * step)
    abar_re, abar_im = mag * jnp.cos(a_im * step), mag * jnp.sin(a_im * step)
    den = a_re * a_re + a_im * a_im
    nr, ni = abar_re - 1.0, abar_im
    coef_re = (nr * a_re + ni * a_im) / den
    coef_im = (ni * a_re - nr * a_im) / den
    bbar_re = coef_re[..., None] * b_re - coef_im[..., None] * b_im
    bbar_im = coef_re[..., None] * b_im + coef_im[..., None] * b_re
    to_in = lambda b: _block_diag(jnp.transpose(b, (0, 2, 1))).astype(BF16)
    to_out = lambda c: _block_diag(jnp.transpose(c, (0, 2, 1))).astype(BF16)
    return (to_in(bbar_re), to_in(bbar_im), abar_re.reshape(1, S5_LANES), abar_im.reshape(1, S5_LANES),
            to_out(c_re), to_out(c_im))


def _s5(xs, prep, d, glu_w, glu_b, batch, seq, steps):
    bre, bim, are, aim, cre, cim = prep
    x_tm = jnp.transpose(xs.reshape(batch, seq, S5_WIDTH), (1, 0, 2)).reshape(seq * batch, S5_WIDTH)
    rows = steps * batch
    d2, gb2, gwb = d.reshape(1, S5_WIDTH), glu_b.reshape(1, S5_WIDTH), glu_w.astype(BF16)
    full = lambda a: pl.BlockSpec(a.shape, lambda i: (0,) * a.ndim)
    y = pl.pallas_call(
        functools.partial(_s5_kernel, steps=steps, batch=batch), grid=(seq // steps,),
        in_specs=[pl.BlockSpec((rows, S5_WIDTH), lambda i: (i, 0)), full(bre), full(bim), full(are), full(aim),
                  full(cre), full(cim), full(d2), full(gwb), full(gb2)],
        out_specs=pl.BlockSpec((rows, S5_WIDTH), lambda i: (i, 0)),
        out_shape=jax.ShapeDtypeStruct((seq * batch, S5_WIDTH), BF16),
        scratch_shapes=[pltpu.VMEM((batch, S5_LANES), F32), pltpu.VMEM((batch, S5_LANES), F32),
                        pltpu.VMEM((rows, S5_LANES), F32), pltpu.VMEM((rows, S5_LANES), F32)],
        compiler_params=_cparams("arbitrary"), name="s5_scan",
    )(x_tm, bre, bim, are, aim, cre, cim, d2, gwb, gb2)
    return jnp.transpose(y.reshape(seq, batch, S5_WIDTH), (1, 0, 2)).reshape(batch * seq, S5_WIDTH)


def _merge_kernel(x_ref, a_ref, b_ref, c_ref, bg_ref, wa_ref, wb_ref, wc_ref, wm_ref, g_ref,
                  wrh_ref, wrl_ref, br_ref, xo_ref, h_ref, route_ref):
    y_a = _dot(a_ref[...], wa_ref[...])
    y_b = _dot(b_ref[...], wb_ref[...])
    y_c = _dot(c_ref[...], wc_ref[...])
    merged = (bg_ref[:, :D_MODEL].astype(F32) * y_a + bg_ref[:, D_MODEL:2 * D_MODEL].astype(F32) * y_b
              + bg_ref[:, 2 * D_MODEL:].astype(F32) * y_c)
    x = x_ref[...] + _dot(merged.astype(BF16), wm_ref[...])
    xo_ref[...] = x
    h = _rms(x, g_ref[...])
    h_ref[...] = h

    h_hi, h_lo = _split_bf16(h)
    logits = _dot(h_hi, wrh_ref[...]) + _dot(h_hi, wrl_ref[...]) + _dot(h_lo, wrh_ref[...]) + br_ref[...]
    lane = lax.broadcasted_iota(jnp.int32, (1, LANES), 1)
    big = jnp.int32(LANES)
    is_grp = lane < MOE_GROUPS
    gl = jnp.where(is_grp, logits, -jnp.inf)
    gmax = jnp.max(gl, axis=-1, keepdims=True)
    grp = jnp.min(jnp.where(gl == gmax, lane, big), axis=-1, keepdims=True)
    grp_w = 1.0 / jnp.sum(jnp.where(is_grp, jnp.exp(logits - gmax), 0.0), axis=-1, keepdims=True)
    e_lane = lane - MOE_GROUPS
    in_grp = (e_lane >= 0) & (lax.shift_right_logical(jnp.maximum(e_lane, 0), 3) == grp) & (e_lane < MOE_EXPERTS)
    emax = jnp.max(jnp.where(in_grp, logits, -jnp.inf), axis=-1, keepdims=True)
    ee = jnp.where(in_grp, jnp.exp(logits - emax), 0.0)
    prob = jnp.where(in_grp, ee / jnp.sum(ee, axis=-1, keepdims=True), -1.0)
    p1 = jnp.max(prob, axis=-1, keepdims=True)
    j1 = jnp.min(jnp.where(prob == p1, lane, big), axis=-1, keepdims=True)
    prob2 = jnp.where(lane == j1, -1.0, prob)
    p2 = jnp.max(prob2, axis=-1, keepdims=True)
    j2 = jnp.min(jnp.where(prob2 == p2, lane, big), axis=-1, keepdims=True)
    psum = p1 + p2
    route = jnp.where(lane == 0, grp_w * p1 / psum, 0.0)
    route = jnp.where(lane == 1, grp_w * p2 / psum, route)
    route = jnp.where(lane == 2, (j1 - MOE_GROUPS).astype(F32), route)
    route = jnp.where(lane == 3, (j2 - MOE_GROUPS).astype(F32), route)
    route_ref[...] = jnp.broadcast_to(route, route_ref.shape)


def _merge(x2, attn, sgu, s5, bg, w_attn_o, w_sgu_o, w_s5_o, w_mix_o, norm_g, rgw, rgb, rew, reb, tm):
    t = x2.shape[0]
    wr = jnp.concatenate([rgw, rew, jnp.zeros((D_MODEL, LANES - MOE_GROUPS - MOE_EXPERTS), F32)], axis=1)
    wr_hi = wr.astype(BF16)
    wr_lo = (wr - wr_hi.astype(F32)).astype(BF16)
    br = jnp.concatenate([rgb, reb, jnp.zeros((LANES - MOE_GROUPS - MOE_EXPERTS,), F32)]).reshape(1, LANES)
    ws = [w.astype(BF16) for w in (w_attn_o, w_sgu_o, w_s5_o, w_mix_o)]
    g = norm_g.reshape(1, D_MODEL)
    row = lambda width: pl.BlockSpec((tm, width), lambda i: (i, 0))
    full = lambda a: pl.BlockSpec(a.shape, lambda i: (0,) * a.ndim)
    return pl.pallas_call(
        _merge_kernel, grid=(t // tm,),
        in_specs=[row(D_MODEL), row(NSA_WIDTH), row(SGU_WIDTH), row(S5_WIDTH), row(3 * D_MODEL)]
                 + [full(w) for w in ws] + [full(g), full(wr_hi), full(wr_lo), full(br)],
        out_specs=(row(D_MODEL), row(D_MODEL), row(LANES)),
        out_shape=(jax.ShapeDtypeStruct((t, D_MODEL), F32), jax.ShapeDtypeStruct((t, D_MODEL), F32),
                   jax.ShapeDtypeStruct((t, LANES), F32)),
        compiler_params=_cparams("parallel"), name="merge_route",
    )(x2, attn, sgu, s5, bg, *ws, g, wr_hi, wr_lo, br)


def _start_row_gather(src_hbm, rows_of, dst_buf, sem, n):
    for r in range(n):
        pltpu.make_async_copy(src_hbm.at[pl.ds(rows_of(r), 1)], dst_buf.at[pl.ds(r, 1)], sem).start()


def _wait_row_gather(src_hbm, dst_buf, sem, n):
    pltpu.make_async_copy(src_hbm.at[pl.ds(0, n)], dst_buf, sem).wait()


def _plan_kernel(route_ref, dest_ref, cnt_ref, carry_sc, *, bm):
    phase, i = pl.program_id(0), pl.program_id(1)
    tp = route_ref.shape[0]
    lane = lax.broadcasted_iota(jnp.int32, (1, LANES), 1)
    route = route_ref[...]
    oh0 = jnp.where(lane == route[:, 2:3].astype(jnp.int32), 1.0, 0.0)
    oh1 = jnp.where(lane == route[:, 3:4].astype(jnp.int32), 1.0, 0.0)
    both = oh0 + oh1
    col_sum = jnp.broadcast_to(jnp.sum(both, axis=0, keepdims=True), carry_sc.shape)

    @pl.when((phase == 0) & (i == 0))
    def _():
        carry_sc[...] = jnp.zeros(carry_sc.shape, F32)

    @pl.when(phase == 0)
    def _():
        carry_sc[...] = carry_sc[...] + col_sum

    @pl.when((phase == 1) & (i == 0))
    def _():
        counts = carry_sc[...]
        cnt_ref[...] = counts
        padded = jnp.floor((counts + (bm - 1)) * (1.0 / bm)) * bm
        incl = padded
        shift = 1
        while shift < LANES:
            incl = incl + jnp.where(lane >= shift, pltpu.roll(incl, shift, 1), 0.0)
            shift *= 2
        carry_sc[...] = incl - padded

    @pl.when(phase == 1)
    def _():
        r_idx = lax.broadcasted_iota(jnp.int32, (tp, tp), 0)
        c_idx = lax.broadcasted_iota(jnp.int32, (tp, tp), 1)
        earlier = jnp.where(c_idx < r_idx, 1.0, 0.0).astype(BF16)
        base = _dot(earlier, both.astype(BF16)) + carry_sc[0:1, :]
        d0 = jnp.sum(oh0 * base, axis=1, keepdims=True)
        d1 = jnp.sum(oh1 * base, axis=1, keepdims=True)
        dest_ref[...] = jnp.where(lane == 0, d0, jnp.where(lane == 1, d1, 0.0)).astype(jnp.int32)
        carry_sc[...] = carry_sc[...] + col_sum


def _dispatch_plan(route, bm, tp):
    t = route.shape[0]
    steps = t // tp
    dest, counts = pl.pallas_call(
        functools.partial(_plan_kernel, bm=bm), grid=(2, steps),
        in_specs=[pl.BlockSpec((tp, LANES), lambda p, i: (i, 0))],
        out_specs=(pl.BlockSpec((tp, LANES), lambda p, i: (i * p, 0)),
                   pl.BlockSpec((SUBLANES, LANES), lambda p, i: (0, 0))),
        out_shape=(jax.ShapeDtypeStruct((t, LANES), jnp.int32), jax.ShapeDtypeStruct((SUBLANES, LANES), F32)),
        scratch_shapes=[pltpu.VMEM((SUBLANES, LANES), F32)],
        compiler_params=_cparams("arbitrary", "arbitrary"), name="moe_plan",
    )(route)
    nb = t * MOE_TOPK // bm + MOE_EXPERTS
    nblk = (counts[0, :MOE_EXPERTS].astype(jnp.int32) + (bm - 1)) // bm
    blk_end = jnp.cumsum(nblk)
    blk_exp = jnp.minimum(jnp.searchsorted(blk_end, jnp.arange(nb, dtype=jnp.int32), side="right").astype(jnp.int32),
                          MOE_EXPERTS - 1)
    return dest[:, :MOE_TOPK], blk_exp, nb


def _scatter_kernel(rows_ref, h_ref, zero_hbm, x_hbm, sem, *, tm):
    del zero_hbm
    n = MOE_TOPK * tm
    for r in range(n):
        pltpu.make_async_copy(h_ref.at[pl.ds(r // MOE_TOPK, 1)], x_hbm.at[pl.ds(rows_ref[0, 0, r], 1)], sem).start()
    for _ in range(MOE_TOPK):
        pltpu.make_async_copy(h_ref, x_hbm.at[pl.ds(0, tm)], sem).wait()


def _scatter_rows(h, dest, nb, bm, tm):
    t = h.shape[0]
    steps = t // tm
    rows = dest.reshape(steps, 1, MOE_TOPK * tm)
    x_buf = jnp.zeros((nb * bm, D_MODEL), F32)
    return pl.pallas_call(
        functools.partial(_scatter_kernel, tm=tm), grid=(steps,),
        in_specs=[pl.BlockSpec((1, 1, MOE_TOPK * tm), lambda i: (i, 0, 0), memory_space=pltpu.SMEM),
                  pl.BlockSpec((tm, D_MODEL), lambda i: (i, 0)),
                  pl.BlockSpec(memory_space=pl.ANY)],
        out_specs=pl.BlockSpec(memory_space=pl.ANY),
        out_shape=jax.ShapeDtypeStruct((nb * bm, D_MODEL), F32),
        scratch_shapes=[pltpu.SemaphoreType.DMA(())],
        input_output_aliases={2: 0},
        compiler_params=_cparams("arbitrary"), name="moe_dispatch",
    )(rows, h, x_buf)


def _moe_kernel(blk_exp_ref, x_ref, wg_ref, wu_ref, wd_ref, y_ref):
    del blk_exp_ref
    xb = x_ref[...].astype(BF16)
    gate = _dot(xb, wg_ref[0])
    hid = gate * _sigmoid(gate) * _dot(xb, wu_ref[0])
    y_ref[...] = _dot(hid.astype(BF16), wd_ref[0])


def _experts(h, route, w_gate, w_up, w_down, bm, tp, tm):
    dest, blk_exp, nb = _dispatch_plan(route, bm, tp)
    x_buf = _scatter_rows(h, dest, nb, bm, tm)
    wspec = lambda shape: pl.BlockSpec((1,) + shape, lambda i, be: (be[i], 0, 0))
    grid_spec = pltpu.PrefetchScalarGridSpec(
        num_scalar_prefetch=1, grid=(nb,),
        in_specs=[pl.BlockSpec((bm, D_MODEL), lambda i, be: (i, 0)),
                  wspec((D_MODEL, EXPERT_HIDDEN)), wspec((D_MODEL, EXPERT_HIDDEN)),
                  wspec((EXPERT_HIDDEN, D_MODEL))],
        out_specs=pl.BlockSpec((bm, D_MODEL), lambda i, be: (i, 0)))
    y_blocks = pl.pallas_call(
        _moe_kernel, grid_spec=grid_spec,
        out_shape=jax.ShapeDtypeStruct((nb * bm, D_MODEL), F32),
        compiler_params=_cparams("parallel"), name="moe_experts",
    )(blk_exp, x_buf, w_gate.astype(BF16), w_up.astype(BF16), w_down.astype(BF16))
    return y_blocks, dest.reshape(-1)


def _combine_kernel(first_ref, next_ref, x_ref, route_ref, g_ref, y_hbm, o_ref, ybuf, sem, *, tm, final_norm):
    i = pl.program_id(0)
    cur = lax.rem(i, 2)
    nxt = 1 - cur
    n = MOE_TOPK * tm

    def row_of(ref):
        return lambda r: ref[0, 0, r]

    @pl.when(i == 0)
    def _():
        _start_row_gather(y_hbm, row_of(first_ref), ybuf.at[0], sem.at[0], n)

    _start_row_gather(y_hbm, row_of(next_ref), ybuf.at[nxt], sem.at[nxt], n)
    _wait_row_gather(y_hbm, ybuf.at[cur], sem.at[cur], n)
    route = route_ref[...]
    x = x_ref[...] + route[:, 0:1] * ybuf[cur, 0:tm, :] + route[:, 1:2] * ybuf[cur, tm:n, :]
    o_ref[...] = _rms(x, g_ref[...]) if final_norm else x

    @pl.when(i == pl.num_programs(0) - 1)
    def _():
        _wait_row_gather(y_hbm, ybuf.at[nxt], sem.at[nxt], n)


def _combine(x2, y_blocks, dest, route, g, final_norm, tm):
    t = x2.shape[0]
    steps = t // tm
    rows = jnp.transpose(dest.reshape(steps, tm, MOE_TOPK), (0, 2, 1)).reshape(steps, 1, MOE_TOPK * tm)
    g2 = g.reshape(1, D_MODEL)
    row = lambda width: pl.BlockSpec((tm, width), lambda i: (i, 0))
    smem_rows = lambda index_map: pl.BlockSpec((1, 1, MOE_TOPK * tm), index_map, memory_space=pltpu.SMEM)
    return pl.pallas_call(
        functools.partial(_combine_kernel, tm=tm, final_norm=final_norm), grid=(steps,),
        in_specs=[smem_rows(lambda i: (0, 0, 0)), smem_rows(lambda i: (jnp.minimum(i + 1, steps - 1), 0, 0)),
                  row(D_MODEL), row(LANES), pl.BlockSpec((1, D_MODEL), lambda i: (0, 0)),
                  pl.BlockSpec(memory_space=pl.ANY)],
        out_specs=row(D_MODEL), out_shape=jax.ShapeDtypeStruct((t, D_MODEL), F32),
        scratch_shapes=[pltpu.VMEM((2, MOE_TOPK * tm, D_MODEL), F32), pltpu.SemaphoreType.DMA((2,))],
        compiler_params=_cparams("arbitrary"), name="moe_combine",
    )(rows, rows, x2, route, g2, y_blocks)


def _tiles(batch, seq):
    t = batch * seq
    return dict(
        tm_proj=min(256, t), tq=min(256, seq), tk=min(256, seq), tm_sgu=min(512, t),
        s5_steps=min(128, seq), tm_merge=min(256, t), bm=256, tp_plan=min(512, t), tm_scat=min(256, t),
        tm_comb=min(256, t))


def kernel(x, positions, norm_mix_g, w_in, cmp_pos_k, cmp_pos_v, cmp_k_w1, cmp_k_w2, cmp_v_w1, cmp_v_w2, w_attn_o, sgu_norm_g, sgu_w, sgu_b, w_sgu_o, s5_a_re, s5_a_im, s5_log_step, s5_b_re, s5_b_im, s5_c_re, s5_c_im, s5_d, s5_glu_w, s5_glu_b, w_s5_o, w_mix_o, norm_ffn_g, router_group_w, router_group_b, router_expert_w, router_expert_b, expert_w_gate, expert_w_up, expert_w_down, norm_final_g):
    batch, seq, _ = x.shape
    depth = w_in.shape[0]
    cfg = _tiles(batch, seq)
    tables = _rope_tables(positions)
    x2 = x.reshape(batch * seq, D_MODEL)
    for l in range(depth):
        qt, qrt, kvc, krow, vt, gt, sgu_in, s5_in, bg = _in_proj(
            x2, norm_mix_g[l].reshape(1, D_MODEL), _prep_in_proj(w_in[l]), tables, cfg["tm_proj"])
        kc, vct = _compress(kvc, _prep_compress(cmp_pos_k[l], cmp_pos_v[l], cmp_k_w1[l], cmp_k_w2[l],
                                                cmp_v_w1[l], cmp_v_w2[l]), batch, seq)
        attn = _attention(qt, qrt, gt, kc, vct, krow, vt, batch, seq, cfg["tq"], cfg["tk"])
        sgu = _sgu(sgu_in, sgu_norm_g[l], sgu_w[l], sgu_b[l], cfg["tm_sgu"])
        s5 = _s5(s5_in, _prep_s5(s5_a_re[l], s5_a_im[l], s5_log_step[l], s5_b_re[l], s5_b_im[l],
                                 s5_c_re[l], s5_c_im[l]),
                 s5_d[l], s5_glu_w[l], s5_glu_b[l], batch, seq, cfg["s5_steps"])
        x_mid, h, route = _merge(x2, attn, sgu, s5, bg, w_attn_o[l], w_sgu_o[l], w_s5_o[l], w_mix_o[l],
                                 norm_ffn_g[l], router_group_w[l], router_group_b[l],
                                 router_expert_w[l], router_expert_b[l], cfg["tm_merge"])
        y_blocks, dest = _experts(h, route, expert_w_gate[l], expert_w_up[l], expert_w_down[l],
                                  cfg["bm"], cfg["tp_plan"], cfg["tm_scat"])
        x2 = _combine(x_mid, y_blocks, dest, route, norm_final_g, l == depth - 1, cfg["tm_comb"])
    return x2.reshape(batch, seq, D_MODEL)
```

```python
import functools
import math

import numpy as np
import jax
import jax.numpy as jnp
from jax import lax
from jax.experimental import pallas as pl
from jax.experimental.pallas import tpu as pltpu

F32 = jnp.float32
BF16 = jnp.bfloat16

D_MODEL = 1024
HEAD_DIM = 64
NSA_HEADS = 8
NSA_KV_GROUPS = 2
HEADS_PER_GROUP = NSA_HEADS // NSA_KV_GROUPS
GROUP_WIDTH = HEADS_PER_GROUP * HEAD_DIM
NSA_WIDTH = NSA_HEADS * HEAD_DIM
KV_WIDTH = NSA_KV_GROUPS * HEAD_DIM
ROPE_DIM = HEAD_DIM // 4
ROPE_HALF = ROPE_DIM // 2
ROPE_THETA = 500000.0
CMP_LEN = 32
CMP_STRIDE = 16
CMP_HIDDEN = 128
SLC_LEN = 64
SLC_SHIFT = int(math.log2(SLC_LEN))
SLC_TOP = 16
WINDOW = 512
FORCE_SCORE = 1.0e4
NEG_INF = -1.0e30
SGU_WIDTH = 256
SGU_GROUPS = 4
SGU_CHUNK = 128
S5_WIDTH = 256
S5_GROUP_CH = 16
S5_GROUPS = S5_WIDTH // S5_GROUP_CH
S5_STATE = 64
S5_LANES = S5_GROUPS * S5_STATE
MOE_GROUPS = 4
MOE_EXPERTS_PER_GROUP = 8
MOE_EXPERTS = MOE_GROUPS * MOE_EXPERTS_PER_GROUP
MOE_TOPK = 2
EXPERT_HIDDEN = 512
RMS_EPS = 1e-6
ATTN_SCALE = HEAD_DIM ** -0.5
LOG2_E = math.log2(math.e)

LANES = 128
SUBLANES = 8
VMEM_LIMIT_BYTES = 56 * 1024 * 1024

BF16_SUBLANES = 16
V_ROWS = HEAD_DIM + BF16_SUBLANES
GATE_COLS = HEADS_PER_GROUP * 3
GATE_ROWS = 16

_OFF_Q = 0
_OFF_KVC = _OFF_Q + NSA_WIDTH
_OFF_KS = _OFF_KVC + 2 * KV_WIDTH
_OFF_VS = _OFF_KS + KV_WIDTH
_OFF_KW = _OFF_VS + KV_WIDTH
_OFF_VW = _OFF_KW + KV_WIDTH
_OFF_GATE = _OFF_VW + KV_WIDTH
_OFF_SGU = _OFF_GATE + LANES
_OFF_S5 = _OFF_SGU + 2 * SGU_WIDTH
_OFF_BG = _OFF_S5 + S5_WIDTH
_IN_W = _OFF_BG + 3 * D_MODEL


def _cparams(*sem):
    return pltpu.CompilerParams(dimension_semantics=sem, vmem_limit_bytes=VMEM_LIMIT_BYTES)


def _gelu(x):
    return 0.5 * x * (1.0 + jnp.tanh(math.sqrt(2.0 / math.pi) * (x + 0.044715 * (x * x * x))))


def _sigmoid(x):
    return 1.0 / (1.0 + jnp.exp(-x))


def _dot(a, b):
    return jnp.dot(a, b, preferred_element_type=F32)


def _dot_nt(a, b):
    return lax.dot_general(a, b, (((1,), (1,)), ((), ())), preferred_element_type=F32)


def _split_bf16(x):
    hi = x.astype(BF16)
    lo = (x - hi.astype(F32)).astype(BF16)
    return hi, lo


def _rms(x, g):
    return x * lax.rsqrt(jnp.mean(x * x, axis=-1, keepdims=True) + RMS_EPS) * g


def _rope(x, c, s_lo, s_hi):
    n = x.shape[-1]
    return x * c + pltpu.roll(x, n - ROPE_HALF, 1) * s_lo + pltpu.roll(x, ROPE_HALF, 1) * s_hi


def _in_proj_kernel(x_ref, g_ref, w_ref, c_ref, slo_ref, shi_ref,
                    qt_ref, qrt_ref, kvc_ref, ksel_ref, kwin_ref, vt_ref, gt_ref, sgu_ref, s5_ref, bg_ref, *, seq):
    tm = x_ref.shape[0]
    x = x_ref[...]
    hb = _rms(x, g_ref[...]).astype(BF16)

    def proj(off, width):
        return _dot(hb, w_ref[:, off:off + width])

    c, s_lo, s_hi = c_ref[...], slo_ref[...], shi_ref[...]
    rep = NSA_WIDTH // LANES
    q = proj(_OFF_Q, NSA_WIDTH)
    qt_ref[...] = (q * ATTN_SCALE).T
    qrot = _rope(q, jnp.concatenate([c] * rep, axis=1), jnp.concatenate([s_lo] * rep, axis=1),
                 jnp.concatenate([s_hi] * rep, axis=1))
    qrt_ref[...] = (qrot * (ATTN_SCALE * LOG2_E)).T.astype(BF16)
    kvc_ref[...] = proj(_OFF_KVC, 2 * KV_WIDTH)
    ks = _rope(proj(_OFF_KS, KV_WIDTH), c, s_lo, s_hi)
    kw = _rope(proj(_OFF_KW, KV_WIDTH), c, s_lo, s_hi)
    pos = lax.rem(pl.program_id(0) * tm, seq) + lax.broadcasted_iota(jnp.int32, (tm, 1), 0)
    blk = lax.shift_right_logical(pos, SLC_SHIFT)
    blk_onehot = jnp.where(lax.broadcasted_iota(jnp.int32, (1, HEAD_DIM), 1) == blk, 1.0, 0.0)
    for g in range(NSA_KV_GROUPS):
        cols = slice(g * HEAD_DIM, (g + 1) * HEAD_DIM)
        ksel_ref[g] = jnp.concatenate([ks[:, cols], blk_onehot], axis=1).astype(BF16)
        kwin_ref[g] = kw[:, cols].astype(BF16)
    aug_rows = lax.broadcasted_iota(jnp.int32, (V_ROWS - HEAD_DIM, LANES), 0)
    ones_row = jnp.where(aug_rows == 0, 1.0, 0.0).astype(BF16)
    for p, off in enumerate((_OFF_VS, _OFF_VW)):
        vt = proj(off, KV_WIDTH).T.astype(BF16)
        for g in range(NSA_KV_GROUPS):
            for ch in range(tm // LANES):
                idx = p * NSA_KV_GROUPS + g
                vt_ref[idx, ch, 0:HEAD_DIM, :] = vt[g * HEAD_DIM:(g + 1) * HEAD_DIM, ch * LANES:(ch + 1) * LANES]
                vt_ref[idx, ch, HEAD_DIM:V_ROWS, :] = ones_row
    gt_ref[...] = _sigmoid(proj(_OFF_GATE, LANES)).T[:NSA_KV_GROUPS * GATE_ROWS]
    sgu_ref[...] = proj(_OFF_SGU, 2 * SGU_WIDTH)
    s5_ref[...] = proj(_OFF_S5, S5_WIDTH)
    for k in range(3):
        bg_ref[:, k * D_MODEL:(k + 1) * D_MODEL] = _sigmoid(
            proj(_OFF_BG + k * D_MODEL, D_MODEL)).astype(BF16)


def _prep_in_proj(w_in):
    sizes = (NSA_WIDTH,) + (KV_WIDTH,) * 6 + (3 * NSA_HEADS, 2 * SGU_WIDTH, S5_WIDTH, 3 * D_MODEL)
    offs = np.concatenate([[0], np.cumsum(sizes)])
    q, kc, vc, ks, vs, kw, vw, ng, sgu, s5, bg = [w_in[:, offs[i]:offs[i + 1]] for i in range(11)]
    pad = lambda n: jnp.zeros((D_MODEL, n), w_in.dtype)
    gates = []
    for g in range(NSA_KV_GROUPS):
        gates += [ng[:, g * GATE_COLS:(g + 1) * GATE_COLS], pad(GATE_ROWS - GATE_COLS)]
    gates.append(pad(LANES - NSA_KV_GROUPS * GATE_ROWS))
    return jnp.concatenate([q, kc, vc, ks, vs, kw, vw] + gates + [sgu, s5, bg], axis=1).astype(BF16)


def _rope_tables(positions):
    inv_freq = ROPE_THETA ** (-jnp.arange(ROPE_HALF, dtype=F32) / ROPE_HALF)
    ang = positions.astype(F32).reshape(-1, 1) * inv_freq
    cos, sin = jnp.cos(ang), jnp.sin(ang)
    t = ang.shape[0]
    rest = HEAD_DIM - ROPE_DIM
    c = jnp.concatenate([cos, cos, jnp.ones((t, rest), F32)], axis=1)
    s_lo = jnp.concatenate([-sin, jnp.zeros((t, HEAD_DIM - ROPE_HALF), F32)], axis=1)
    s_hi = jnp.concatenate([jnp.zeros((t, ROPE_HALF), F32), sin, jnp.zeros((t, rest), F32)], axis=1)
    rep = LANES // HEAD_DIM
    return tuple(jnp.concatenate([a] * rep, axis=1) for a in (c, s_lo, s_hi))


def _in_proj(x2, g, w, tables, seq, tm):
    t = x2.shape[0]
    row = lambda width: pl.BlockSpec((tm, width), lambda i: (i, 0))
    col = lambda height: pl.BlockSpec((height, tm), lambda i: (0, i))
    full = lambda a: pl.BlockSpec(a.shape, lambda i: (0,) * a.ndim)
    n_kv = 2 * NSA_KV_GROUPS
    out_shape = (
        jax.ShapeDtypeStruct((NSA_WIDTH, t), F32),
        jax.ShapeDtypeStruct((NSA_WIDTH, t), BF16),
        jax.ShapeDtypeStruct((t, 2 * KV_WIDTH), F32),
        jax.ShapeDtypeStruct((NSA_KV_GROUPS, t, 2 * HEAD_DIM), BF16),
        jax.ShapeDtypeStruct((NSA_KV_GROUPS, t, HEAD_DIM), BF16),
        jax.ShapeDtypeStruct((n_kv, t // LANES, V_ROWS, LANES), BF16),
        jax.ShapeDtypeStruct((NSA_KV_GROUPS * GATE_ROWS, t), F32),
        jax.ShapeDtypeStruct((t, 2 * SGU_WIDTH), F32),
        jax.ShapeDtypeStruct((t, S5_WIDTH), F32),
        jax.ShapeDtypeStruct((t, 3 * D_MODEL), BF16),
    )
    out_specs = (col(NSA_WIDTH), col(NSA_WIDTH), row(2 * KV_WIDTH),
                 pl.BlockSpec((NSA_KV_GROUPS, tm, 2 * HEAD_DIM), lambda i: (0, i, 0)),
                 pl.BlockSpec((NSA_KV_GROUPS, tm, HEAD_DIM), lambda i: (0, i, 0)),
                 pl.BlockSpec((n_kv, tm // LANES, V_ROWS, LANES), lambda i: (0, i, 0, 0)),
                 col(NSA_KV_GROUPS * GATE_ROWS), row(2 * SGU_WIDTH), row(S5_WIDTH), row(3 * D_MODEL))
    return pl.pallas_call(
        functools.partial(_in_proj_kernel, seq=seq), grid=(t // tm,),
        in_specs=[row(D_MODEL), full(g), full(w), row(LANES), row(LANES), row(LANES)],
        out_specs=out_specs, out_shape=out_shape,
        compiler_params=_cparams("parallel"), name="in_proj",
    )(x2, g, w, *tables)


def _compress_kernel(r_ref, wa_ref, wb_ref, pos_ref, w1_ref, w2_ref, w2t_ref, kc_ref, vct_ref):
    rb = r_ref[0].astype(BF16)
    n = rb.shape[0]
    a = _dot(rb, wa_ref[...])
    b = _dot(rb, wb_ref[...])
    b = pltpu.roll(b, n - 1, 0)
    rows = lax.broadcasted_iota(jnp.int32, b.shape, 0)
    b = jnp.where(rows == n - 1, 0.0, b)
    for m in range(2 * NSA_KV_GROUPS):
        typ, g = divmod(m, NSA_KV_GROUPS)
        bias = _dot(pos_ref[typ].astype(BF16), w1_ref[typ])
        hid = a[:, m * CMP_HIDDEN:(m + 1) * CMP_HIDDEN] + b[:, m * CMP_HIDDEN:(m + 1) * CMP_HIDDEN] + bias
        act = _gelu(hid).astype(BF16)
        if typ == 0:
            kc_ref[0, g] = _dot(act, w2_ref[...])
        else:
            vct_ref[0, g] = _dot_nt(w2t_ref[...], act)


def _prep_compress(cmp_pos_k, cmp_pos_v, k_w1, k_w2, v_w1, v_w2):
    streams = 2 * NSA_KV_GROUPS
    w1 = jnp.stack([k_w1, v_w1]).reshape(2, CMP_LEN, HEAD_DIM, CMP_HIDDEN)

    def expand(half):
        out = jnp.zeros((CMP_STRIDE, streams, HEAD_DIM, streams, CMP_HIDDEN), F32)
        for m in range(streams):
            out = out.at[:, m, :, m, :].set(w1[m // NSA_KV_GROUPS, half * CMP_STRIDE:(half + 1) * CMP_STRIDE])
        return out.reshape(CMP_STRIDE * streams * HEAD_DIM, streams * CMP_HIDDEN).astype(BF16)

    pos = jnp.stack([cmp_pos_k, cmp_pos_v]).reshape(2, 1, CMP_LEN * HEAD_DIM)
    return (expand(0), expand(1), pos, jnp.stack([k_w1, v_w1]).astype(BF16),
            k_w2.astype(BF16), v_w2.T.astype(BF16))


def _compress(kvc, prep, batch, seq):
    n = seq // CMP_STRIDE
    r = kvc.reshape(batch, n, CMP_STRIDE * 2 * KV_WIDTH)
    full = lambda a: pl.BlockSpec(a.shape, lambda b: (0,) * a.ndim)
    return pl.pallas_call(
        _compress_kernel, grid=(batch,),
        in_specs=[pl.BlockSpec((1, n, r.shape[-1]), lambda b: (b, 0, 0))] + [full(a) for a in prep],
        out_specs=(pl.BlockSpec((1, NSA_KV_GROUPS, n, HEAD_DIM), lambda b: (b, 0, 0, 0)),
                   pl.BlockSpec((1, NSA_KV_GROUPS, HEAD_DIM, n), lambda b: (b, 0, 0, 0))),
        out_shape=(jax.ShapeDtypeStruct((batch, NSA_KV_GROUPS, n, HEAD_DIM), F32),
                   jax.ShapeDtypeStruct((batch, NSA_KV_GROUPS, HEAD_DIM, n), F32)),
        compiler_params=_cparams("parallel"), name="compress_kv",
    )(r, *prep)


def _attn_kernel(qt_ref, qrt_ref, gt_ref, kc_ref, vct_ref, ks_ref, vst_ref, kw_ref, vwt_ref, ovt_ref,
                 o_ref, qa_sc, m_sc, acc_sc, ow_sc, s_sc, p_sc, a_sc, *, tq, tk, seq):
    n_cmp = kc_ref.shape[2]
    n_slc = seq // SLC_LEN
    heads = HEADS_PER_GROUP
    t0 = pl.program_id(2) * tq
    t_lane = t0 + lax.broadcasted_iota(jnp.int32, (1, tq), 1)

    def lane_tile(a):
        return jnp.concatenate([a] * heads, axis=1)

    def head_cols(ref):
        return jnp.concatenate([ref[r * HEAD_DIM:(r + 1) * HEAD_DIM, :] for r in range(heads)], axis=1)

    q_hi, q_lo = _split_bf16(head_cols(qt_ref))
    qr = head_cols(qrt_ref)

    slab = min(WINDOW + tq, seq)
    st = pl.multiple_of(jnp.clip(t0 - WINDOW, 0, seq - slab), LANES)
    s = _dot(kw_ref[0, pl.ds(st, slab), :], qr)
    diff = t_lane - (st + lax.broadcasted_iota(jnp.int32, (slab, 1), 0))
    s = s + lane_tile(jnp.where((diff >= 0) & (diff < WINDOW), 0.0, NEG_INF))
    p = jnp.exp2(s - jnp.max(s, axis=0, keepdims=True))
    st_tile = st // LANES
    vw_t = jnp.concatenate([vwt_ref[0, st_tile + c] for c in range(slab // LANES)], axis=1)
    pv = _dot(vw_t, p.astype(BF16))
    ow_sc[...] = pv[0:HEAD_DIM] / pv[HEAD_DIM:HEAD_DIM + 1]

    kc_hi, kc_lo = _split_bf16(kc_ref[0, 0])
    s = _dot(kc_hi, q_hi) + _dot(kc_lo, q_hi) + _dot(kc_hi, q_lo)
    cmp_end = lax.broadcasted_iota(jnp.int32, (n_cmp, 1), 0) * CMP_STRIDE + (CMP_LEN - 1)
    m_c = cmp_end <= lane_tile(t_lane)
    s = jnp.where(m_c, s, NEG_INF)
    e = jnp.where(m_c, jnp.exp(s - jnp.max(s, axis=0, keepdims=True)), 0.0)
    p = e / jnp.maximum(jnp.sum(e, axis=0, keepdims=True), 1e-30)
    o_cmp = _dot(vct_ref[0, 0].astype(BF16), p.astype(BF16))
    p_sum = p[:, 0:tq]
    for r in range(1, heads):
        p_sum = p_sum + p[:, r * tq:(r + 1) * tq]
    ps_hi, ps_lo = _split_bf16(p_sum)
    imp = _dot(ovt_ref[...], ps_hi) + _dot(ovt_ref[...], ps_lo)

    j_idx = lax.broadcasted_iota(jnp.int32, (n_slc, 1), 0)
    blk_t = lax.shift_right_logical(t_lane, SLC_SHIFT)
    causal = j_idx <= blk_t
    forced = (j_idx == 0) | (j_idx == blk_t) | (j_idx == blk_t - 1)
    imp = jnp.where(forced, FORCE_SCORE, imp)
    imp = jnp.where(causal, imp, NEG_INF)
    picked = jnp.zeros((n_slc, tq), F32)
    j_f32 = j_idx.astype(F32)
    for _ in range(min(SLC_TOP, n_slc)):
        best = jnp.max(imp, axis=0, keepdims=True)
        first = jnp.min(jnp.where(imp == best, j_f32, float(n_slc)), axis=0, keepdims=True)
        hit = j_f32 == first
        picked = jnp.where(hit, 1.0, picked)
        imp = jnp.where(hit, -jnp.inf, imp)
    bias = jnp.where((picked > 0.5) & causal, 0.0, NEG_INF)
    if n_slc < HEAD_DIM:
        bias = jnp.concatenate([bias, jnp.zeros((HEAD_DIM - n_slc, tq), F32)], axis=0)
    qa_sc[...] = jnp.concatenate([qr, lane_tile(bias.astype(BF16))], axis=0)

    m_sc[...] = jnp.full(m_sc.shape, NEG_INF, F32)
    acc_sc[...] = jnp.zeros(acc_sc.shape, F32)
    p_sc[1] = jnp.zeros(p_sc.shape[1:], BF16)
    a_sc[1] = jnp.ones(a_sc.shape[1:], F32)
    lanes_per_tile = tk // LANES

    def scores(j, slot):
        k0 = j * tk if isinstance(j, int) else pl.multiple_of(j * tk, tk)
        s_sc[slot] = _dot(ks_ref[0, pl.ds(k0, tk), :], qa_sc[...])

    def softmax(j, slot, diagonal):
        s = s_sc[slot]
        if diagonal:
            key_pos = j * tk + lax.broadcasted_iota(jnp.int32, (tk, 1), 0)
            s = s + lane_tile(jnp.where(key_pos <= t_lane, 0.0, NEG_INF))
        m_old = m_sc[...]
        m_new = jnp.maximum(m_old, jnp.max(s, axis=0, keepdims=True))
        m_sc[...] = m_new
        a_sc[slot] = jnp.exp2(m_old - m_new)
        p_sc[slot] = jnp.exp2(s - m_new).astype(BF16)

    def accumulate(j, slot):
        tile = jnp.maximum(j, 0) * lanes_per_tile
        v_t = jnp.concatenate([vst_ref[0, tile + c] for c in range(lanes_per_tile)], axis=1)
        acc_sc[...] = a_sc[slot] * acc_sc[...] + _dot(v_t, p_sc[slot])

    def pipe_step(i, i_is_odd):
        a, b = (1, 0) if i_is_odd else (0, 1)
        accumulate(i - 2, a)
        softmax(i - 1, b, False)
        scores(i, a)

    n_kv = (t0 + tq + tk - 1) // tk
    scores(0, 0)

    def two_steps(u, carry):
        pipe_step(2 * u + 1, True)
        pipe_step(2 * u + 2, False)
        return carry

    lax.fori_loop(0, (n_kv - 1) // 2, two_steps, 0)

    @pl.when(((n_kv - 1) & 1) == 1)
    def _():
        pipe_step(n_kv - 1, True)

    last = (n_kv - 1) & 1
    accumulate(n_kv - 2, 1 - last)
    softmax(n_kv - 1, last, True)
    accumulate(n_kv - 1, last)
    o_slc = acc_sc[0:HEAD_DIM, :] / acc_sc[HEAD_DIM:HEAD_DIM + 1, :]
    o_win = ow_sc[...]

    gates = gt_ref[...]
    outs = []
    for r in range(heads):
        cols = slice(r * tq, (r + 1) * tq)
        outs.append(gates[3 * r:3 * r + 1, :] * o_cmp[:, cols] + gates[3 * r + 1:3 * r + 2, :] * o_slc[:, cols]
                    + gates[3 * r + 2:3 * r + 3, :] * o_win[:, cols])
    o_ref[...] = jnp.concatenate(outs, axis=0).T.astype(o_ref.dtype)


def _overlap_matrix_t(n_cmp, n_slc):
    c0 = np.arange(n_cmp)[None, :] * CMP_STRIDE
    s0 = np.arange(n_slc)[:, None] * SLC_LEN
    ov = np.clip(np.minimum(c0 + CMP_LEN, s0 + SLC_LEN) - np.maximum(c0, s0), 0, None) / CMP_LEN
    ov[:, n_cmp - 1] = 0.0
    return jnp.asarray(ov, BF16)


def _attention(qt, qrt, gt, kc, vct, ksel, kwin, vt, batch, seq, tq, tk):
    assert tk % tq == 0 and seq // SLC_LEN <= HEAD_DIM
    n_cmp = seq // CMP_STRIDE
    n_slc = seq // SLC_LEN
    nq = seq // tq
    ksel = ksel.reshape(NSA_KV_GROUPS, batch, seq, 2 * HEAD_DIM)
    kwin = kwin.reshape(NSA_KV_GROUPS, batch, seq, HEAD_DIM)
    vt = vt.reshape(2 * NSA_KV_GROUPS, batch, seq // LANES, V_ROWS, LANES)
    ovt = _overlap_matrix_t(n_cmp, n_slc)
    qspec = pl.BlockSpec((GROUP_WIDTH, tq), lambda b, g, i: (g, b * nq + i))
    k_spec = lambda width: pl.BlockSpec((None, 1, seq, width), lambda b, g, i: (g, b, 0, 0))
    v_spec = lambda p: pl.BlockSpec((None, 1, seq // LANES, V_ROWS, LANES),
                                    lambda b, g, i: (p * NSA_KV_GROUPS + g, b, 0, 0, 0))
    kernel = functools.partial(_attn_kernel, tq=tq, tk=tk, seq=seq)
    return pl.pallas_call(
        kernel, grid=(batch, NSA_KV_GROUPS, nq),
        in_specs=[qspec, qspec, pl.BlockSpec((GATE_ROWS, tq), lambda b, g, i: (g, b * nq + i)),
                  pl.BlockSpec((1, 1, n_cmp, HEAD_DIM), lambda b, g, i: (b, g, 0, 0)),
                  pl.BlockSpec((1, 1, HEAD_DIM, n_cmp), lambda b, g, i: (b, g, 0, 0)),
                  k_spec(2 * HEAD_DIM), v_spec(0), k_spec(HEAD_DIM), v_spec(1),
                  pl.BlockSpec(ovt.shape, lambda b, g, i: (0, 0))],
        out_specs=pl.BlockSpec((tq, GROUP_WIDTH), lambda b, g, i: (b * nq + i, g)),
        out_shape=jax.ShapeDtypeStruct((batch * seq, NSA_WIDTH), BF16),
        scratch_shapes=[pltpu.VMEM((2 * HEAD_DIM, HEADS_PER_GROUP * tq), BF16),
                        pltpu.VMEM((1, HEADS_PER_GROUP * tq), F32),
                        pltpu.VMEM((V_ROWS, HEADS_PER_GROUP * tq), F32),
                        pltpu.VMEM((HEAD_DIM, HEADS_PER_GROUP * tq), F32),
                        pltpu.VMEM((2, tk, HEADS_PER_GROUP * tq), F32),
                        pltpu.VMEM((2, tk, HEADS_PER_GROUP * tq), BF16),
                        pltpu.VMEM((2, 1, HEADS_PER_GROUP * tq), F32)],
        compiler_params=_cparams("parallel", "parallel", "arbitrary"), name="nsa_attention",
    )(qt, qrt, gt, kc, vct, ksel, vt, kwin, vt, ovt)


def _sgu_kernel(uv_ref, g_ref, w_ref, b_ref, o_ref, *, chunks):
    z = _gelu(uv_ref[...])
    u = z[:, :SGU_WIDTH]
    v = _rms(z[:, SGU_WIDTH:], g_ref[...])
    gw = SGU_WIDTH // SGU_GROUPS
    rows = lax.broadcasted_iota(jnp.int32, (SGU_CHUNK, SGU_GROUPS * SGU_CHUNK), 0)
    cols = lax.broadcasted_iota(jnp.int32, (SGU_CHUNK, SGU_GROUPS * SGU_CHUNK), 1)
    w = jnp.where((cols & (SGU_CHUNK - 1)) <= rows, w_ref[...], 0.0).astype(BF16)
    grp_r = lax.broadcasted_iota(jnp.int32, (SGU_GROUPS * SGU_CHUNK, SGU_WIDTH), 0) // SGU_CHUNK
    grp_c = lax.broadcasted_iota(jnp.int32, (SGU_GROUPS * SGU_CHUNK, SGU_WIDTH), 1) // gw
    for c in range(chunks):
        vc = v[c * SGU_CHUNK:(c + 1) * SGU_CHUNK].astype(BF16)
        v_bd = jnp.where(grp_r == grp_c, jnp.concatenate([vc] * SGU_GROUPS, axis=0), jnp.zeros((), BF16))
        mixed = _dot(w, v_bd) + b_ref[...]
        o_ref[c * SGU_CHUNK:(c + 1) * SGU_CHUNK, :] = (u[c * SGU_CHUNK:(c + 1) * SGU_CHUNK] * mixed).astype(o_ref.dtype)


def _sgu(sgu_in, norm_g, w_s, b_s, tm):
    t = sgu_in.shape[0]
    w_cat = jnp.transpose(w_s, (1, 0, 2)).reshape(SGU_CHUNK, SGU_GROUPS * SGU_CHUNK)
    bias = jnp.repeat(b_s.T, SGU_WIDTH // SGU_GROUPS, axis=1)
    g = norm_g.reshape(1, SGU_WIDTH)
    full = lambda a: pl.BlockSpec(a.shape, lambda i: (0,) * a.ndim)
    return pl.pallas_call(
        functools.partial(_sgu_kernel, chunks=tm // SGU_CHUNK), grid=(t // tm,),
        in_specs=[pl.BlockSpec((tm, 2 * SGU_WIDTH), lambda i: (i, 0)), full(g), full(w_cat), full(bias)],
        out_specs=pl.BlockSpec((tm, SGU_WIDTH), lambda i: (i, 0)),
        out_shape=jax.ShapeDtypeStruct((t, SGU_WIDTH), BF16),
        compiler_params=_cparams("parallel"), name="sgu",
    )(sgu_in, g, w_cat, bias)


def _s5_kernel(x_ref, bre_ref, bim_ref, are_ref, aim_ref, cre_ref, cim_ref, d_ref, gw_ref, gb_ref,
               o_ref, hre_sc, him_sc, ure_sc, uim_sc, *, steps, batch):
    @pl.when(pl.program_id(0) == 0)
    def _():
        hre_sc[...] = jnp.zeros(hre_sc.shape, F32)
        him_sc[...] = jnp.zeros(him_sc.shape, F32)

    x = x_ref[...]
    xb = x.astype(BF16)
    ure_sc[...] = _dot(xb, bre_ref[...])
    uim_sc[...] = _dot(xb, bim_ref[...])
    a_re = jnp.broadcast_to(are_ref[...], (batch, S5_LANES))
    a_im = jnp.broadcast_to(aim_ref[...], (batch, S5_LANES))

    def step(t, carry):
        h_re, h_im = carry
        rows = pl.ds(pl.multiple_of(t * batch, batch), batch)
        n_re = a_re * h_re - a_im * h_im + ure_sc[rows, :]
        n_im = a_re * h_im + a_im * h_re + uim_sc[rows, :]
        ure_sc[rows, :] = n_re
        uim_sc[rows, :] = n_im
        return n_re, n_im

    h_re, h_im = lax.fori_loop(0, steps, step, (hre_sc[...], him_sc[...]))
    hre_sc[...] = h_re
    him_sc[...] = h_im
    y = _dot(ure_sc[...].astype(BF16), cre_ref[...]) - _dot(uim_sc[...].astype(BF16), cim_ref[...])
    y = _gelu(y + d_ref[...] * x)
    y = y * _sigmoid(_dot(y.astype(BF16), gw_ref[...]) + gb_ref[...])
    o_ref[...] = y.astype(o_ref.dtype)


def _block_diag(blocks):
    g, r, c = blocks.shape
    eye = jnp.eye(g, dtype=blocks.dtype)
    return (blocks[:, :, None, :] * eye[:, None, :, None]).reshape(g * r, g * c)


def _prep_s5(a_re, a_im, log_step, b_re, b_im, c_re, c_im):
    step = jnp.exp(log_step)[:, None]
    mag = jnp.exp(a_re * step)
    abar_re, abar_im = mag * jnp.cos(a_im * step), mag * jnp.sin(a_im * step)
    den = a_re * a_re + a_im * a_im
    nr, ni = abar_re - 1.0, abar_im
    coef_re = (nr * a_re + ni * a_im) / den
    coef_im = (ni * a_re - nr * a_im) / den
    bbar_re = coef_re[..., None] * b_re - coef_im[..., None] * b_im
    bbar_im = coef_re[..., None] * b_im + coef_im[..., None] * b_re
    to_in = lambda b: _block_diag(jnp.transpose(b, (0, 2, 1))).astype(BF16)
    to_out = lambda c: _block_diag(jnp.transpose(c, (0, 2, 1))).astype(BF16)
    return (to_in(bbar_re), to_in(bbar_im), abar_re.reshape(1, S5_LANES), abar_im.reshape(1, S5_LANES),
            to_out(c_re), to_out(c_im))


def _s5(xs, prep, d, glu_w, glu_b, batch, seq, steps):
    bre, bim, are, aim, cre, cim = prep
    x_tm = jnp.transpose(xs.reshape(batch, seq, S5_WIDTH), (1, 0, 2)).reshape(seq * batch, S5_WIDTH)
    rows = steps * batch
    d2, gb2, gwb = d.reshape(1, S5_WIDTH), glu_b.reshape(1, S5_WIDTH), glu_w.astype(BF16)
    full = lambda a: pl.BlockSpec(a.shape, lambda i: (0,) * a.ndim)
    y = pl.pallas_call(
        functools.partial(_s5_kernel, steps=steps, batch=batch), grid=(seq // steps,),
        in_specs=[pl.BlockSpec((rows, S5_WIDTH), lambda i: (i, 0)), full(bre), full(bim), full(are), full(aim),
                  full(cre), full(cim), full(d2), full(gwb), full(gb2)],
        out_specs=pl.BlockSpec((rows, S5_WIDTH), lambda i: (i, 0)),
        out_shape=jax.ShapeDtypeStruct((seq * batch, S5_WIDTH), BF16),
        scratch_shapes=[pltpu.VMEM((batch, S5_LANES), F32), pltpu.VMEM((batch, S5_LANES), F32),
                        pltpu.VMEM((rows, S5_LANES), F32), pltpu.VMEM((rows, S5_LANES), F32)],
        compiler_params=_cparams("arbitrary"), name="s5_scan",
    )(x_tm, bre, bim, are, aim, cre, cim, d2, gwb, gb2)
    return jnp.transpose(y.reshape(seq, batch, S5_WIDTH), (1, 0, 2)).reshape(batch * seq, S5_WIDTH)


def _merge_kernel(x_ref, a_ref, b_ref, c_ref, bg_ref, wa_ref, wb_ref, wc_ref, wm_ref, g_ref,
                  wrh_ref, wrl_ref, br_ref, xo_ref, h_ref, route_ref):
    y_a = _dot(a_ref[...], wa_ref[...])
    y_b = _dot(b_ref[...], wb_ref[...])
    y_c = _dot(c_ref[...], wc_ref[...])
    merged = (bg_ref[:, :D_MODEL].astype(F32) * y_a + bg_ref[:, D_MODEL:2 * D_MODEL].astype(F32) * y_b
              + bg_ref[:, 2 * D_MODEL:].astype(F32) * y_c)
    x = x_ref[...] + _dot(merged.astype(BF16), wm_ref[...])
    xo_ref[...] = x
    h = _rms(x, g_ref[...])
    h_ref[...] = h

    h_hi, h_lo = _split_bf16(h)
    logits = _dot(h_hi, wrh_ref[...]) + _dot(h_hi, wrl_ref[...]) + _dot(h_lo, wrh_ref[...]) + br_ref[...]
    lane = lax.broadcasted_iota(jnp.int32, (1, LANES), 1)
    big = jnp.int32(LANES)
    is_grp = lane < MOE_GROUPS
    gl = jnp.where(is_grp, logits, -jnp.inf)
    gmax = jnp.max(gl, axis=-1, keepdims=True)
    grp = jnp.min(jnp.where(gl == gmax, lane, big), axis=-1, keepdims=True)
    grp_w = 1.0 / jnp.sum(jnp.where(is_grp, jnp.exp(logits - gmax), 0.0), axis=-1, keepdims=True)
    e_lane = lane - MOE_GROUPS
    in_grp = (e_lane >= 0) & (lax.shift_right_logical(jnp.maximum(e_lane, 0), 3) == grp) & (e_lane < MOE_EXPERTS)
    emax = jnp.max(jnp.where(in_grp, logits, -jnp.inf), axis=-1, keepdims=True)
    ee = jnp.where(in_grp, jnp.exp(logits - emax), 0.0)
    prob = jnp.where(in_grp, ee / jnp.sum(ee, axis=-1, keepdims=True), -1.0)
    p1 = jnp.max(prob, axis=-1, keepdims=True)
    j1 = jnp.min(jnp.where(prob == p1, lane, big), axis=-1, keepdims=True)
    prob2 = jnp.where(lane == j1, -1.0, prob)
    p2 = jnp.max(prob2, axis=-1, keepdims=True)
    j2 = jnp.min(jnp.where(prob2 == p2, lane, big), axis=-1, keepdims=True)
    psum = p1 + p2
    route = jnp.where(lane == 0, grp_w * p1 / psum, 0.0)
    route = jnp.where(lane == 1, grp_w * p2 / psum, route)
    route = jnp.where(lane == 2, (j1 - MOE_GROUPS).astype(F32), route)
    route = jnp.where(lane == 3, (j2 - MOE_GROUPS).astype(F32), route)
    route_ref[...] = jnp.broadcast_to(route, route_ref.shape)


def _merge(x2, attn, sgu, s5, bg, w_attn_o, w_sgu_o, w_s5_o, w_mix_o, norm_g, rgw, rgb, rew, reb, tm):
    t = x2.shape[0]
    wr = jnp.concatenate([rgw, rew, jnp.zeros((D_MODEL, LANES - MOE_GROUPS - MOE_EXPERTS), F32)], axis=1)
    wr_hi = wr.astype(BF16)
    wr_lo = (wr - wr_hi.astype(F32)).astype(BF16)
    br = jnp.concatenate([rgb, reb, jnp.zeros((LANES - MOE_GROUPS - MOE_EXPERTS,), F32)]).reshape(1, LANES)
    ws = [w.astype(BF16) for w in (w_attn_o, w_sgu_o, w_s5_o, w_mix_o)]
    g = norm_g.reshape(1, D_MODEL)
    row = lambda width: pl.BlockSpec((tm, width), lambda i: (i, 0))
    full = lambda a: pl.BlockSpec(a.shape, lambda i: (0,) * a.ndim)
    return pl.pallas_call(
        _merge_kernel, grid=(t // tm,),
        in_specs=[row(D_MODEL), row(NSA_WIDTH), row(SGU_WIDTH), row(S5_WIDTH), row(3 * D_MODEL)]
                 + [full(w) for w in ws] + [full(g), full(wr_hi), full(wr_lo), full(br)],
        out_specs=(row(D_MODEL), row(D_MODEL), row(LANES)),
        out_shape=(jax.ShapeDtypeStruct((t, D_MODEL), F32), jax.ShapeDtypeStruct((t, D_MODEL), F32),
                   jax.ShapeDtypeStruct((t, LANES), F32)),
        compiler_params=_cparams("parallel"), name="merge_route",
    )(x2, attn, sgu, s5, bg, *ws, g, wr_hi, wr_lo, br)


def _start_row_gather(src_hbm, rows_of, dst_buf, sem, n):
    for r in range(n):
        pltpu.make_async_copy(src_hbm.at[pl.ds(rows_of(r), 1)], dst_buf.at[pl.ds(r, 1)], sem).start(priority=r % 2)


def _wait_row_gather(src_hbm, dst_buf, sem, n):
    pltpu.make_async_copy(src_hbm.at[pl.ds(0, n)], dst_buf, sem).wait()


def _plan_kernel(route_ref, dest_ref, cnt_ref, carry_sc, *, bm):
    phase, i = pl.program_id(0), pl.program_id(1)
    tp = route_ref.shape[0]
    lane = lax.broadcasted_iota(jnp.int32, (1, LANES), 1)
    route = route_ref[...]
    oh0 = jnp.where(lane == route[:, 2:3].astype(jnp.int32), 1.0, 0.0)
    oh1 = jnp.where(lane == route[:, 3:4].astype(jnp.int32), 1.0, 0.0)
    both = oh0 + oh1
    col_sum = jnp.broadcast_to(jnp.sum(both, axis=0, keepdims=True), carry_sc.shape)

    @pl.when((phase == 0) & (i == 0))
    def _():
        carry_sc[...] = jnp.zeros(carry_sc.shape, F32)

    @pl.when(phase == 0)
    def _():
        carry_sc[...] = carry_sc[...] + col_sum

    @pl.when((phase == 1) & (i == 0))
    def _():
        counts = carry_sc[...]
        cnt_ref[...] = counts
        padded = jnp.floor((counts + (bm - 1)) * (1.0 / bm)) * bm
        incl = padded
        shift = 1
        while shift < LANES:
            incl = incl + jnp.where(lane >= shift, pltpu.roll(incl, shift, 1), 0.0)
            shift *= 2
        carry_sc[...] = incl - padded

    @pl.when(phase == 1)
    def _():
        r_idx = lax.broadcasted_iota(jnp.int32, (tp, tp), 0)
        c_idx = lax.broadcasted_iota(jnp.int32, (tp, tp), 1)
        earlier = jnp.where(c_idx < r_idx, 1.0, 0.0).astype(BF16)
        base = _dot(earlier, both.astype(BF16)) + carry_sc[0:1, :]
        d0 = jnp.sum(oh0 * base, axis=1, keepdims=True)
        d1 = jnp.sum(oh1 * base, axis=1, keepdims=True)
        dest_ref[...] = jnp.where(lane == 0, d0, jnp.where(lane == 1, d1, 0.0)).astype(jnp.int32)
        carry_sc[...] = carry_sc[...] + col_sum


def _dispatch_plan(route, bm, tp):
    t = route.shape[0]
    steps = t // tp
    dest, counts = pl.pallas_call(
        functools.partial(_plan_kernel, bm=bm), grid=(2, steps),
        in_specs=[pl.BlockSpec((tp, LANES), lambda p, i: (i, 0))],
        out_specs=(pl.BlockSpec((tp, LANES), lambda p, i: (i * p, 0)),
                   pl.BlockSpec((SUBLANES, LANES), lambda p, i: (0, 0))),
        out_shape=(jax.ShapeDtypeStruct((t, LANES), jnp.int32), jax.ShapeDtypeStruct((SUBLANES, LANES), F32)),
        scratch_shapes=[pltpu.VMEM((SUBLANES, LANES), F32)],
        compiler_params=_cparams("arbitrary", "arbitrary"), name="moe_plan",
    )(route)
    nb = t * MOE_TOPK // bm + MOE_EXPERTS
    nblk = (counts[0, :MOE_EXPERTS].astype(jnp.int32) + (bm - 1)) // bm
    blk_end = jnp.cumsum(nblk)
    blk_exp = jnp.sum((blk_end[None, :] <= jnp.arange(nb, dtype=jnp.int32)[:, None]).astype(jnp.int32), axis=1)
    return dest[:, :MOE_TOPK], jnp.minimum(blk_exp, MOE_EXPERTS - 1), nb


def _scatter_kernel(rows_ref, h_ref, zero_hbm, x_hbm, sem, *, tm):
    del zero_hbm
    n = MOE_TOPK * tm
    for r in range(n):
        pltpu.make_async_copy(h_ref.at[pl.ds(r // MOE_TOPK, 1)], x_hbm.at[pl.ds(rows_ref[0, 0, r], 1)],
                              sem).start(priority=r % 2)
    for _ in range(MOE_TOPK):
        pltpu.make_async_copy(h_ref, x_hbm.at[pl.ds(0, tm)], sem).wait()


def _scatter_rows(h, dest, nb, bm, tm):
    t = h.shape[0]
    steps = t // tm
    rows = dest.reshape(steps, 1, MOE_TOPK * tm)
    x_buf = jnp.zeros((nb * bm, D_MODEL), F32)
    return pl.pallas_call(
        functools.partial(_scatter_kernel, tm=tm), grid=(steps,),
        in_specs=[pl.BlockSpec((1, 1, MOE_TOPK * tm), lambda i: (i, 0, 0), memory_space=pltpu.SMEM),
                  pl.BlockSpec((tm, D_MODEL), lambda i: (i, 0)),
                  pl.BlockSpec(memory_space=pl.ANY)],
        out_specs=pl.BlockSpec(memory_space=pl.ANY),
        out_shape=jax.ShapeDtypeStruct((nb * bm, D_MODEL), F32),
        scratch_shapes=[pltpu.SemaphoreType.DMA(())],
        input_output_aliases={2: 0},
        compiler_params=_cparams("arbitrary"), name="moe_dispatch",
    )(rows, h, x_buf)


def _moe_kernel(blk_exp_ref, x_ref, wg_ref, wu_ref, wd_ref, y_ref):
    del blk_exp_ref
    xb = x_ref[...].astype(BF16)
    gate = _dot(xb, wg_ref[0])
    hid = gate * _sigmoid(gate) * _dot(xb, wu_ref[0])
    y_ref[...] = _dot(hid.astype(BF16), wd_ref[0])


def _experts(h, route, w_gate, w_up, w_down, bm, tp, tm):
    dest, blk_exp, nb = _dispatch_plan(route, bm, tp)
    x_buf = _scatter_rows(h, dest, nb, bm, tm)
    wspec = lambda shape: pl.BlockSpec((1,) + shape, lambda i, be: (be[i], 0, 0))
    grid_spec = pltpu.PrefetchScalarGridSpec(
        num_scalar_prefetch=1, grid=(nb,),
        in_specs=[pl.BlockSpec((bm, D_MODEL), lambda i, be: (i, 0)),
                  wspec((D_MODEL, EXPERT_HIDDEN)), wspec((D_MODEL, EXPERT_HIDDEN)),
                  wspec((EXPERT_HIDDEN, D_MODEL))],
        out_specs=pl.BlockSpec((bm, D_MODEL), lambda i, be: (i, 0)))
    y_blocks = pl.pallas_call(
        _moe_kernel, grid_spec=grid_spec,
        out_shape=jax.ShapeDtypeStruct((nb * bm, D_MODEL), F32),
        compiler_params=_cparams("parallel"), name="moe_experts",
    )(blk_exp, x_buf, w_gate.astype(BF16), w_up.astype(BF16), w_down.astype(BF16))
    return y_blocks, dest.reshape(-1)


def _combine_kernel(first_ref, next_ref, x_ref, route_ref, g_ref, y_hbm, o_ref, ybuf, sem, *, tm, final_norm):
    i = pl.program_id(0)
    cur = lax.rem(i, 2)
    nxt = 1 - cur
    n = MOE_TOPK * tm

    def row_of(ref):
        return lambda r: ref[0, 0, r]

    @pl.when(i == 0)
    def _():
        _start_row_gather(y_hbm, row_of(first_ref), ybuf.at[0], sem.at[0], n)

    _start_row_gather(y_hbm, row_of(next_ref), ybuf.at[nxt], sem.at[nxt], n)
    _wait_row_gather(y_hbm, ybuf.at[cur], sem.at[cur], n)
    route = route_ref[...]
    x = x_ref[...] + route[:, 0:1] * ybuf[cur, 0:tm, :] + route[:, 1:2] * ybuf[cur, tm:n, :]
    o_ref[...] = _rms(x, g_ref[...]) if final_norm else x

    @pl.when(i == pl.num_programs(0) - 1)
    def _():
        _wait_row_gather(y_hbm, ybuf.at[nxt], sem.at[nxt], n)


def _combine(x2, y_blocks, dest, route, g, final_norm, tm):
    t = x2.shape[0]
    steps = t // tm
    rows = jnp.transpose(dest.reshape(steps, tm, MOE_TOPK), (0, 2, 1)).reshape(steps, 1, MOE_TOPK * tm)
    g2 = g.reshape(1, D_MODEL)
    row = lambda width: pl.BlockSpec((tm, width), lambda i: (i, 0))
    smem_rows = lambda index_map: pl.BlockSpec((1, 1, MOE_TOPK * tm), index_map, memory_space=pltpu.SMEM)
    return pl.pallas_call(
        functools.partial(_combine_kernel, tm=tm, final_norm=final_norm), grid=(steps,),
        in_specs=[smem_rows(lambda i: (0, 0, 0)), smem_rows(lambda i: (jnp.minimum(i + 1, steps - 1), 0, 0)),
                  row(D_MODEL), row(LANES), pl.BlockSpec((1, D_MODEL), lambda i: (0, 0)),
                  pl.BlockSpec(memory_space=pl.ANY)],
        out_specs=row(D_MODEL), out_shape=jax.ShapeDtypeStruct((t, D_MODEL), F32),
        scratch_shapes=[pltpu.VMEM((2, MOE_TOPK * tm, D_MODEL), F32), pltpu.SemaphoreType.DMA((2,))],
        compiler_params=_cparams("arbitrary"), name="moe_combine",
    )(rows, rows, x2, route, g2, y_blocks)


def _tiles(batch, seq):
    t = batch * seq
    return dict(
        tm_proj=min(256, t), tq=min(256, seq), tk=min(256, seq), tm_sgu=min(512, t),
        s5_steps=min(128, seq), tm_merge=min(256, t), bm=256, tp_plan=min(512, t), tm_scat=min(256, t),
        tm_comb=min(256, t))


def kernel(x, positions, norm_mix_g, w_in, cmp_pos_k, cmp_pos_v, cmp_k_w1, cmp_k_w2, cmp_v_w1, cmp_v_w2, w_attn_o, sgu_norm_g, sgu_w, sgu_b, w_sgu_o, s5_a_re, s5_a_im, s5_log_step, s5_b_re, s5_b_im, s5_c_re, s5_c_im, s5_d, s5_glu_w, s5_glu_b, w_s5_o, w_mix_o, norm_ffn_g, router_group_w, router_group_b, router_expert_w, router_expert_b, expert_w_gate, expert_w_up, expert_w_down, norm_final_g):
    batch, seq, _ = x.shape
    depth = w_in.shape[0]
    cfg = _tiles(batch, seq)
    tables = _rope_tables(positions)
    x2 = x.reshape(batch * seq, D_MODEL)
    for l in range(depth):
        qt, qrt, kvc, ksel, kwin, vt, gt, sgu_in, s5_in, bg = _in_proj(
            x2, norm_mix_g[l].reshape(1, D_MODEL), _prep_in_proj(w_in[l]), tables, seq, cfg["tm_proj"])
        kc, vct = _compress(kvc, _prep_compress(cmp_pos_k[l], cmp_pos_v[l], cmp_k_w1[l], cmp_k_w2[l],
                                                cmp_v_w1[l], cmp_v_w2[l]), batch, seq)
        attn = _attention(qt, qrt, gt, kc, vct, ksel, kwin, vt, batch, seq, cfg["tq"], cfg["tk"])
        sgu = _sgu(sgu_in, sgu_norm_g[l], sgu_w[l], sgu_b[l], cfg["tm_sgu"])
        s5 = _s5(s5_in, _prep_s5(s5_a_re[l], s5_a_im[l], s5_log_step[l], s5_b_re[l], s5_b_im[l],
                                 s5_c_re[l], s5_c_im[l]),
                 s5_d[l], s5_glu_w[l], s5_glu_b[l], batch, seq, cfg["s5_steps"])
        x_mid, h, route = _merge(x2, attn, sgu, s5, bg, w_attn_o[l], w_sgu_o[l], w_s5_o[l], w_mix_o[l],
                                 norm_ffn_g[l], router_group_w[l], router_group_b[l],
                                 router_expert_w[l], router_expert_b[l], cfg["tm_merge"])
        y_blocks, dest = _experts(h, route, expert_w_gate[l], expert_w_up[l], expert_w_down[l],
                                  cfg["bm"], cfg["tp_plan"], cfg["tm_scat"])
        x2 = _combine(x_mid, y_blocks, dest, route, norm_final_g, l == depth - 1, cfg["tm_comb"])
    return x2.reshape(batch, seq, D_MODEL)
```

```python
import functools
import math

import numpy as np
import jax
import jax.numpy as jnp
from jax import lax
from jax.experimental import pallas as pl
from jax.experimental.pallas import tpu as pltpu

F32 = jnp.float32
BF16 = jnp.bfloat16

D_MODEL = 1024
HEAD_DIM = 64
NSA_HEADS = 8
NSA_KV_GROUPS = 2
HEADS_PER_GROUP = NSA_HEADS // NSA_KV_GROUPS
GROUP_WIDTH = HEADS_PER_GROUP * HEAD_DIM
NSA_WIDTH = NSA_HEADS * HEAD_DIM
KV_WIDTH = NSA_KV_GROUPS * HEAD_DIM
ROPE_DIM = HEAD_DIM // 4
ROPE_HALF = ROPE_DIM // 2
ROPE_THETA = 500000.0
CMP_LEN = 32
CMP_STRIDE = 16
CMP_HIDDEN = 128
SLC_LEN = 64
SLC_SHIFT = int(math.log2(SLC_LEN))
SLC_TOP = 16
WINDOW = 512
FORCE_SCORE = 1.0e4
NEG_INF = -1.0e30
SGU_WIDTH = 256
SGU_GROUPS = 4
SGU_CHUNK = 128
S5_WIDTH = 256
S5_GROUP_CH = 16
S5_GROUPS = S5_WIDTH // S5_GROUP_CH
S5_STATE = 64
S5_LANES = S5_GROUPS * S5_STATE
MOE_GROUPS = 4
MOE_EXPERTS_PER_GROUP = 8
MOE_EXPERTS = MOE_GROUPS * MOE_EXPERTS_PER_GROUP
MOE_TOPK = 2
EXPERT_HIDDEN = 512
ROUTER_ROWS = 40
RMS_EPS = 1e-6
ATTN_SCALE = HEAD_DIM ** -0.5
LOG2_E = math.log2(math.e)

LANES = 128
SUBLANES = 8
VMEM_LIMIT_BYTES = 56 * 1024 * 1024

BF16_SUBLANES = 16
V_ROWS = HEAD_DIM + BF16_SUBLANES
GATE_COLS = HEADS_PER_GROUP * 3
GATE_ROWS = 16

_OFF_Q = 0
_OFF_KVC = _OFF_Q + NSA_WIDTH
_OFF_KS = _OFF_KVC + 2 * KV_WIDTH
_OFF_VS = _OFF_KS + KV_WIDTH
_OFF_KW = _OFF_VS + KV_WIDTH
_OFF_VW = _OFF_KW + KV_WIDTH
_OFF_GATE = _OFF_VW + KV_WIDTH
_OFF_SGU = _OFF_GATE + LANES
_OFF_S5 = _OFF_SGU + 2 * SGU_WIDTH
_OFF_BG = _OFF_S5 + S5_WIDTH
_IN_W = _OFF_BG + 3 * D_MODEL


def _cparams(*sem):
    return pltpu.CompilerParams(dimension_semantics=sem, vmem_limit_bytes=VMEM_LIMIT_BYTES)


def _gelu(x):
    return 0.5 * x * (1.0 + jnp.tanh(math.sqrt(2.0 / math.pi) * (x + 0.044715 * (x * x * x))))


def _sigmoid(x):
    return 1.0 / (1.0 + jnp.exp(-x))


def _dot(a, b):
    return jnp.dot(a, b, preferred_element_type=F32)


def _dot_nt(a, b):
    return lax.dot_general(a, b, (((1,), (1,)), ((), ())), preferred_element_type=F32)


def _split_bf16(x):
    hi = x.astype(BF16)
    lo = (x - hi.astype(F32)).astype(BF16)
    return hi, lo


def _rms(x, g):
    return x * lax.rsqrt(jnp.mean(x * x, axis=-1, keepdims=True) + RMS_EPS) * g


def _rope(x, c, s_lo, s_hi):
    n = x.shape[-1]
    return x * c + pltpu.roll(x, n - ROPE_HALF, 1) * s_lo + pltpu.roll(x, ROPE_HALF, 1) * s_hi


def _in_proj_kernel(x_ref, g_ref, w_ref, c_ref, slo_ref, shi_ref,
                    qt_ref, qrt_ref, kvc_ref, ksel_ref, kwin_ref, vt_ref, gt_ref, sgu_ref, s5_ref, bg_ref, *, seq):
    tm = x_ref.shape[0]
    x = x_ref[...]
    hb = _rms(x, g_ref[...]).astype(BF16)

    def proj(off, width):
        return _dot(hb, w_ref[:, off:off + width])

    c, s_lo, s_hi = c_ref[...], slo_ref[...], shi_ref[...]
    rep = NSA_WIDTH // LANES
    q = proj(_OFF_Q, NSA_WIDTH)
    qt_ref[...] = (q * ATTN_SCALE).T
    qrot = _rope(q, jnp.concatenate([c] * rep, axis=1), jnp.concatenate([s_lo] * rep, axis=1),
                 jnp.concatenate([s_hi] * rep, axis=1))
    qrt_ref[...] = (qrot * (ATTN_SCALE * LOG2_E)).T.astype(BF16)
    kvc_ref[...] = proj(_OFF_KVC, 2 * KV_WIDTH)
    ks = _rope(proj(_OFF_KS, KV_WIDTH), c, s_lo, s_hi)
    kw = _rope(proj(_OFF_KW, KV_WIDTH), c, s_lo, s_hi)
    pos = lax.rem(pl.program_id(0) * tm, seq) + lax.broadcasted_iota(jnp.int32, (tm, 1), 0)
    blk = lax.shift_right_logical(pos, SLC_SHIFT)
    blk_onehot = jnp.where(lax.broadcasted_iota(jnp.int32, (1, HEAD_DIM), 1) == blk, 1.0, 0.0)
    for g in range(NSA_KV_GROUPS):
        cols = slice(g * HEAD_DIM, (g + 1) * HEAD_DIM)
        ksel_ref[g] = jnp.concatenate([ks[:, cols], blk_onehot], axis=1).astype(BF16)
        kwin_ref[g] = kw[:, cols].astype(BF16)
    aug_rows = lax.broadcasted_iota(jnp.int32, (V_ROWS - HEAD_DIM, LANES), 0)
    ones_row = jnp.where(aug_rows == 0, 1.0, 0.0).astype(BF16)
    for p, off in enumerate((_OFF_VS, _OFF_VW)):
        vt = proj(off, KV_WIDTH).T.astype(BF16)
        for g in range(NSA_KV_GROUPS):
            for ch in range(tm // LANES):
                idx = p * NSA_KV_GROUPS + g
                vt_ref[idx, ch, 0:HEAD_DIM, :] = vt[g * HEAD_DIM:(g + 1) * HEAD_DIM, ch * LANES:(ch + 1) * LANES]
                vt_ref[idx, ch, HEAD_DIM:V_ROWS, :] = ones_row
    gt_ref[...] = _sigmoid(proj(_OFF_GATE, LANES)).T[:NSA_KV_GROUPS * GATE_ROWS]
    sgu_ref[...] = proj(_OFF_SGU, 2 * SGU_WIDTH)
    s5_ref[...] = proj(_OFF_S5, S5_WIDTH)
    for k in range(3):
        bg_ref[:, k * D_MODEL:(k + 1) * D_MODEL] = _sigmoid(
            proj(_OFF_BG + k * D_MODEL, D_MODEL)).astype(BF16)


def _prep_in_proj(w_in):
    sizes = (NSA_WIDTH,) + (KV_WIDTH,) * 6 + (3 * NSA_HEADS, 2 * SGU_WIDTH, S5_WIDTH, 3 * D_MODEL)
    offs = np.concatenate([[0], np.cumsum(sizes)])
    q, kc, vc, ks, vs, kw, vw, ng, sgu, s5, bg = [w_in[..., offs[i]:offs[i + 1]] for i in range(11)]
    pad = lambda n: jnp.zeros(w_in.shape[:-1] + (n,), w_in.dtype)
    gates = []
    for g in range(NSA_KV_GROUPS):
        gates += [ng[..., g * GATE_COLS:(g + 1) * GATE_COLS], pad(GATE_ROWS - GATE_COLS)]
    gates.append(pad(LANES - NSA_KV_GROUPS * GATE_ROWS))
    return jnp.concatenate([q, kc, vc, ks, vs, kw, vw] + gates + [sgu, s5, bg], axis=-1).astype(BF16)


def _layer_spec(a, layer):
    return pl.BlockSpec((None,) + a.shape[1:], lambda *_: (layer,) + (0,) * (a.ndim - 1))


def _rope_tables(positions):
    inv_freq = ROPE_THETA ** (-jnp.arange(ROPE_HALF, dtype=F32) / ROPE_HALF)
    ang = positions.astype(F32).reshape(-1, 1) * inv_freq
    cos, sin = jnp.cos(ang), jnp.sin(ang)
    t = ang.shape[0]
    rest = HEAD_DIM - ROPE_DIM
    c = jnp.concatenate([cos, cos, jnp.ones((t, rest), F32)], axis=1)
    s_lo = jnp.concatenate([-sin, jnp.zeros((t, HEAD_DIM - ROPE_HALF), F32)], axis=1)
    s_hi = jnp.concatenate([jnp.zeros((t, ROPE_HALF), F32), sin, jnp.zeros((t, rest), F32)], axis=1)
    rep = LANES // HEAD_DIM
    return tuple(jnp.concatenate([a] * rep, axis=1) for a in (c, s_lo, s_hi))


def _in_proj(x2, g, w_all, layer, tables, seq, tm):
    t = x2.shape[0]
    row = lambda width: pl.BlockSpec((tm, width), lambda i: (i, 0))
    col = lambda height: pl.BlockSpec((height, tm), lambda i: (0, i))
    full = lambda a: pl.BlockSpec(a.shape, lambda i: (0,) * a.ndim)
    n_kv = 2 * NSA_KV_GROUPS
    out_shape = (
        jax.ShapeDtypeStruct((NSA_WIDTH, t), F32),
        jax.ShapeDtypeStruct((NSA_WIDTH, t), BF16),
        jax.ShapeDtypeStruct((t, 2 * KV_WIDTH), F32),
        jax.ShapeDtypeStruct((NSA_KV_GROUPS, t, 2 * HEAD_DIM), BF16),
        jax.ShapeDtypeStruct((NSA_KV_GROUPS, t, HEAD_DIM), BF16),
        jax.ShapeDtypeStruct((n_kv, t // LANES, V_ROWS, LANES), BF16),
        jax.ShapeDtypeStruct((NSA_KV_GROUPS * GATE_ROWS, t), F32),
        jax.ShapeDtypeStruct((t, 2 * SGU_WIDTH), F32),
        jax.ShapeDtypeStruct((seq, (t // seq) * S5_WIDTH), F32),
        jax.ShapeDtypeStruct((t, 3 * D_MODEL), BF16),
    )
    seq_tiles = seq // tm
    time_major = pl.BlockSpec((tm, S5_WIDTH), lambda i: (i % seq_tiles, i // seq_tiles))
    out_specs = (col(NSA_WIDTH), col(NSA_WIDTH), row(2 * KV_WIDTH),
                 pl.BlockSpec((NSA_KV_GROUPS, tm, 2 * HEAD_DIM), lambda i: (0, i, 0)),
                 pl.BlockSpec((NSA_KV_GROUPS, tm, HEAD_DIM), lambda i: (0, i, 0)),
                 pl.BlockSpec((n_kv, tm // LANES, V_ROWS, LANES), lambda i: (0, i, 0, 0)),
                 col(NSA_KV_GROUPS * GATE_ROWS), row(2 * SGU_WIDTH), time_major, row(3 * D_MODEL))
    return pl.pallas_call(
        functools.partial(_in_proj_kernel, seq=seq), grid=(t // tm,),
        in_specs=[row(D_MODEL), full(g), _layer_spec(w_all, layer), row(LANES), row(LANES), row(LANES)],
        out_specs=out_specs, out_shape=out_shape,
        compiler_params=_cparams("parallel"), name="in_proj",
    )(x2, g, w_all, *tables)


def _compress_kernel(r_ref, wa_ref, wb_ref, pos_ref, w1_ref, w2_ref, w2t_ref, kc_ref, vct_ref):
    rb = r_ref[0].astype(BF16)
    n = rb.shape[0]
    a = _dot(rb, wa_ref[...])
    b = _dot(rb, wb_ref[...])
    b = pltpu.roll(b, n - 1, 0)
    rows = lax.broadcasted_iota(jnp.int32, b.shape, 0)
    b = jnp.where(rows == n - 1, 0.0, b)
    for m in range(2 * NSA_KV_GROUPS):
        typ, g = divmod(m, NSA_KV_GROUPS)
        bias = _dot(pos_ref[typ].astype(BF16), w1_ref[typ])
        hid = a[:, m * CMP_HIDDEN:(m + 1) * CMP_HIDDEN] + b[:, m * CMP_HIDDEN:(m + 1) * CMP_HIDDEN] + bias
        act = _gelu(hid).astype(BF16)
        if typ == 0:
            kc_ref[0, g] = _dot(act, w2_ref[...])
        else:
            vct_ref[0, g] = _dot_nt(w2t_ref[...], act)


def _prep_compress(cmp_pos_k, cmp_pos_v, k_w1, k_w2, v_w1, v_w2):
    streams = 2 * NSA_KV_GROUPS
    w1 = jnp.stack([k_w1, v_w1]).reshape(2, CMP_LEN, HEAD_DIM, CMP_HIDDEN)

    def expand(half):
        out = jnp.zeros((CMP_STRIDE, streams, HEAD_DIM, streams, CMP_HIDDEN), F32)
        for m in range(streams):
            out = out.at[:, m, :, m, :].set(w1[m // NSA_KV_GROUPS, half * CMP_STRIDE:(half + 1) * CMP_STRIDE])
        return out.reshape(CMP_STRIDE * streams * HEAD_DIM, streams * CMP_HIDDEN).astype(BF16)

    pos = jnp.stack([cmp_pos_k, cmp_pos_v]).reshape(2, 1, CMP_LEN * HEAD_DIM)
    return (expand(0), expand(1), pos, jnp.stack([k_w1, v_w1]).astype(BF16),
            k_w2.astype(BF16), v_w2.T.astype(BF16))


def _compress(kvc, prep, batch, seq):
    n = seq // CMP_STRIDE
    r = kvc.reshape(batch, n, CMP_STRIDE * 2 * KV_WIDTH)
    full = lambda a: pl.BlockSpec(a.shape, lambda b: (0,) * a.ndim)
    return pl.pallas_call(
        _compress_kernel, grid=(batch,),
        in_specs=[pl.BlockSpec((1, n, r.shape[-1]), lambda b: (b, 0, 0))] + [full(a) for a in prep],
        out_specs=(pl.BlockSpec((1, NSA_KV_GROUPS, n, HEAD_DIM), lambda b: (b, 0, 0, 0)),
                   pl.BlockSpec((1, NSA_KV_GROUPS, HEAD_DIM, n), lambda b: (b, 0, 0, 0))),
        out_shape=(jax.ShapeDtypeStruct((batch, NSA_KV_GROUPS, n, HEAD_DIM), F32),
                   jax.ShapeDtypeStruct((batch, NSA_KV_GROUPS, HEAD_DIM, n), F32)),
        compiler_params=_cparams("parallel"), name="compress_kv",
    )(r, *prep)


def _attn_kernel(qt_ref, qrt_ref, gt_ref, kc_ref, vct_ref, ks_ref, vst_ref, kw_ref, vwt_ref, ovt_ref,
                 o_ref, qa_sc, m_sc, acc_sc, ow_sc, s_sc, p_sc, a_sc, *, tq, tk, seq):
    n_cmp = kc_ref.shape[2]
    n_slc = seq // SLC_LEN
    heads = HEADS_PER_GROUP
    t0 = pl.program_id(2) * tq
    t_lane = t0 + lax.broadcasted_iota(jnp.int32, (1, tq), 1)

    def lane_tile(a):
        return jnp.concatenate([a] * heads, axis=1)

    def head_cols(ref):
        return jnp.concatenate([ref[r * HEAD_DIM:(r + 1) * HEAD_DIM, :] for r in range(heads)], axis=1)

    q_hi, q_lo = _split_bf16(head_cols(qt_ref))
    qr = head_cols(qrt_ref)

    slab = min(WINDOW + tq, seq)
    st = pl.multiple_of(jnp.clip(t0 - WINDOW, 0, seq - slab), LANES)
    s = _dot(kw_ref[0, pl.ds(st, slab), :], qr)
    diff = t_lane - (st + lax.broadcasted_iota(jnp.int32, (slab, 1), 0))
    s = s + lane_tile(jnp.where((diff >= 0) & (diff < WINDOW), 0.0, NEG_INF))
    p = jnp.exp2(s - jnp.max(s, axis=0, keepdims=True))
    st_tile = st // LANES
    vw_t = jnp.concatenate([vwt_ref[0, st_tile + c] for c in range(slab // LANES)], axis=1)
    pv = _dot(vw_t, p.astype(BF16))
    ow_sc[...] = pv[0:HEAD_DIM] / pv[HEAD_DIM:HEAD_DIM + 1]

    kc_hi, kc_lo = _split_bf16(kc_ref[0, 0])
    s = _dot(kc_hi, q_hi) + _dot(kc_lo, q_hi) + _dot(kc_hi, q_lo)
    cmp_end = lax.broadcasted_iota(jnp.int32, (n_cmp, 1), 0) * CMP_STRIDE + (CMP_LEN - 1)
    m_c = cmp_end <= lane_tile(t_lane)
    s = jnp.where(m_c, s, NEG_INF)
    e = jnp.where(m_c, jnp.exp(s - jnp.max(s, axis=0, keepdims=True)), 0.0)
    p = e / jnp.maximum(jnp.sum(e, axis=0, keepdims=True), 1e-30)
    o_cmp = _dot(vct_ref[0, 0].astype(BF16), p.astype(BF16))
    p_sum = p[:, 0:tq]
    for r in range(1, heads):
        p_sum = p_sum + p[:, r * tq:(r + 1) * tq]
    ps_hi, ps_lo = _split_bf16(p_sum)
    imp = _dot(ovt_ref[...], ps_hi) + _dot(ovt_ref[...], ps_lo)

    j_idx = lax.broadcasted_iota(jnp.int32, (n_slc, 1), 0)
    blk_t = lax.shift_right_logical(t_lane, SLC_SHIFT)
    causal = j_idx <= blk_t
    forced = (j_idx == 0) | (j_idx == blk_t) | (j_idx == blk_t - 1)
    imp = jnp.where(forced, FORCE_SCORE, imp)
    imp = jnp.where(causal, imp, NEG_INF)
    picked = jnp.zeros((n_slc, tq), F32)
    j_f32 = j_idx.astype(F32)
    for _ in range(min(SLC_TOP, n_slc)):
        best = jnp.max(imp, axis=0, keepdims=True)
        first = jnp.min(jnp.where(imp == best, j_f32, float(n_slc)), axis=0, keepdims=True)
        hit = j_f32 == first
        picked = jnp.where(hit, 1.0, picked)
        imp = jnp.where(hit, -jnp.inf, imp)
    bias = jnp.where((picked > 0.5) & causal, 0.0, NEG_INF)
    if n_slc < HEAD_DIM:
        bias = jnp.concatenate([bias, jnp.zeros((HEAD_DIM - n_slc, tq), F32)], axis=0)
    qa_sc[...] = jnp.concatenate([qr, lane_tile(bias.astype(BF16))], axis=0)

    m_sc[...] = jnp.full(m_sc.shape, NEG_INF, F32)
    acc_sc[...] = jnp.zeros(acc_sc.shape, F32)
    p_sc[1] = jnp.zeros(p_sc.shape[1:], BF16)
    a_sc[1] = jnp.ones(a_sc.shape[1:], F32)
    lanes_per_tile = tk // LANES

    def scores(j, slot):
        k0 = j * tk if isinstance(j, int) else pl.multiple_of(j * tk, tk)
        s_sc[slot] = _dot(ks_ref[0, pl.ds(k0, tk), :], qa_sc[...])

    def softmax(j, slot, diagonal):
        s = s_sc[slot]
        if diagonal:
            key_pos = j * tk + lax.broadcasted_iota(jnp.int32, (tk, 1), 0)
            s = s + lane_tile(jnp.where(key_pos <= t_lane, 0.0, NEG_INF))
        m_old = m_sc[...]
        m_new = jnp.maximum(m_old, jnp.max(s, axis=0, keepdims=True))
        m_sc[...] = m_new
        a_sc[slot] = jnp.exp2(m_old - m_new)
        p_sc[slot] = jnp.exp2(s - m_new).astype(BF16)

    def accumulate(j, slot):
        tile = jnp.maximum(j, 0) * lanes_per_tile
        v_t = jnp.concatenate([vst_ref[0, tile + c] for c in range(lanes_per_tile)], axis=1)
        acc_sc[...] = a_sc[slot] * acc_sc[...] + _dot(v_t, p_sc[slot])

    def pipe_step(i, i_is_odd):
        a, b = (1, 0) if i_is_odd else (0, 1)
        accumulate(i - 2, a)
        softmax(i - 1, b, False)
        scores(i, a)

    n_kv = (t0 + tq + tk - 1) // tk
    scores(0, 0)

    def two_steps(u, carry):
        pipe_step(2 * u + 1, True)
        pipe_step(2 * u + 2, False)
        return carry

    lax.fori_loop(0, (n_kv - 1) // 2, two_steps, 0)

    @pl.when(((n_kv - 1) & 1) == 1)
    def _():
        pipe_step(n_kv - 1, True)

    last = (n_kv - 1) & 1
    accumulate(n_kv - 2, 1 - last)
    softmax(n_kv - 1, last, True)
    accumulate(n_kv - 1, last)
    o_slc = acc_sc[0:HEAD_DIM, :] / acc_sc[HEAD_DIM:HEAD_DIM + 1, :]
    o_win = ow_sc[...]

    gates = gt_ref[...]
    outs = []
    for r in range(heads):
        cols = slice(r * tq, (r + 1) * tq)
        outs.append(gates[3 * r:3 * r + 1, :] * o_cmp[:, cols] + gates[3 * r + 1:3 * r + 2, :] * o_slc[:, cols]
                    + gates[3 * r + 2:3 * r + 3, :] * o_win[:, cols])
    o_ref[...] = jnp.concatenate(outs, axis=0).T.astype(o_ref.dtype)


def _overlap_matrix_t(n_cmp, n_slc):
    c0 = np.arange(n_cmp)[None, :] * CMP_STRIDE
    s0 = np.arange(n_slc)[:, None] * SLC_LEN
    ov = np.clip(np.minimum(c0 + CMP_LEN, s0 + SLC_LEN) - np.maximum(c0, s0), 0, None) / CMP_LEN
    ov[:, n_cmp - 1] = 0.0
    return jnp.asarray(ov, BF16)


def _attention(qt, qrt, gt, kc, vct, ksel, kwin, vt, batch, seq, tq, tk):
    assert tk % tq == 0 and seq // SLC_LEN <= HEAD_DIM
    n_cmp = seq // CMP_STRIDE
    n_slc = seq // SLC_LEN
    nq = seq // tq
    ksel = ksel.reshape(NSA_KV_GROUPS, batch, seq, 2 * HEAD_DIM)
    kwin = kwin.reshape(NSA_KV_GROUPS, batch, seq, HEAD_DIM)
    vt = vt.reshape(2 * NSA_KV_GROUPS, batch, seq // LANES, V_ROWS, LANES)
    ovt = _overlap_matrix_t(n_cmp, n_slc)
    qspec = pl.BlockSpec((GROUP_WIDTH, tq), lambda b, g, i: (g, b * nq + i))
    k_spec = lambda width: pl.BlockSpec((None, 1, seq, width), lambda b, g, i: (g, b, 0, 0))
    v_spec = lambda p: pl.BlockSpec((None, 1, seq // LANES, V_ROWS, LANES),
                                    lambda b, g, i: (p * NSA_KV_GROUPS + g, b, 0, 0, 0))
    kernel = functools.partial(_attn_kernel, tq=tq, tk=tk, seq=seq)
    return pl.pallas_call(
        kernel, grid=(batch, NSA_KV_GROUPS, nq),
        in_specs=[qspec, qspec, pl.BlockSpec((GATE_ROWS, tq), lambda b, g, i: (g, b * nq + i)),
                  pl.BlockSpec((1, 1, n_cmp, HEAD_DIM), lambda b, g, i: (b, g, 0, 0)),
                  pl.BlockSpec((1, 1, HEAD_DIM, n_cmp), lambda b, g, i: (b, g, 0, 0)),
                  k_spec(2 * HEAD_DIM), v_spec(0), k_spec(HEAD_DIM), v_spec(1),
                  pl.BlockSpec(ovt.shape, lambda b, g, i: (0, 0))],
        out_specs=pl.BlockSpec((tq, GROUP_WIDTH), lambda b, g, i: (b * nq + i, g)),
        out_shape=jax.ShapeDtypeStruct((batch * seq, NSA_WIDTH), BF16),
        scratch_shapes=[pltpu.VMEM((2 * HEAD_DIM, HEADS_PER_GROUP * tq), BF16),
                        pltpu.VMEM((1, HEADS_PER_GROUP * tq), F32),
                        pltpu.VMEM((V_ROWS, HEADS_PER_GROUP * tq), F32),
                        pltpu.VMEM((HEAD_DIM, HEADS_PER_GROUP * tq), F32),
                        pltpu.VMEM((2, tk, HEADS_PER_GROUP * tq), F32),
                        pltpu.VMEM((2, tk, HEADS_PER_GROUP * tq), BF16),
                        pltpu.VMEM((2, 1, HEADS_PER_GROUP * tq), F32)],
        compiler_params=_cparams("parallel", "parallel", "arbitrary"), name="nsa_attention",
    )(qt, qrt, gt, kc, vct, ksel, vt, kwin, vt, ovt)


def _sgu_kernel(uv_ref, g_ref, w_ref, b_ref, o_ref, *, chunks):
    z = _gelu(uv_ref[...])
    u = z[:, :SGU_WIDTH]
    v = _rms(z[:, SGU_WIDTH:], g_ref[...])
    gw = SGU_WIDTH // SGU_GROUPS
    rows = lax.broadcasted_iota(jnp.int32, (SGU_CHUNK, SGU_GROUPS * SGU_CHUNK), 0)
    cols = lax.broadcasted_iota(jnp.int32, (SGU_CHUNK, SGU_GROUPS * SGU_CHUNK), 1)
    w = jnp.where((cols & (SGU_CHUNK - 1)) <= rows, w_ref[...], 0.0).astype(BF16)
    grp_r = lax.broadcasted_iota(jnp.int32, (SGU_GROUPS * SGU_CHUNK, SGU_WIDTH), 0) // SGU_CHUNK
    grp_c = lax.broadcasted_iota(jnp.int32, (SGU_GROUPS * SGU_CHUNK, SGU_WIDTH), 1) // gw
    for c in range(chunks):
        vc = v[c * SGU_CHUNK:(c + 1) * SGU_CHUNK].astype(BF16)
        v_bd = jnp.where(grp_r == grp_c, jnp.concatenate([vc] * SGU_GROUPS, axis=0), jnp.zeros((), BF16))
        mixed = _dot(w, v_bd) + b_ref[...]
        o_ref[c * SGU_CHUNK:(c + 1) * SGU_CHUNK, :] = (u[c * SGU_CHUNK:(c + 1) * SGU_CHUNK] * mixed).astype(o_ref.dtype)


def _sgu(sgu_in, norm_g, w_s, b_s, tm):
    t = sgu_in.shape[0]
    w_cat = jnp.transpose(w_s, (1, 0, 2)).reshape(SGU_CHUNK, SGU_GROUPS * SGU_CHUNK)
    bias = jnp.repeat(b_s.T, SGU_WIDTH // SGU_GROUPS, axis=1)
    g = norm_g.reshape(1, SGU_WIDTH)
    full = lambda a: pl.BlockSpec(a.shape, lambda i: (0,) * a.ndim)
    return pl.pallas_call(
        functools.partial(_sgu_kernel, chunks=tm // SGU_CHUNK), grid=(t // tm,),
        in_specs=[pl.BlockSpec((tm, 2 * SGU_WIDTH), lambda i: (i, 0)), full(g), full(w_cat), full(bias)],
        out_specs=pl.BlockSpec((tm, SGU_WIDTH), lambda i: (i, 0)),
        out_shape=jax.ShapeDtypeStruct((t, SGU_WIDTH), BF16),
        compiler_params=_cparams("parallel"), name="sgu",
    )(sgu_in, g, w_cat, bias)


def _s5_kernel(x_ref, bre_ref, bim_ref, are_ref, aim_ref, cre_ref, cim_ref, d_ref, gw_ref, gb_ref,
               o_ref, hre_sc, him_sc, ure_sc, uim_sc, *, steps, batch):
    @pl.when(pl.program_id(0) == 0)
    def _():
        hre_sc[...] = jnp.zeros(hre_sc.shape, F32)
        him_sc[...] = jnp.zeros(him_sc.shape, F32)

    x = x_ref[...]
    xb = x.astype(BF16)
    ure_sc[...] = _dot(xb, bre_ref[...])
    uim_sc[...] = _dot(xb, bim_ref[...])
    a_re = jnp.broadcast_to(are_ref[...], (batch, S5_LANES))
    a_im = jnp.broadcast_to(aim_ref[...], (batch, S5_LANES))

    def step(t, carry):
        h_re, h_im = carry
        rows = pl.ds(pl.multiple_of(t * batch, batch), batch)
        n_re = a_re * h_re - a_im * h_im + ure_sc[rows, :]
        n_im = a_re * h_im + a_im * h_re + uim_sc[rows, :]
        ure_sc[rows, :] = n_re
        uim_sc[rows, :] = n_im
        return n_re, n_im

    h_re, h_im = lax.fori_loop(0, steps, step, (hre_sc[...], him_sc[...]))
    hre_sc[...] = h_re
    him_sc[...] = h_im
    y = _dot(ure_sc[...].astype(BF16), cre_ref[...]) - _dot(uim_sc[...].astype(BF16), cim_ref[...])
    y = _gelu(y + d_ref[...] * x)
    y = y * _sigmoid(_dot(y.astype(BF16), gw_ref[...]) + gb_ref[...])
    o_ref[...] = y.astype(o_ref.dtype)


def _block_diag(blocks):
    g, r, c = blocks.shape
    eye = jnp.eye(g, dtype=blocks.dtype)
    return (blocks[:, :, None, :] * eye[:, None, :, None]).reshape(g * r, g * c)


def _prep_s5(a_re, a_im, log_step, b_re, b_im, c_re, c_im):
    step = jnp.exp(log_step)[:, None]
    mag = jnp.exp(a_re * step)
    abar_re, abar_im = mag * jnp.cos(a_im * step), mag * jnp.sin(a_im * step)
    den = a_re * a_re + a_im * a_im
    nr, ni = abar_re - 1.0, abar_im
    coef_re = (nr * a_re + ni * a_im) / den
    coef_im = (ni * a_re - nr * a_im) / den
    bbar_re = coef_re[..., None] * b_re - coef_im[..., None] * b_im
    bbar_im = coef_re[..., None] * b_im + coef_im[..., None] * b_re
    to_in = lambda b: _block_diag(jnp.transpose(b, (0, 2, 1))).astype(BF16)
    to_out = lambda c: _block_diag(jnp.transpose(c, (0, 2, 1))).astype(BF16)
    return (to_in(bbar_re), to_in(bbar_im), abar_re.reshape(1, S5_LANES), abar_im.reshape(1, S5_LANES),
            to_out(c_re), to_out(c_im))


def _s5(xs, prep, d, glu_w, glu_b, batch, seq, steps):
    bre, bim, are, aim, cre, cim = prep
    x_tm = xs.reshape(seq * batch, S5_WIDTH)
    rows = steps * batch
    d2, gb2, gwb = d.reshape(1, S5_WIDTH), glu_b.reshape(1, S5_WIDTH), glu_w.astype(BF16)
    full = lambda a: pl.BlockSpec(a.shape, lambda i: (0,) * a.ndim)
    y = pl.pallas_call(
        functools.partial(_s5_kernel, steps=steps, batch=batch), grid=(seq // steps,),
        in_specs=[pl.BlockSpec((rows, S5_WIDTH), lambda i: (i, 0)), full(bre), full(bim), full(are), full(aim),
                  full(cre), full(cim), full(d2), full(gwb), full(gb2)],
        out_specs=pl.BlockSpec((rows, S5_WIDTH), lambda i: (i, 0)),
        out_shape=jax.ShapeDtypeStruct((seq * batch, S5_WIDTH), BF16),
        scratch_shapes=[pltpu.VMEM((batch, S5_LANES), F32), pltpu.VMEM((batch, S5_LANES), F32),
                        pltpu.VMEM((rows, S5_LANES), F32), pltpu.VMEM((rows, S5_LANES), F32)],
        compiler_params=_cparams("arbitrary"), name="s5_scan",
    )(x_tm, bre, bim, are, aim, cre, cim, d2, gwb, gb2)
    return y.reshape(seq, batch * S5_WIDTH)


def _merge_kernel(x_ref, a_ref, b_ref, c_ref, bg_ref, wa_ref, wb_ref, wc_ref, wm_ref, g_ref,
                  wrh_ref, wrl_ref, br_ref, xo_ref, h_ref, route_ref):
    y_a = _dot(a_ref[...], wa_ref[...])
    y_b = _dot(b_ref[...], wb_ref[...])
    y_c = _dot(c_ref[...], wc_ref[...])
    merged = (bg_ref[:, :D_MODEL].astype(F32) * y_a + bg_ref[:, D_MODEL:2 * D_MODEL].astype(F32) * y_b
              + bg_ref[:, 2 * D_MODEL:].astype(F32) * y_c)
    x = x_ref[...] + _dot(merged.astype(BF16), wm_ref[...])
    xo_ref[...] = x
    h = _rms(x, g_ref[...])
    h_ref[...] = h

    h_hi, h_lo = _split_bf16(h)
    logits = (_dot_nt(wrh_ref[...], h_hi) + _dot_nt(wrl_ref[...], h_hi) + _dot_nt(wrh_ref[...], h_lo))
    logits = logits[0:ROUTER_ROWS] + br_ref[...]
    row = lax.broadcasted_iota(jnp.int32, (ROUTER_ROWS, 1), 0).astype(F32)
    big = float(ROUTER_ROWS)
    is_grp = row < MOE_GROUPS
    gl = jnp.where(is_grp, logits, -jnp.inf)
    gmax = jnp.max(gl, axis=0, keepdims=True)
    grp = jnp.min(jnp.where(gl == gmax, row, big), axis=0, keepdims=True)
    grp_w = 1.0 / jnp.sum(jnp.where(is_grp, jnp.exp(logits - gmax), 0.0), axis=0, keepdims=True)
    e_row = row - MOE_GROUPS
    in_grp = (e_row >= 0) & (jnp.floor(e_row * (1.0 / MOE_EXPERTS_PER_GROUP)) == grp)
    emax = jnp.max(jnp.where(in_grp, logits, -jnp.inf), axis=0, keepdims=True)
    ee = jnp.where(in_grp, jnp.exp(logits - emax), 0.0)
    prob = jnp.where(in_grp, ee / jnp.sum(ee, axis=0, keepdims=True), -1.0)
    p1 = jnp.max(prob, axis=0, keepdims=True)
    j1 = jnp.min(jnp.where(prob == p1, row, big), axis=0, keepdims=True)
    prob2 = jnp.where(row == j1, -1.0, prob)
    p2 = jnp.max(prob2, axis=0, keepdims=True)
    j2 = jnp.min(jnp.where(prob2 == p2, row, big), axis=0, keepdims=True)
    psum = p1 + p2
    out_row = lax.broadcasted_iota(jnp.int32, (LANES, 1), 0)
    route_t = jnp.where(out_row == 0, grp_w * p1 / psum, 0.0)
    route_t = jnp.where(out_row == 1, grp_w * p2 / psum, route_t)
    route_t = jnp.where(out_row == 2, j1 - MOE_GROUPS, route_t)
    route_t = jnp.where(out_row == 3, j2 - MOE_GROUPS, route_t)
    route_ref[...] = route_t.T


def _merge(x2, attn, sgu, s5, bg, ws, layer, norm_g, rgw, rgb, rew, reb, tm):
    t = x2.shape[0]
    seq_tiles = s5.shape[0] // tm
    wr = jnp.concatenate([rgw, rew, jnp.zeros((D_MODEL, LANES - MOE_GROUPS - MOE_EXPERTS), F32)], axis=1).T
    wr_hi = wr.astype(BF16)
    wr_lo = (wr - wr_hi.astype(F32)).astype(BF16)
    br = jnp.concatenate([rgb, reb, jnp.zeros((ROUTER_ROWS - MOE_GROUPS - MOE_EXPERTS,), F32)]).reshape(ROUTER_ROWS, 1)
    g = norm_g.reshape(1, D_MODEL)
    row = lambda width: pl.BlockSpec((tm, width), lambda i: (i, 0))
    full = lambda a: pl.BlockSpec(a.shape, lambda i: (0,) * a.ndim)
    return pl.pallas_call(
        _merge_kernel, grid=(t // tm,),
        in_specs=[row(D_MODEL), row(NSA_WIDTH), row(SGU_WIDTH),
                  pl.BlockSpec((tm, S5_WIDTH), lambda i: (i % seq_tiles, i // seq_tiles)),
                  row(3 * D_MODEL)]
                 + [_layer_spec(w, layer) for w in ws] + [full(g), full(wr_hi), full(wr_lo), full(br)],
        out_specs=(row(D_MODEL), row(D_MODEL), row(LANES)),
        out_shape=(jax.ShapeDtypeStruct((t, D_MODEL), F32), jax.ShapeDtypeStruct((t, D_MODEL), F32),
                   jax.ShapeDtypeStruct((t, LANES), F32)),
        compiler_params=_cparams("parallel"), name="merge_route",
    )(x2, attn, sgu, s5, bg, *ws, g, wr_hi, wr_lo, br)


def _start_row_gather(src_hbm, rows_of, dst_buf, sem, n):
    for r in range(n):
        pltpu.make_async_copy(src_hbm.at[pl.ds(rows_of(r), 1)], dst_buf.at[pl.ds(r, 1)], sem).start(priority=r % 2)


def _wait_row_gather(src_hbm, dst_buf, sem, n):
    pltpu.make_async_copy(src_hbm.at[pl.ds(0, n)], dst_buf, sem).wait()


def _plan_kernel(route_ref, dest_ref, cnt_ref, carry_sc, *, bm):
    phase, i = pl.program_id(0), pl.program_id(1)
    tp = route_ref.shape[0]
    lane = lax.broadcasted_iota(jnp.int32, (1, LANES), 1)
    route = route_ref[...]
    oh0 = jnp.where(lane == route[:, 2:3].astype(jnp.int32), 1.0, 0.0)
    oh1 = jnp.where(lane == route[:, 3:4].astype(jnp.int32), 1.0, 0.0)
    both = oh0 + oh1
    col_sum = jnp.broadcast_to(jnp.sum(both, axis=0, keepdims=True), carry_sc.shape)

    @pl.when((phase == 0) & (i == 0))
    def _():
        carry_sc[...] = jnp.zeros(carry_sc.shape, F32)

    @pl.when(phase == 0)
    def _():
        carry_sc[...] = carry_sc[...] + col_sum

    @pl.when((phase == 1) & (i == 0))
    def _():
        counts = carry_sc[...]
        cnt_ref[...] = counts
        padded = jnp.floor((counts + (bm - 1)) * (1.0 / bm)) * bm
        incl = padded
        shift = 1
        while shift < LANES:
            incl = incl + jnp.where(lane >= shift, pltpu.roll(incl, shift, 1), 0.0)
            shift *= 2
        carry_sc[...] = incl - padded

    @pl.when(phase == 1)
    def _():
        r_idx = lax.broadcasted_iota(jnp.int32, (tp, tp), 0)
        c_idx = lax.broadcasted_iota(jnp.int32, (tp, tp), 1)
        earlier = jnp.where(c_idx < r_idx, 1.0, 0.0).astype(BF16)
        base = _dot(earlier, both.astype(BF16)) + carry_sc[0:1, :]
        d0 = jnp.sum(oh0 * base, axis=1, keepdims=True)
        d1 = jnp.sum(oh1 * base, axis=1, keepdims=True)
        dest_ref[...] = jnp.where(lane == 0, d0, jnp.where(lane == 1, d1, 0.0)).astype(jnp.int32)
        carry_sc[...] = carry_sc[...] + col_sum


def _dispatch_plan(route, bm, tp):
    t = route.shape[0]
    steps = t // tp
    dest, counts = pl.pallas_call(
        functools.partial(_plan_kernel, bm=bm), grid=(2, steps),
        in_specs=[pl.BlockSpec((tp, LANES), lambda p, i: (i, 0))],
        out_specs=(pl.BlockSpec((tp, LANES), lambda p, i: (i * p, 0)),
                   pl.BlockSpec((SUBLANES, LANES), lambda p, i: (0, 0))),
        out_shape=(jax.ShapeDtypeStruct((t, LANES), jnp.int32), jax.ShapeDtypeStruct((SUBLANES, LANES), F32)),
        scratch_shapes=[pltpu.VMEM((SUBLANES, LANES), F32)],
        compiler_params=_cparams("arbitrary", "arbitrary"), name="moe_plan",
    )(route)
    nb = t * MOE_TOPK // bm + MOE_EXPERTS
    nblk = (counts[0, :MOE_EXPERTS].astype(jnp.int32) + (bm - 1)) // bm
    blk_end = jnp.cumsum(nblk)
    blk_exp = jnp.sum((blk_end[None, :] <= jnp.arange(nb, dtype=jnp.int32)[:, None]).astype(jnp.int32), axis=1)
    return dest[:, :MOE_TOPK], jnp.minimum(blk_exp, MOE_EXPERTS - 1), nb


def _scatter_kernel(rows_ref, h_ref, zero_hbm, x_hbm, sem, *, tm):
    del zero_hbm
    n = MOE_TOPK * tm
    for r in range(n):
        pltpu.make_async_copy(h_ref.at[pl.ds(r // MOE_TOPK, 1)], x_hbm.at[pl.ds(rows_ref[0, 0, r], 1)],
                              sem).start(priority=r % 2)
    for _ in range(MOE_TOPK):
        pltpu.make_async_copy(h_ref, x_hbm.at[pl.ds(0, tm)], sem).wait()


def _scatter_rows(h, dest, x_buf, tm):
    t = h.shape[0]
    steps = t // tm
    rows = dest.reshape(steps, 1, MOE_TOPK * tm)
    return pl.pallas_call(
        functools.partial(_scatter_kernel, tm=tm), grid=(steps,),
        in_specs=[pl.BlockSpec((1, 1, MOE_TOPK * tm), lambda i: (i, 0, 0), memory_space=pltpu.SMEM),
                  pl.BlockSpec((tm, D_MODEL), lambda i: (i, 0)),
                  pl.BlockSpec(memory_space=pl.ANY)],
        out_specs=pl.BlockSpec(memory_space=pl.ANY),
        out_shape=jax.ShapeDtypeStruct(x_buf.shape, F32),
        scratch_shapes=[pltpu.SemaphoreType.DMA(())],
        input_output_aliases={2: 0},
        compiler_params=_cparams("arbitrary"), name="moe_dispatch",
    )(rows, h, x_buf)


def _moe_kernel(blk_exp_ref, x_ref, wg_ref, wu_ref, wd_ref, y_ref):
    del blk_exp_ref
    xb = x_ref[...].astype(BF16)
    gate = _dot(xb, wg_ref[0])
    hid = gate * _sigmoid(gate) * _dot(xb, wu_ref[0])
    y_ref[...] = _dot(hid.astype(BF16), wd_ref[0])


def _experts(h, route, x_buf, expert_ws, layer, bm, tp, tm):
    dest, blk_exp, nb = _dispatch_plan(route, bm, tp)
    if x_buf is None:
        x_buf = jnp.zeros((nb * bm, D_MODEL), F32)
    x_buf = _scatter_rows(h, dest, x_buf, tm)
    wspec = lambda shape: pl.BlockSpec((None, 1) + shape, lambda i, be: (layer, be[i], 0, 0))
    grid_spec = pltpu.PrefetchScalarGridSpec(
        num_scalar_prefetch=1, grid=(nb,),
        in_specs=[pl.BlockSpec((bm, D_MODEL), lambda i, be: (i, 0)),
                  wspec((D_MODEL, EXPERT_HIDDEN)), wspec((D_MODEL, EXPERT_HIDDEN)),
                  wspec((EXPERT_HIDDEN, D_MODEL))],
        out_specs=pl.BlockSpec((bm, D_MODEL), lambda i, be: (i, 0)))
    y_blocks = pl.pallas_call(
        _moe_kernel, grid_spec=grid_spec,
        out_shape=jax.ShapeDtypeStruct((nb * bm, D_MODEL), F32),
        compiler_params=_cparams("parallel"), name="moe_experts",
    )(blk_exp, x_buf, *expert_ws)
    return y_blocks, dest.reshape(-1), x_buf


def _combine_kernel(first_ref, next_ref, x_ref, route_ref, g_ref, y_hbm, o_ref, ybuf, sem, *, tm, final_norm):
    i = pl.program_id(0)
    cur = lax.rem(i, 2)
    nxt = 1 - cur
    n = MOE_TOPK * tm

    def row_of(ref):
        return lambda r: ref[0, 0, r]

    @pl.when(i == 0)
    def _():
        _start_row_gather(y_hbm, row_of(first_ref), ybuf.at[0], sem.at[0], n)

    _start_row_gather(y_hbm, row_of(next_ref), ybuf.at[nxt], sem.at[nxt], n)
    _wait_row_gather(y_hbm, ybuf.at[cur], sem.at[cur], n)
    route = route_ref[...]
    x = x_ref[...] + route[:, 0:1] * ybuf[cur, 0:tm, :] + route[:, 1:2] * ybuf[cur, tm:n, :]
    o_ref[...] = _rms(x, g_ref[...]) if final_norm else x

    @pl.when(i == pl.num_programs(0) - 1)
    def _():
        _wait_row_gather(y_hbm, ybuf.at[nxt], sem.at[nxt], n)


def _combine(x2, y_blocks, dest, route, g, final_norm, tm):
    t = x2.shape[0]
    steps = t // tm
    rows = jnp.transpose(dest.reshape(steps, tm, MOE_TOPK), (0, 2, 1)).reshape(steps, 1, MOE_TOPK * tm)
    g2 = g.reshape(1, D_MODEL)
    row = lambda width: pl.BlockSpec((tm, width), lambda i: (i, 0))
    smem_rows = lambda index_map: pl.BlockSpec((1, 1, MOE_TOPK * tm), index_map, memory_space=pltpu.SMEM)
    return pl.pallas_call(
        functools.partial(_combine_kernel, tm=tm, final_norm=final_norm), grid=(steps,),
        in_specs=[smem_rows(lambda i: (0, 0, 0)), smem_rows(lambda i: (jnp.minimum(i + 1, steps - 1), 0, 0)),
                  row(D_MODEL), row(LANES), pl.BlockSpec((1, D_MODEL), lambda i: (0, 0)),
                  pl.BlockSpec(memory_space=pl.ANY)],
        out_specs=row(D_MODEL), out_shape=jax.ShapeDtypeStruct((t, D_MODEL), F32),
        scratch_shapes=[pltpu.VMEM((2, MOE_TOPK * tm, D_MODEL), F32), pltpu.SemaphoreType.DMA((2,))],
        compiler_params=_cparams("arbitrary"), name="moe_combine",
    )(rows, rows, x2, route, g2, y_blocks)


def _tiles(batch, seq):
    t = batch * seq
    return dict(
        tm_proj=min(256, t), tq=min(256, seq), tk=min(256, seq), tm_sgu=min(512, t),
        s5_steps=min(128, seq), tm_merge=min(256, t), bm=256, tp_plan=min(512, t), tm_scat=min(256, t),
        tm_comb=min(256, t))


def kernel(x, positions, norm_mix_g, w_in, cmp_pos_k, cmp_pos_v, cmp_k_w1, cmp_k_w2, cmp_v_w1, cmp_v_w2, w_attn_o, sgu_norm_g, sgu_w, sgu_b, w_sgu_o, s5_a_re, s5_a_im, s5_log_step, s5_b_re, s5_b_im, s5_c_re, s5_c_im, s5_d, s5_glu_w, s5_glu_b, w_s5_o, w_mix_o, norm_ffn_g, router_group_w, router_group_b, router_expert_w, router_expert_b, expert_w_gate, expert_w_up, expert_w_down, norm_final_g):
    batch, seq, _ = x.shape
    depth = w_in.shape[0]
    cfg = _tiles(batch, seq)
    tables = _rope_tables(positions)
    x2 = x.reshape(batch * seq, D_MODEL)
    x_buf = None
    w_in_all = _prep_in_proj(w_in)
    merge_ws = tuple(w.astype(BF16) for w in (w_attn_o, w_sgu_o, w_s5_o, w_mix_o))
    expert_ws = tuple(w.astype(BF16) for w in (expert_w_gate, expert_w_up, expert_w_down))
    for l in range(depth):
        qt, qrt, kvc, ksel, kwin, vt, gt, sgu_in, s5_in, bg = _in_proj(
            x2, norm_mix_g[l].reshape(1, D_MODEL), w_in_all, l, tables, seq, cfg["tm_proj"])
        kc, vct = _compress(kvc, _prep_compress(cmp_pos_k[l], cmp_pos_v[l], cmp_k_w1[l], cmp_k_w2[l],
                                                cmp_v_w1[l], cmp_v_w2[l]), batch, seq)
        attn = _attention(qt, qrt, gt, kc, vct, ksel, kwin, vt, batch, seq, cfg["tq"], cfg["tk"])
        sgu = _sgu(sgu_in, sgu_norm_g[l], sgu_w[l], sgu_b[l], cfg["tm_sgu"])
        s5 = _s5(s5_in, _prep_s5(s5_a_re[l], s5_a_im[l], s5_log_step[l], s5_b_re[l], s5_b_im[l],
                                 s5_c_re[l], s5_c_im[l]),
                 s5_d[l], s5_glu_w[l], s5_glu_b[l], batch, seq, cfg["s5_steps"])
        x_mid, h, route = _merge(x2, attn, sgu, s5, bg, merge_ws, l,
                                 norm_ffn_g[l], router_group_w[l], router_group_b[l],
                                 router_expert_w[l], router_expert_b[l], cfg["tm_merge"])
        y_blocks, dest, x_buf = _experts(h, route, x_buf, expert_ws, l, cfg["bm"], cfg["tp_plan"], cfg["tm_scat"])
        x2 = _combine(x_mid, y_blocks, dest, route, norm_final_g, l == depth - 1, cfg["tm_comb"])
    return x2.reshape(batch, seq, D_MODEL)
```

```python
import functools
import math

import numpy as np
import jax
import jax.numpy as jnp
from jax import lax
from jax.experimental import pallas as pl
from jax.experimental.pallas import tpu as pltpu

F32 = jnp.float32
BF16 = jnp.bfloat16

D_MODEL = 1024
HEAD_DIM = 64
NSA_HEADS = 8
NSA_KV_GROUPS = 2
HEADS_PER_GROUP = NSA_HEADS // NSA_KV_GROUPS
GROUP_WIDTH = HEADS_PER_GROUP * HEAD_DIM
NSA_WIDTH = NSA_HEADS * HEAD_DIM
KV_WIDTH = NSA_KV_GROUPS * HEAD_DIM
ROPE_DIM = HEAD_DIM // 4
ROPE_HALF = ROPE_DIM // 2
ROPE_THETA = 500000.0
CMP_LEN = 32
CMP_STRIDE = 16
CMP_HIDDEN = 128
SLC_LEN = 64
SLC_SHIFT = int(math.log2(SLC_LEN))
SLC_TOP = 16
WINDOW = 512
FORCE_SCORE = 1.0e4
NEG_INF = -1.0e30
SGU_WIDTH = 256
SGU_GROUPS = 4
SGU_CHUNK = 128
S5_WIDTH = 256
S5_GROUP_CH = 16
S5_GROUPS = S5_WIDTH // S5_GROUP_CH
S5_STATE = 64
S5_LANES = S5_GROUPS * S5_STATE
MOE_GROUPS = 4
MOE_EXPERTS_PER_GROUP = 8
MOE_EXPERTS = MOE_GROUPS * MOE_EXPERTS_PER_GROUP
MOE_TOPK = 2
EXPERT_HIDDEN = 512
ROUTER_ROWS = 40
RMS_EPS = 1e-6
ATTN_SCALE = HEAD_DIM ** -0.5
LOG2_E = math.log2(math.e)

LANES = 128
SUBLANES = 8
VMEM_LIMIT_BYTES = 56 * 1024 * 1024

BF16_SUBLANES = 16
V_ROWS = HEAD_DIM + BF16_SUBLANES
GATE_COLS = HEADS_PER_GROUP * 3
GATE_ROWS = 16

_OFF_Q = 0
_OFF_KVC = _OFF_Q + NSA_WIDTH
_OFF_KS = _OFF_KVC + 2 * KV_WIDTH
_OFF_VS = _OFF_KS + KV_WIDTH
_OFF_KW = _OFF_VS + KV_WIDTH
_OFF_VW = _OFF_KW + KV_WIDTH
_OFF_GATE = _OFF_VW + KV_WIDTH
_OFF_SGU = _OFF_GATE + LANES
_OFF_S5 = _OFF_SGU + 2 * SGU_WIDTH
_OFF_BG = _OFF_S5 + S5_WIDTH
_IN_W = _OFF_BG + 3 * D_MODEL


def _cparams(*sem):
    return pltpu.CompilerParams(dimension_semantics=sem, vmem_limit_bytes=VMEM_LIMIT_BYTES)


def _gelu(x):
    return 0.5 * x * (1.0 + jnp.tanh(math.sqrt(2.0 / math.pi) * (x + 0.044715 * (x * x * x))))


def _sigmoid(x):
    return 1.0 / (1.0 + jnp.exp(-x))


def _dot(a, b):
    return jnp.dot(a, b, preferred_element_type=F32)


def _dot_nt(a, b):
    return lax.dot_general(a, b, (((1,), (1,)), ((), ())), preferred_element_type=F32)


def _split_bf16(x):
    hi = x.astype(BF16)
    lo = (x - hi.astype(F32)).astype(BF16)
    return hi, lo


def _rms(x, g):
    return x * lax.rsqrt(jnp.mean(x * x, axis=-1, keepdims=True) + RMS_EPS) * g


def _rope(x, c, s_lo, s_hi):
    n = x.shape[-1]
    return x * c + pltpu.roll(x, n - ROPE_HALF, 1) * s_lo + pltpu.roll(x, ROPE_HALF, 1) * s_hi


def _in_proj_kernel(x_ref, g_ref, w_ref, c_ref, slo_ref, shi_ref,
                    qt_ref, qrt_ref, kvc_ref, ksel_ref, kwin_ref, vt_ref, gt_ref, sgu_ref, s5_ref, bg_ref, *, seq):
    tm = x_ref.shape[0]
    x = x_ref[...]
    hb = _rms(x, g_ref[...]).astype(BF16)

    def proj(off, width):
        return _dot(hb, w_ref[:, off:off + width])

    c, s_lo, s_hi = c_ref[...], slo_ref[...], shi_ref[...]
    rep = NSA_WIDTH // LANES
    q = proj(_OFF_Q, NSA_WIDTH)
    qt_ref[...] = (q * ATTN_SCALE).T
    qrot = _rope(q, jnp.concatenate([c] * rep, axis=1), jnp.concatenate([s_lo] * rep, axis=1),
                 jnp.concatenate([s_hi] * rep, axis=1))
    qrt_ref[...] = (qrot * (ATTN_SCALE * LOG2_E)).T.astype(BF16)
    kvc_ref[...] = proj(_OFF_KVC, 2 * KV_WIDTH)
    ks = _rope(proj(_OFF_KS, KV_WIDTH), c, s_lo, s_hi)
    kw = _rope(proj(_OFF_KW, KV_WIDTH), c, s_lo, s_hi)
    pos = lax.rem(pl.program_id(0) * tm, seq) + lax.broadcasted_iota(jnp.int32, (tm, 1), 0)
    blk = lax.shift_right_logical(pos, SLC_SHIFT)
    blk_onehot = jnp.where(lax.broadcasted_iota(jnp.int32, (1, HEAD_DIM), 1) == blk, 1.0, 0.0)
    for g in range(NSA_KV_GROUPS):
        cols = slice(g * HEAD_DIM, (g + 1) * HEAD_DIM)
        ksel_ref[g] = jnp.concatenate([ks[:, cols], blk_onehot], axis=1).astype(BF16)
        kwin_ref[g] = kw[:, cols].astype(BF16)
    aug_rows = lax.broadcasted_iota(jnp.int32, (V_ROWS - HEAD_DIM, LANES), 0)
    ones_row = jnp.where(aug_rows == 0, 1.0, 0.0).astype(BF16)
    for p, off in enumerate((_OFF_VS, _OFF_VW)):
        vt = proj(off, KV_WIDTH).T.astype(BF16)
        for g in range(NSA_KV_GROUPS):
            for ch in range(tm // LANES):
                idx = p * NSA_KV_GROUPS + g
                vt_ref[idx, ch, 0:HEAD_DIM, :] = vt[g * HEAD_DIM:(g + 1) * HEAD_DIM, ch * LANES:(ch + 1) * LANES]
                vt_ref[idx, ch, HEAD_DIM:V_ROWS, :] = ones_row
    gt_ref[...] = _sigmoid(proj(_OFF_GATE, LANES)).T[:NSA_KV_GROUPS * GATE_ROWS]
    sgu_ref[...] = proj(_OFF_SGU, 2 * SGU_WIDTH)
    s5_ref[...] = proj(_OFF_S5, S5_WIDTH)
    for k in range(3):
        bg_ref[:, k * D_MODEL:(k + 1) * D_MODEL] = _sigmoid(
            proj(_OFF_BG + k * D_MODEL, D_MODEL)).astype(BF16)


def _prep_in_proj(w_in):
    sizes = (NSA_WIDTH,) + (KV_WIDTH,) * 6 + (3 * NSA_HEADS, 2 * SGU_WIDTH, S5_WIDTH, 3 * D_MODEL)
    offs = np.concatenate([[0], np.cumsum(sizes)])
    q, kc, vc, ks, vs, kw, vw, ng, sgu, s5, bg = [w_in[..., offs[i]:offs[i + 1]] for i in range(11)]
    pad = lambda n: jnp.zeros(w_in.shape[:-1] + (n,), w_in.dtype)
    gates = []
    for g in range(NSA_KV_GROUPS):
        gates += [ng[..., g * GATE_COLS:(g + 1) * GATE_COLS], pad(GATE_ROWS - GATE_COLS)]
    gates.append(pad(LANES - NSA_KV_GROUPS * GATE_ROWS))
    return jnp.concatenate([q, kc, vc, ks, vs, kw, vw] + gates + [sgu, s5, bg], axis=-1).astype(BF16)


def _layer_spec(a, layer):
    return pl.BlockSpec((None,) + a.shape[1:], lambda *_: (layer,) + (0,) * (a.ndim - 1))


def _rope_tables(positions):
    inv_freq = ROPE_THETA ** (-jnp.arange(ROPE_HALF, dtype=F32) / ROPE_HALF)
    ang = positions.astype(F32).reshape(-1, 1) * inv_freq
    cos, sin = jnp.cos(ang), jnp.sin(ang)
    t = ang.shape[0]
    rest = HEAD_DIM - ROPE_DIM
    c = jnp.concatenate([cos, cos, jnp.ones((t, rest), F32)], axis=1)
    s_lo = jnp.concatenate([-sin, jnp.zeros((t, HEAD_DIM - ROPE_HALF), F32)], axis=1)
    s_hi = jnp.concatenate([jnp.zeros((t, ROPE_HALF), F32), sin, jnp.zeros((t, rest), F32)], axis=1)
    rep = LANES // HEAD_DIM
    return tuple(jnp.concatenate([a] * rep, axis=1) for a in (c, s_lo, s_hi))


def _in_proj(x2, g, w_all, layer, tables, seq, tm):
    t = x2.shape[0]
    row = lambda width: pl.BlockSpec((tm, width), lambda i: (i, 0))
    col = lambda height: pl.BlockSpec((height, tm), lambda i: (0, i))
    full = lambda a: pl.BlockSpec(a.shape, lambda i: (0,) * a.ndim)
    n_kv = 2 * NSA_KV_GROUPS
    out_shape = (
        jax.ShapeDtypeStruct((NSA_WIDTH, t), F32),
        jax.ShapeDtypeStruct((NSA_WIDTH, t), BF16),
        jax.ShapeDtypeStruct((t, 2 * KV_WIDTH), F32),
        jax.ShapeDtypeStruct((NSA_KV_GROUPS, t, 2 * HEAD_DIM), BF16),
        jax.ShapeDtypeStruct((NSA_KV_GROUPS, t, HEAD_DIM), BF16),
        jax.ShapeDtypeStruct((n_kv, t // LANES, V_ROWS, LANES), BF16),
        jax.ShapeDtypeStruct((NSA_KV_GROUPS * GATE_ROWS, t), F32),
        jax.ShapeDtypeStruct((t, 2 * SGU_WIDTH), F32),
        jax.ShapeDtypeStruct((seq, (t // seq) * S5_WIDTH), F32),
        jax.ShapeDtypeStruct((t, 3 * D_MODEL), BF16),
    )
    seq_tiles = seq // tm
    time_major = pl.BlockSpec((tm, S5_WIDTH), lambda i: (i % seq_tiles, i // seq_tiles))
    out_specs = (col(NSA_WIDTH), col(NSA_WIDTH), row(2 * KV_WIDTH),
                 pl.BlockSpec((NSA_KV_GROUPS, tm, 2 * HEAD_DIM), lambda i: (0, i, 0)),
                 pl.BlockSpec((NSA_KV_GROUPS, tm, HEAD_DIM), lambda i: (0, i, 0)),
                 pl.BlockSpec((n_kv, tm // LANES, V_ROWS, LANES), lambda i: (0, i, 0, 0)),
                 col(NSA_KV_GROUPS * GATE_ROWS), row(2 * SGU_WIDTH), time_major, row(3 * D_MODEL))
    return pl.pallas_call(
        functools.partial(_in_proj_kernel, seq=seq), grid=(t // tm,),
        in_specs=[row(D_MODEL), full(g), _layer_spec(w_all, layer), row(LANES), row(LANES), row(LANES)],
        out_specs=out_specs, out_shape=out_shape,
        compiler_params=_cparams("parallel"), name="in_proj",
    )(x2, g, w_all, *tables)


def _compress_kernel(r_ref, wa_ref, wb_ref, pos_ref, w1_ref, w2_ref, w2t_ref, kc_ref, vct_ref):
    rb = r_ref[0].astype(BF16)
    n = rb.shape[0]
    a = _dot(rb, wa_ref[...])
    b = _dot(rb, wb_ref[...])
    b = pltpu.roll(b, n - 1, 0)
    rows = lax.broadcasted_iota(jnp.int32, b.shape, 0)
    b = jnp.where(rows == n - 1, 0.0, b)
    for m in range(2 * NSA_KV_GROUPS):
        typ, g = divmod(m, NSA_KV_GROUPS)
        bias = _dot(pos_ref[typ].astype(BF16), w1_ref[typ])
        hid = a[:, m * CMP_HIDDEN:(m + 1) * CMP_HIDDEN] + b[:, m * CMP_HIDDEN:(m + 1) * CMP_HIDDEN] + bias
        act = _gelu(hid).astype(BF16)
        if typ == 0:
            kc_ref[0, g] = _dot(act, w2_ref[...])
        else:
            vct_ref[0, g] = _dot_nt(w2t_ref[...], act)


def _prep_compress(cmp_pos_k, cmp_pos_v, k_w1, k_w2, v_w1, v_w2):
    streams = 2 * NSA_KV_GROUPS
    w1 = jnp.stack([k_w1, v_w1]).reshape(2, CMP_LEN, HEAD_DIM, CMP_HIDDEN)

    def expand(half):
        out = jnp.zeros((CMP_STRIDE, streams, HEAD_DIM, streams, CMP_HIDDEN), F32)
        for m in range(streams):
            out = out.at[:, m, :, m, :].set(w1[m // NSA_KV_GROUPS, half * CMP_STRIDE:(half + 1) * CMP_STRIDE])
        return out.reshape(CMP_STRIDE * streams * HEAD_DIM, streams * CMP_HIDDEN).astype(BF16)

    pos = jnp.stack([cmp_pos_k, cmp_pos_v]).reshape(2, 1, CMP_LEN * HEAD_DIM)
    return (expand(0), expand(1), pos, jnp.stack([k_w1, v_w1]).astype(BF16),
            k_w2.astype(BF16), v_w2.T.astype(BF16))


def _compress(kvc, prep, batch, seq):
    n = seq // CMP_STRIDE
    r = kvc.reshape(batch, n, CMP_STRIDE * 2 * KV_WIDTH)
    full = lambda a: pl.BlockSpec(a.shape, lambda b: (0,) * a.ndim)
    return pl.pallas_call(
        _compress_kernel, grid=(batch,),
        in_specs=[pl.BlockSpec((1, n, r.shape[-1]), lambda b: (b, 0, 0))] + [full(a) for a in prep],
        out_specs=(pl.BlockSpec((1, NSA_KV_GROUPS, n, HEAD_DIM), lambda b: (b, 0, 0, 0)),
                   pl.BlockSpec((1, NSA_KV_GROUPS, HEAD_DIM, n), lambda b: (b, 0, 0, 0))),
        out_shape=(jax.ShapeDtypeStruct((batch, NSA_KV_GROUPS, n, HEAD_DIM), F32),
                   jax.ShapeDtypeStruct((batch, NSA_KV_GROUPS, HEAD_DIM, n), F32)),
        compiler_params=_cparams("parallel"), name="compress_kv",
    )(r, *prep)


def _attn_kernel(qt_ref, qrt_ref, gt_ref, kc_ref, vct_ref, ks_ref, vst_ref, kw_ref, vwt_ref, ovt_ref,
                 o_ref, qa_sc, m_sc, acc_sc, ow_sc, oc_sc, s_sc, p_sc, a_sc, *, tq, tk, seq):
    n_cmp = kc_ref.shape[2]
    n_slc = seq // SLC_LEN
    heads = HEADS_PER_GROUP
    t0 = pl.program_id(2) * tq
    t_lane = t0 + lax.broadcasted_iota(jnp.int32, (1, tq), 1)

    def lane_tile(a):
        return jnp.concatenate([a] * heads, axis=1)

    def head_cols(ref):
        return jnp.concatenate([ref[r * HEAD_DIM:(r + 1) * HEAD_DIM, :] for r in range(heads)], axis=1)

    q_hi, q_lo = _split_bf16(head_cols(qt_ref))
    qr = head_cols(qrt_ref)

    slab = min(WINDOW + tq, seq)
    st = pl.multiple_of(jnp.clip(t0 - WINDOW, 0, seq - slab), LANES)
    s = _dot(kw_ref[0, pl.ds(st, slab), :], qr)
    diff = t_lane - (st + lax.broadcasted_iota(jnp.int32, (slab, 1), 0))
    s = s + lane_tile(jnp.where((diff >= 0) & (diff < WINDOW), 0.0, NEG_INF))
    p = jnp.exp2(s - jnp.max(s, axis=0, keepdims=True))
    st_tile = st // LANES
    vw_t = jnp.concatenate([vwt_ref[0, st_tile + c] for c in range(slab // LANES)], axis=1)
    pv = _dot(vw_t, p.astype(BF16))
    ow_sc[...] = pv[0:HEAD_DIM] / pv[HEAD_DIM:HEAD_DIM + 1]

    def compressed_and_select(nc, ns, all_selected, forced_distinct):
        kc_hi, kc_lo = _split_bf16(kc_ref[0, 0, 0:nc, :])
        s = _dot(kc_hi, q_hi) + _dot(kc_lo, q_hi) + _dot(kc_hi, q_lo)
        cmp_end = lax.broadcasted_iota(jnp.int32, (nc, 1), 0) * CMP_STRIDE + (CMP_LEN - 1)
        m_c = cmp_end <= lane_tile(t_lane)
        s = jnp.where(m_c, s, NEG_INF)
        e = jnp.where(m_c, jnp.exp(s - jnp.max(s, axis=0, keepdims=True)), 0.0)
        p = e / jnp.maximum(jnp.sum(e, axis=0, keepdims=True), 1e-30)
        oc_sc[...] = _dot(vct_ref[0, 0, :, 0:nc].astype(BF16), p.astype(BF16))

        j_idx = lax.broadcasted_iota(jnp.int32, (ns, 1), 0)
        blk_t = lax.shift_right_logical(t_lane, SLC_SHIFT)
        causal = j_idx <= blk_t
        if all_selected:
            picked = causal
        else:
            p_sum = p[:, 0:tq]
            for r in range(1, heads):
                p_sum = p_sum + p[:, r * tq:(r + 1) * tq]
            ps_hi, ps_lo = _split_bf16(p_sum)
            imp = _dot(ovt_ref[0:ns, 0:nc], ps_hi) + _dot(ovt_ref[0:ns, 0:nc], ps_lo)
            forced = (j_idx == 0) | (j_idx == blk_t) | (j_idx == blk_t - 1)
            j_f32 = j_idx.astype(F32)
            if forced_distinct:
                taken = jnp.where(forced, 1.0, 0.0)
                imp = jnp.where(forced, -jnp.inf, jnp.where(causal, imp, NEG_INF))
                rounds = SLC_TOP - 3
            else:
                taken = jnp.zeros((ns, tq), F32)
                imp = jnp.where(causal, jnp.where(forced, FORCE_SCORE, imp), NEG_INF)
                rounds = min(SLC_TOP, ns)
            for _ in range(rounds):
                best = jnp.max(imp, axis=0, keepdims=True)
                first = jnp.min(jnp.where(imp == best, j_f32, float(ns)), axis=0, keepdims=True)
                hit = j_f32 == first
                taken = jnp.where(hit, 1.0, taken)
                imp = jnp.where(hit, -jnp.inf, imp)
            picked = (taken > 0.5) & causal
        bias = jnp.where(picked, 0.0, NEG_INF)
        if ns < HEAD_DIM:
            bias = jnp.concatenate([bias, jnp.zeros((HEAD_DIM - ns, tq), F32)], axis=0)
        qa_sc[...] = jnp.concatenate([qr, lane_tile(bias.astype(BF16))], axis=0)

    nq = seq // tq
    qi = pl.program_id(2)

    def sizes(tiles):
        nc = min(n_cmp, -(-max(tiles * tq // CMP_STRIDE, 1) // LANES) * LANES)
        ns = min(n_slc, -(-max(tiles * tq // SLC_LEN, 1) // SUBLANES) * SUBLANES)
        return nc, ns

    bounds = sorted({0, min(nq, SLC_TOP * SLC_LEN // tq), min(nq, max(nq // 2, 1)), nq})
    for lo, hi in zip(bounds[:-1], bounds[1:]):
        nc, ns = sizes(hi)
        variant = functools.partial(compressed_and_select, nc, ns, hi * tq <= SLC_TOP * SLC_LEN,
                                    lo * tq >= 2 * SLC_LEN)
        pl.when((qi >= lo) & (qi < hi))(variant)

    m_sc[...] = jnp.full(m_sc.shape, NEG_INF, F32)
    acc_sc[...] = jnp.zeros(acc_sc.shape, F32)
    p_sc[1] = jnp.zeros(p_sc.shape[1:], BF16)
    a_sc[1] = jnp.ones(a_sc.shape[1:], F32)
    lanes_per_tile = tk // LANES

    def scores(j, slot):
        k0 = j * tk if isinstance(j, int) else pl.multiple_of(j * tk, tk)
        s_sc[slot] = _dot(ks_ref[0, pl.ds(k0, tk), :], qa_sc[...])

    def softmax(j, slot, diagonal):
        s = s_sc[slot]
        if diagonal:
            key_pos = j * tk + lax.broadcasted_iota(jnp.int32, (tk, 1), 0)
            s = s + lane_tile(jnp.where(key_pos <= t_lane, 0.0, NEG_INF))
        m_old = m_sc[...]
        m_new = jnp.maximum(m_old, jnp.max(s, axis=0, keepdims=True))
        m_sc[...] = m_new
        a_sc[slot] = jnp.exp2(m_old - m_new)
        p_sc[slot] = jnp.exp2(s - m_new).astype(BF16)

    def accumulate(j, slot):
        tile = jnp.maximum(j, 0) * lanes_per_tile
        v_t = jnp.concatenate([vst_ref[0, tile + c] for c in range(lanes_per_tile)], axis=1)
        acc_sc[...] = a_sc[slot] * acc_sc[...] + _dot(v_t, p_sc[slot])

    def pipe_step(i, i_is_odd):
        a, b = (1, 0) if i_is_odd else (0, 1)
        accumulate(i - 2, a)
        softmax(i - 1, b, False)
        scores(i, a)

    n_kv = (t0 + tq + tk - 1) // tk
    scores(0, 0)

    def two_steps(u, carry):
        pipe_step(2 * u + 1, True)
        pipe_step(2 * u + 2, False)
        return carry

    lax.fori_loop(0, (n_kv - 1) // 2, two_steps, 0)

    @pl.when(((n_kv - 1) & 1) == 1)
    def _():
        pipe_step(n_kv - 1, True)

    last = (n_kv - 1) & 1
    accumulate(n_kv - 2, 1 - last)
    softmax(n_kv - 1, last, True)
    accumulate(n_kv - 1, last)
    o_slc = acc_sc[0:HEAD_DIM, :] / acc_sc[HEAD_DIM:HEAD_DIM + 1, :]
    o_win = ow_sc[...]
    o_cmp = oc_sc[...]

    gates = gt_ref[...]
    outs = []
    for r in range(heads):
        cols = slice(r * tq, (r + 1) * tq)
        outs.append(gates[3 * r:3 * r + 1, :] * o_cmp[:, cols] + gates[3 * r + 1:3 * r + 2, :] * o_slc[:, cols]
                    + gates[3 * r + 2:3 * r + 3, :] * o_win[:, cols])
    o_ref[...] = jnp.concatenate(outs, axis=0).T.astype(o_ref.dtype)


def _overlap_matrix_t(n_cmp, n_slc):
    c0 = np.arange(n_cmp)[None, :] * CMP_STRIDE
    s0 = np.arange(n_slc)[:, None] * SLC_LEN
    ov = np.clip(np.minimum(c0 + CMP_LEN, s0 + SLC_LEN) - np.maximum(c0, s0), 0, None) / CMP_LEN
    ov[:, n_cmp - 1] = 0.0
    return jnp.asarray(ov, BF16)


def _attention(qt, qrt, gt, kc, vct, ksel, kwin, vt, batch, seq, tq, tk):
    assert tk % tq == 0 and seq // SLC_LEN <= HEAD_DIM
    n_cmp = seq // CMP_STRIDE
    n_slc = seq // SLC_LEN
    nq = seq // tq
    ksel = ksel.reshape(NSA_KV_GROUPS, batch, seq, 2 * HEAD_DIM)
    kwin = kwin.reshape(NSA_KV_GROUPS, batch, seq, HEAD_DIM)
    vt = vt.reshape(2 * NSA_KV_GROUPS, batch, seq // LANES, V_ROWS, LANES)
    ovt = _overlap_matrix_t(n_cmp, n_slc)
    qspec = pl.BlockSpec((GROUP_WIDTH, tq), lambda b, g, i: (g, b * nq + i))
    k_spec = lambda width: pl.BlockSpec((None, 1, seq, width), lambda b, g, i: (g, b, 0, 0))
    v_spec = lambda p: pl.BlockSpec((None, 1, seq // LANES, V_ROWS, LANES),
                                    lambda b, g, i: (p * NSA_KV_GROUPS + g, b, 0, 0, 0))
    kernel = functools.partial(_attn_kernel, tq=tq, tk=tk, seq=seq)
    return pl.pallas_call(
        kernel, grid=(batch, NSA_KV_GROUPS, nq),
        in_specs=[qspec, qspec, pl.BlockSpec((GATE_ROWS, tq), lambda b, g, i: (g, b * nq + i)),
                  pl.BlockSpec((1, 1, n_cmp, HEAD_DIM), lambda b, g, i: (b, g, 0, 0)),
                  pl.BlockSpec((1, 1, HEAD_DIM, n_cmp), lambda b, g, i: (b, g, 0, 0)),
                  k_spec(2 * HEAD_DIM), v_spec(0), k_spec(HEAD_DIM), v_spec(1),
                  pl.BlockSpec(ovt.shape, lambda b, g, i: (0, 0))],
        out_specs=pl.BlockSpec((tq, GROUP_WIDTH), lambda b, g, i: (b * nq + i, g)),
        out_shape=jax.ShapeDtypeStruct((batch * seq, NSA_WIDTH), BF16),
        scratch_shapes=[pltpu.VMEM((2 * HEAD_DIM, HEADS_PER_GROUP * tq), BF16),
                        pltpu.VMEM((1, HEADS_PER_GROUP * tq), F32),
                        pltpu.VMEM((V_ROWS, HEADS_PER_GROUP * tq), F32),
                        pltpu.VMEM((HEAD_DIM, HEADS_PER_GROUP * tq), F32),
                        pltpu.VMEM((HEAD_DIM, HEADS_PER_GROUP * tq), F32),
                        pltpu.VMEM((2, tk, HEADS_PER_GROUP * tq), F32),
                        pltpu.VMEM((2, tk, HEADS_PER_GROUP * tq), BF16),
                        pltpu.VMEM((2, 1, HEADS_PER_GROUP * tq), F32)],
        compiler_params=_cparams("parallel", "parallel", "arbitrary"), name="nsa_attention",
    )(qt, qrt, gt, kc, vct, ksel, vt, kwin, vt, ovt)


def _sgu_kernel(uv_ref, g_ref, w_ref, b_ref, o_ref, *, chunks):
    z = _gelu(uv_ref[...])
    u = z[:, :SGU_WIDTH]
    v = _rms(z[:, SGU_WIDTH:], g_ref[...])
    gw = SGU_WIDTH // SGU_GROUPS
    rows = lax.broadcasted_iota(jnp.int32, (SGU_CHUNK, SGU_GROUPS * SGU_CHUNK), 0)
    cols = lax.broadcasted_iota(jnp.int32, (SGU_CHUNK, SGU_GROUPS * SGU_CHUNK), 1)
    w = jnp.where((cols & (SGU_CHUNK - 1)) <= rows, w_ref[...], 0.0).astype(BF16)
    grp_r = lax.broadcasted_iota(jnp.int32, (SGU_GROUPS * SGU_CHUNK, SGU_WIDTH), 0) // SGU_CHUNK
    grp_c = lax.broadcasted_iota(jnp.int32, (SGU_GROUPS * SGU_CHUNK, SGU_WIDTH), 1) // gw
    for c in range(chunks):
        vc = v[c * SGU_CHUNK:(c + 1) * SGU_CHUNK].astype(BF16)
        v_bd = jnp.where(grp_r == grp_c, jnp.concatenate([vc] * SGU_GROUPS, axis=0), jnp.zeros((), BF16))
        mixed = _dot(w, v_bd) + b_ref[...]
        o_ref[c * SGU_CHUNK:(c + 1) * SGU_CHUNK, :] = (u[c * SGU_CHUNK:(c + 1) * SGU_CHUNK] * mixed).astype(o_ref.dtype)


def _sgu(sgu_in, norm_g, w_s, b_s, tm):
    t = sgu_in.shape[0]
    w_cat = jnp.transpose(w_s, (1, 0, 2)).reshape(SGU_CHUNK, SGU_GROUPS * SGU_CHUNK)
    bias = jnp.repeat(b_s.T, SGU_WIDTH // SGU_GROUPS, axis=1)
    g = norm_g.reshape(1, SGU_WIDTH)
    full = lambda a: pl.BlockSpec(a.shape, lambda i: (0,) * a.ndim)
    return pl.pallas_call(
        functools.partial(_sgu_kernel, chunks=tm // SGU_CHUNK), grid=(t // tm,),
        in_specs=[pl.BlockSpec((tm, 2 * SGU_WIDTH), lambda i: (i, 0)), full(g), full(w_cat), full(bias)],
        out_specs=pl.BlockSpec((tm, SGU_WIDTH), lambda i: (i, 0)),
        out_shape=jax.ShapeDtypeStruct((t, SGU_WIDTH), BF16),
        compiler_params=_cparams("parallel"), name="sgu",
    )(sgu_in, g, w_cat, bias)


def _s5_kernel(x_ref, bre_ref, bim_ref, are_ref, aim_ref, cre_ref, cim_ref, d_ref, gw_ref, gb_ref,
               o_ref, hre_sc, him_sc, ure_sc, uim_sc, *, steps, batch):
    @pl.when(pl.program_id(0) == 0)
    def _():
        hre_sc[...] = jnp.zeros(hre_sc.shape, F32)
        him_sc[...] = jnp.zeros(him_sc.shape, F32)

    x = x_ref[...]
    xb = x.astype(BF16)
    ure_sc[...] = _dot(xb, bre_ref[...])
    uim_sc[...] = _dot(xb, bim_ref[...])
    a_re = jnp.broadcast_to(are_ref[...], (batch, S5_LANES))
    a_im = jnp.broadcast_to(aim_ref[...], (batch, S5_LANES))

    def step(t, carry):
        h_re, h_im = carry
        rows = pl.ds(pl.multiple_of(t * batch, batch), batch)
        n_re = a_re * h_re - a_im * h_im + ure_sc[rows, :]
        n_im = a_re * h_im + a_im * h_re + uim_sc[rows, :]
        ure_sc[rows, :] = n_re
        uim_sc[rows, :] = n_im
        return n_re, n_im

    h_re, h_im = lax.fori_loop(0, steps, step, (hre_sc[...], him_sc[...]))
    hre_sc[...] = h_re
    him_sc[...] = h_im
    y = _dot(ure_sc[...].astype(BF16), cre_ref[...]) - _dot(uim_sc[...].astype(BF16), cim_ref[...])
    y = _gelu(y + d_ref[...] * x)
    y = y * _sigmoid(_dot(y.astype(BF16), gw_ref[...]) + gb_ref[...])
    o_ref[...] = y.astype(o_ref.dtype)


def _block_diag(blocks):
    g, r, c = blocks.shape
    eye = jnp.eye(g, dtype=blocks.dtype)
    return (blocks[:, :, None, :] * eye[:, None, :, None]).reshape(g * r, g * c)


def _prep_s5(a_re, a_im, log_step, b_re, b_im, c_re, c_im):
    step = jnp.exp(log_step)[:, None]
    mag = jnp.exp(a_re * step)
    abar_re, abar_im = mag * jnp.cos(a_im * step), mag * jnp.sin(a_im * step)
    den = a_re * a_re + a_im * a_im
    nr, ni = abar_re - 1.0, abar_im
    coef_re = (nr * a_re + ni * a_im) / den
    coef_im = (ni * a_re - nr * a_im) / den
    bbar_re = coef_re[..., None] * b_re - coef_im[..., None] * b_im
    bbar_im = coef_re[..., None] * b_im + coef_im[..., None] * b_re
    to_in = lambda b: _block_diag(jnp.transpose(b, (0, 2, 1))).astype(BF16)
    to_out = lambda c: _block_diag(jnp.transpose(c, (0, 2, 1))).astype(BF16)
    return (to_in(bbar_re), to_in(bbar_im), abar_re.reshape(1, S5_LANES), abar_im.reshape(1, S5_LANES),
            to_out(c_re), to_out(c_im))


def _s5(xs, prep, d, glu_w, glu_b, batch, seq, steps):
    bre, bim, are, aim, cre, cim = prep
    x_tm = xs.reshape(seq * batch, S5_WIDTH)
    rows = steps * batch
    d2, gb2, gwb = d.reshape(1, S5_WIDTH), glu_b.reshape(1, S5_WIDTH), glu_w.astype(BF16)
    full = lambda a: pl.BlockSpec(a.shape, lambda i: (0,) * a.ndim)
    y = pl.pallas_call(
        functools.partial(_s5_kernel, steps=steps, batch=batch), grid=(seq // steps,),
        in_specs=[pl.BlockSpec((rows, S5_WIDTH), lambda i: (i, 0)), full(bre), full(bim), full(are), full(aim),
                  full(cre), full(cim), full(d2), full(gwb), full(gb2)],
        out_specs=pl.BlockSpec((rows, S5_WIDTH), lambda i: (i, 0)),
        out_shape=jax.ShapeDtypeStruct((seq * batch, S5_WIDTH), BF16),
        scratch_shapes=[pltpu.VMEM((batch, S5_LANES), F32), pltpu.VMEM((batch, S5_LANES), F32),
                        pltpu.VMEM((rows, S5_LANES), F32), pltpu.VMEM((rows, S5_LANES), F32)],
        compiler_params=_cparams("arbitrary"), name="s5_scan",
    )(x_tm, bre, bim, are, aim, cre, cim, d2, gwb, gb2)
    return y.reshape(seq, batch * S5_WIDTH)


def _merge_kernel(x_ref, a_ref, b_ref, c_ref, bg_ref, wa_ref, wb_ref, wc_ref, wm_ref, g_ref,
                  wrh_ref, wrl_ref, br_ref, xo_ref, h_ref, route_ref):
    y_a = _dot(a_ref[...], wa_ref[...])
    y_b = _dot(b_ref[...], wb_ref[...])
    y_c = _dot(c_ref[...], wc_ref[...])
    merged = (bg_ref[:, :D_MODEL].astype(F32) * y_a + bg_ref[:, D_MODEL:2 * D_MODEL].astype(F32) * y_b
              + bg_ref[:, 2 * D_MODEL:].astype(F32) * y_c)
    x = x_ref[...] + _dot(merged.astype(BF16), wm_ref[...])
    xo_ref[...] = x
    h = _rms(x, g_ref[...])
    h_ref[...] = h

    h_hi, h_lo = _split_bf16(h)
    logits = (_dot_nt(wrh_ref[...], h_hi) + _dot_nt(wrl_ref[...], h_hi) + _dot_nt(wrh_ref[...], h_lo))
    logits = logits[0:ROUTER_ROWS] + br_ref[...]
    row = lax.broadcasted_iota(jnp.int32, (ROUTER_ROWS, 1), 0).astype(F32)
    big = float(ROUTER_ROWS)
    is_grp = row < MOE_GROUPS
    gl = jnp.where(is_grp, logits, -jnp.inf)
    gmax = jnp.max(gl, axis=0, keepdims=True)
    grp = jnp.min(jnp.where(gl == gmax, row, big), axis=0, keepdims=True)
    grp_w = 1.0 / jnp.sum(jnp.where(is_grp, jnp.exp(logits - gmax), 0.0), axis=0, keepdims=True)
    e_row = row - MOE_GROUPS
    in_grp = (e_row >= 0) & (jnp.floor(e_row * (1.0 / MOE_EXPERTS_PER_GROUP)) == grp)
    emax = jnp.max(jnp.where(in_grp, logits, -jnp.inf), axis=0, keepdims=True)
    ee = jnp.where(in_grp, jnp.exp(logits - emax), 0.0)
    prob = jnp.where(in_grp, ee / jnp.sum(ee, axis=0, keepdims=True), -1.0)
    p1 = jnp.max(prob, axis=0, keepdims=True)
    j1 = jnp.min(jnp.where(prob == p1, row, big), axis=0, keepdims=True)
    prob2 = jnp.where(row == j1, -1.0, prob)
    p2 = jnp.max(prob2, axis=0, keepdims=True)
    j2 = jnp.min(jnp.where(prob2 == p2, row, big), axis=0, keepdims=True)
    psum = p1 + p2
    out_row = lax.broadcasted_iota(jnp.int32, (LANES, 1), 0)
    route_t = jnp.where(out_row == 0, grp_w * p1 / psum, 0.0)
    route_t = jnp.where(out_row == 1, grp_w * p2 / psum, route_t)
    route_t = jnp.where(out_row == 2, j1 - MOE_GROUPS, route_t)
    route_t = jnp.where(out_row == 3, j2 - MOE_GROUPS, route_t)
    route_ref[...] = route_t.T


def _merge(x2, attn, sgu, s5, bg, ws, layer, norm_g, rgw, rgb, rew, reb, tm):
    t = x2.shape[0]
    seq_tiles = s5.shape[0] // tm
    wr = jnp.concatenate([rgw, rew, jnp.zeros((D_MODEL, LANES - MOE_GROUPS - MOE_EXPERTS), F32)], axis=1).T
    wr_hi = wr.astype(BF16)
    wr_lo = (wr - wr_hi.astype(F32)).astype(BF16)
    br = jnp.concatenate([rgb, reb, jnp.zeros((ROUTER_ROWS - MOE_GROUPS - MOE_EXPERTS,), F32)]).reshape(ROUTER_ROWS, 1)
    g = norm_g.reshape(1, D_MODEL)
    row = lambda width: pl.BlockSpec((tm, width), lambda i: (i, 0))
    full = lambda a: pl.BlockSpec(a.shape, lambda i: (0,) * a.ndim)
    return pl.pallas_call(
        _merge_kernel, grid=(t // tm,),
        in_specs=[row(D_MODEL), row(NSA_WIDTH), row(SGU_WIDTH),
                  pl.BlockSpec((tm, S5_WIDTH), lambda i: (i % seq_tiles, i // seq_tiles)),
                  row(3 * D_MODEL)]
                 + [_layer_spec(w, layer) for w in ws] + [full(g), full(wr_hi), full(wr_lo), full(br)],
        out_specs=(row(D_MODEL), row(D_MODEL), row(LANES)),
        out_shape=(jax.ShapeDtypeStruct((t, D_MODEL), F32), jax.ShapeDtypeStruct((t, D_MODEL), F32),
                   jax.ShapeDtypeStruct((t, LANES), F32)),
        compiler_params=_cparams("parallel"), name="merge_route",
    )(x2, attn, sgu, s5, bg, *ws, g, wr_hi, wr_lo, br)


def _start_row_gather(src_hbm, rows_of, dst_buf, sem, n):
    for r in range(n):
        pltpu.make_async_copy(src_hbm.at[pl.ds(rows_of(r), 1)], dst_buf.at[pl.ds(r, 1)], sem).start(priority=r % 2)


def _wait_row_gather(src_hbm, dst_buf, sem, n):
    pltpu.make_async_copy(src_hbm.at[pl.ds(0, n)], dst_buf, sem).wait()


def _plan_kernel(route_ref, dest_ref, cnt_ref, carry_sc, *, bm):
    phase, i = pl.program_id(0), pl.program_id(1)
    tp = route_ref.shape[0]
    lane = lax.broadcasted_iota(jnp.int32, (1, LANES), 1)
    route = route_ref[...]
    oh0 = jnp.where(lane == route[:, 2:3].astype(jnp.int32), 1.0, 0.0)
    oh1 = jnp.where(lane == route[:, 3:4].astype(jnp.int32), 1.0, 0.0)
    both = oh0 + oh1
    col_sum = jnp.broadcast_to(jnp.sum(both, axis=0, keepdims=True), carry_sc.shape)

    @pl.when((phase == 0) & (i == 0))
    def _():
        carry_sc[...] = jnp.zeros(carry_sc.shape, F32)

    @pl.when(phase == 0)
    def _():
        carry_sc[...] = carry_sc[...] + col_sum

    @pl.when((phase == 1) & (i == 0))
    def _():
        counts = carry_sc[...]
        cnt_ref[...] = counts
        padded = jnp.floor((counts + (bm - 1)) * (1.0 / bm)) * bm
        incl = padded
        shift = 1
        while shift < LANES:
            incl = incl + jnp.where(lane >= shift, pltpu.roll(incl, shift, 1), 0.0)
            shift *= 2
        carry_sc[...] = incl - padded

    @pl.when(phase == 1)
    def _():
        r_idx = lax.broadcasted_iota(jnp.int32, (tp, tp), 0)
        c_idx = lax.broadcasted_iota(jnp.int32, (tp, tp), 1)
        earlier = jnp.where(c_idx < r_idx, 1.0, 0.0).astype(BF16)
        base = _dot(earlier, both.astype(BF16)) + carry_sc[0:1, :]
        d0 = jnp.sum(oh0 * base, axis=1, keepdims=True)
        d1 = jnp.sum(oh1 * base, axis=1, keepdims=True)
        dest_ref[...] = jnp.where(lane == 0, d0, jnp.where(lane == 1, d1, 0.0)).astype(jnp.int32)
        carry_sc[...] = carry_sc[...] + col_sum


def _dispatch_plan(route, bm, tp):
    t = route.shape[0]
    steps = t // tp
    dest, counts = pl.pallas_call(
        functools.partial(_plan_kernel, bm=bm), grid=(2, steps),
        in_specs=[pl.BlockSpec((tp, LANES), lambda p, i: (i, 0))],
        out_specs=(pl.BlockSpec((tp, LANES), lambda p, i: (i * p, 0)),
                   pl.BlockSpec((SUBLANES, LANES), lambda p, i: (0, 0))),
        out_shape=(jax.ShapeDtypeStruct((t, LANES), jnp.int32), jax.ShapeDtypeStruct((SUBLANES, LANES), F32)),
        scratch_shapes=[pltpu.VMEM((SUBLANES, LANES), F32)],
        compiler_params=_cparams("arbitrary", "arbitrary"), name="moe_plan",
    )(route)
    nb = t * MOE_TOPK // bm + MOE_EXPERTS
    nblk = (counts[0, :MOE_EXPERTS].astype(jnp.int32) + (bm - 1)) // bm
    blk_end = jnp.cumsum(nblk)
    blk_exp = jnp.sum((blk_end[None, :] <= jnp.arange(nb, dtype=jnp.int32)[:, None]).astype(jnp.int32), axis=1)
    return dest[:, :MOE_TOPK], jnp.minimum(blk_exp, MOE_EXPERTS - 1), nb


def _scatter_kernel(rows_ref, h_ref, zero_hbm, x_hbm, sem, *, tm):
    del zero_hbm
    n = MOE_TOPK * tm
    for r in range(n):
        pltpu.make_async_copy(h_ref.at[pl.ds(r // MOE_TOPK, 1)], x_hbm.at[pl.ds(rows_ref[0, 0, r], 1)],
                              sem).start(priority=r % 2)
    for _ in range(MOE_TOPK):
        pltpu.make_async_copy(h_ref, x_hbm.at[pl.ds(0, tm)], sem).wait()


def _scatter_rows(h, dest, x_buf, tm):
    t = h.shape[0]
    steps = t // tm
    rows = dest.reshape(steps, 1, MOE_TOPK * tm)
    return pl.pallas_call(
        functools.partial(_scatter_kernel, tm=tm), grid=(steps,),
        in_specs=[pl.BlockSpec((1, 1, MOE_TOPK * tm), lambda i: (i, 0, 0), memory_space=pltpu.SMEM),
                  pl.BlockSpec((tm, D_MODEL), lambda i: (i, 0)),
                  pl.BlockSpec(memory_space=pl.ANY)],
        out_specs=pl.BlockSpec(memory_space=pl.ANY),
        out_shape=jax.ShapeDtypeStruct(x_buf.shape, F32),
        scratch_shapes=[pltpu.SemaphoreType.DMA(())],
        input_output_aliases={2: 0},
        compiler_params=_cparams("arbitrary"), name="moe_dispatch",
    )(rows, h, x_buf)


def _moe_kernel(blk_exp_ref, x_ref, wg_ref, wu_ref, wd_ref, y_ref):
    del blk_exp_ref
    xb = x_ref[...].astype(BF16)
    gate = _dot(xb, wg_ref[0])
    hid = gate * _sigmoid(gate) * _dot(xb, wu_ref[0])
    y_ref[...] = _dot(hid.astype(BF16), wd_ref[0])


def _experts(h, route, x_buf, expert_ws, layer, bm, tp, tm):
    dest, blk_exp, nb = _dispatch_plan(route, bm, tp)
    if x_buf is None:
        x_buf = jnp.zeros((nb * bm, D_MODEL), F32)
    x_buf = _scatter_rows(h, dest, x_buf, tm)
    wspec = lambda shape: pl.BlockSpec((None, 1) + shape, lambda i, be: (layer, be[i], 0, 0))
    grid_spec = pltpu.PrefetchScalarGridSpec(
        num_scalar_prefetch=1, grid=(nb,),
        in_specs=[pl.BlockSpec((bm, D_MODEL), lambda i, be: (i, 0)),
                  wspec((D_MODEL, EXPERT_HIDDEN)), wspec((D_MODEL, EXPERT_HIDDEN)),
                  wspec((EXPERT_HIDDEN, D_MODEL))],
        out_specs=pl.BlockSpec((bm, D_MODEL), lambda i, be: (i, 0)))
    y_blocks = pl.pallas_call(
        _moe_kernel, grid_spec=grid_spec,
        out_shape=jax.ShapeDtypeStruct((nb * bm, D_MODEL), F32),
        compiler_params=_cparams("parallel"), name="moe_experts",
    )(blk_exp, x_buf, *expert_ws)
    return y_blocks, dest.reshape(-1), x_buf


def _combine_kernel(first_ref, next_ref, x_ref, route_ref, g_ref, y_hbm, o_ref, ybuf, sem, *, tm, final_norm):
    i = pl.program_id(0)
    cur = lax.rem(i, 2)
    nxt = 1 - cur
    n = MOE_TOPK * tm

    def row_of(ref):
        return lambda r: ref[0, 0, r]

    @pl.when(i == 0)
    def _():
        _start_row_gather(y_hbm, row_of(first_ref), ybuf.at[0], sem.at[0], n)

    _start_row_gather(y_hbm, row_of(next_ref), ybuf.at[nxt], sem.at[nxt], n)
    _wait_row_gather(y_hbm, ybuf.at[cur], sem.at[cur], n)
    route = route_ref[...]
    x = x_ref[...] + route[:, 0:1] * ybuf[cur, 0:tm, :] + route[:, 1:2] * ybuf[cur, tm:n, :]
    o_ref[...] = _rms(x, g_ref[...]) if final_norm else x

    @pl.when(i == pl.num_programs(0) - 1)
    def _():
        _wait_row_gather(y_hbm, ybuf.at[nxt], sem.at[nxt], n)


def _combine(x2, y_blocks, dest, route, g, final_norm, tm):
    t = x2.shape[0]
    steps = t // tm
    rows = jnp.transpose(dest.reshape(steps, tm, MOE_TOPK), (0, 2, 1)).reshape(steps, 1, MOE_TOPK * tm)
    g2 = g.reshape(1, D_MODEL)
    row = lambda width: pl.BlockSpec((tm, width), lambda i: (i, 0))
    smem_rows = lambda index_map: pl.BlockSpec((1, 1, MOE_TOPK * tm), index_map, memory_space=pltpu.SMEM)
    return pl.pallas_call(
        functools.partial(_combine_kernel, tm=tm, final_norm=final_norm), grid=(steps,),
        in_specs=[smem_rows(lambda i: (0, 0, 0)), smem_rows(lambda i: (jnp.minimum(i + 1, steps - 1), 0, 0)),
                  row(D_MODEL), row(LANES), pl.BlockSpec((1, D_MODEL), lambda i: (0, 0)),
                  pl.BlockSpec(memory_space=pl.ANY)],
        out_specs=row(D_MODEL), out_shape=jax.ShapeDtypeStruct((t, D_MODEL), F32),
        scratch_shapes=[pltpu.VMEM((2, MOE_TOPK * tm, D_MODEL), F32), pltpu.SemaphoreType.DMA((2,))],
        compiler_params=_cparams("arbitrary"), name="moe_combine",
    )(rows, rows, x2, route, g2, y_blocks)


def _tiles(batch, seq):
    t = batch * seq
    return dict(
        tm_proj=min(256, t), tq=min(256, seq), tk=min(256, seq), tm_sgu=min(512, t),
        s5_steps=min(128, seq), tm_merge=min(256, t), bm=256, tp_plan=min(512, t), tm_scat=min(256, t),
        tm_comb=min(256, t))


def kernel(x, positions, norm_mix_g, w_in, cmp_pos_k, cmp_pos_v, cmp_k_w1, cmp_k_w2, cmp_v_w1, cmp_v_w2, w_attn_o, sgu_norm_g, sgu_w, sgu_b, w_sgu_o, s5_a_re, s5_a_im, s5_log_step, s5_b_re, s5_b_im, s5_c_re, s5_c_im, s5_d, s5_glu_w, s5_glu_b, w_s5_o, w_mix_o, norm_ffn_g, router_group_w, router_group_b, router_expert_w, router_expert_b, expert_w_gate, expert_w_up, expert_w_down, norm_final_g):
    batch, seq, _ = x.shape
    depth = w_in.shape[0]
    cfg = _tiles(batch, seq)
    tables = _rope_tables(positions)
    x2 = x.reshape(batch * seq, D_MODEL)
    x_buf = None
    w_in_all = _prep_in_proj(w_in)
    merge_ws = tuple(w.astype(BF16) for w in (w_attn_o, w_sgu_o, w_s5_o, w_mix_o))
    expert_ws = tuple(w.astype(BF16) for w in (expert_w_gate, expert_w_up, expert_w_down))
    for l in range(depth):
        qt, qrt, kvc, ksel, kwin, vt, gt, sgu_in, s5_in, bg = _in_proj(
            x2, norm_mix_g[l].reshape(1, D_MODEL), w_in_all, l, tables, seq, cfg["tm_proj"])
        kc, vct = _compress(kvc, _prep_compress(cmp_pos_k[l], cmp_pos_v[l], cmp_k_w1[l], cmp_k_w2[l],
                                                cmp_v_w1[l], cmp_v_w2[l]), batch, seq)
        attn = _attention(qt, qrt, gt, kc, vct, ksel, kwin, vt, batch, seq, cfg["tq"], cfg["tk"])
        sgu = _sgu(sgu_in, sgu_norm_g[l], sgu_w[l], sgu_b[l], cfg["tm_sgu"])
        s5 = _s5(s5_in, _prep_s5(s5_a_re[l], s5_a_im[l], s5_log_step[l], s5_b_re[l], s5_b_im[l],
                                 s5_c_re[l], s5_c_im[l]),
                 s5_d[l], s5_glu_w[l], s5_glu_b[l], batch, seq, cfg["s5_steps"])
        x_mid, h, route = _merge(x2, attn, sgu, s5, bg, merge_ws, l,
                                 norm_ffn_g[l], router_group_w[l], router_group_b[l],
                                 router_expert_w[l], router_expert_b[l], cfg["tm_merge"])
        y_blocks, dest, x_buf = _experts(h, route, x_buf, expert_ws, l, cfg["bm"], cfg["tp_plan"], cfg["tm_scat"])
        x2 = _combine(x_mid, y_blocks, dest, route, norm_final_g, l == depth - 1, cfg["tm_comb"])
    return x2.reshape(batch, seq, D_MODEL)
```

```python
import functools
import math

import numpy as np
import jax
import jax.numpy as jnp
from jax import lax
from jax.experimental import pallas as pl
from jax.experimental.pallas import tpu as pltpu

F32 = jnp.float32
BF16 = jnp.bfloat16

D_MODEL = 1024
HEAD_DIM = 64
NSA_HEADS = 8
NSA_KV_GROUPS = 2
HEADS_PER_GROUP = NSA_HEADS // NSA_KV_GROUPS
GROUP_WIDTH = HEADS_PER_GROUP * HEAD_DIM
NSA_WIDTH = NSA_HEADS * HEAD_DIM
KV_WIDTH = NSA_KV_GROUPS * HEAD_DIM
ROPE_DIM = HEAD_DIM // 4
ROPE_HALF = ROPE_DIM // 2
ROPE_THETA = 500000.0
CMP_LEN = 32
CMP_STRIDE = 16
CMP_HIDDEN = 128
SLC_LEN = 64
SLC_SHIFT = int(math.log2(SLC_LEN))
SLC_TOP = 16
WINDOW = 512
FORCE_SCORE = 1.0e4
NEG_INF = -1.0e30
SGU_WIDTH = 256
SGU_GROUPS = 4
SGU_CHUNK = 128
S5_WIDTH = 256
S5_GROUP_CH = 16
S5_GROUPS = S5_WIDTH // S5_GROUP_CH
S5_STATE = 64
S5_LANES = S5_GROUPS * S5_STATE
MOE_GROUPS = 4
MOE_EXPERTS_PER_GROUP = 8
MOE_EXPERTS = MOE_GROUPS * MOE_EXPERTS_PER_GROUP
MOE_TOPK = 2
EXPERT_HIDDEN = 512
ROUTER_ROWS = 40
RMS_EPS = 1e-6
ATTN_SCALE = HEAD_DIM ** -0.5
LOG2_E = math.log2(math.e)

LANES = 128
SUBLANES = 8
VMEM_LIMIT_BYTES = 56 * 1024 * 1024

BF16_SUBLANES = 16
V_ROWS = HEAD_DIM + BF16_SUBLANES
GATE_COLS = HEADS_PER_GROUP * 3
GATE_ROWS = 16

_OFF_Q = 0
_OFF_KVC = _OFF_Q + NSA_WIDTH
_OFF_KS = _OFF_KVC + 2 * KV_WIDTH
_OFF_VS = _OFF_KS + KV_WIDTH
_OFF_KW = _OFF_VS + KV_WIDTH
_OFF_VW = _OFF_KW + KV_WIDTH
_OFF_GATE = _OFF_VW + KV_WIDTH
_OFF_SGU = _OFF_GATE + LANES
_OFF_S5 = _OFF_SGU + 2 * SGU_WIDTH
_OFF_BG = _OFF_S5 + S5_WIDTH
_IN_W = _OFF_BG + 3 * D_MODEL


def _cparams(*sem):
    return pltpu.CompilerParams(dimension_semantics=sem, vmem_limit_bytes=VMEM_LIMIT_BYTES)


def _gelu(x):
    return 0.5 * x * (1.0 + jnp.tanh(math.sqrt(2.0 / math.pi) * (x + 0.044715 * (x * x * x))))


def _sigmoid(x):
    return 1.0 / (1.0 + jnp.exp(-x))


def _dot(a, b):
    return jnp.dot(a, b, preferred_element_type=F32)


def _dot_nt(a, b):
    return lax.dot_general(a, b, (((1,), (1,)), ((), ())), preferred_element_type=F32)


def _split_bf16(x):
    hi = x.astype(BF16)
    lo = (x - hi.astype(F32)).astype(BF16)
    return hi, lo


def _rms(x, g):
    return x * lax.rsqrt(jnp.mean(x * x, axis=-1, keepdims=True) + RMS_EPS) * g


def _rope(x, c, s_lo, s_hi):
    n = x.shape[-1]
    return x * c + pltpu.roll(x, n - ROPE_HALF, 1) * s_lo + pltpu.roll(x, ROPE_HALF, 1) * s_hi


def _in_proj_kernel(x_ref, g_ref, w_ref, c_ref, slo_ref, shi_ref,
                    qt_ref, qrt_ref, kvc_ref, ksel_ref, kwin_ref, vt_ref, gt_ref, sgu_ref, s5_ref, bg_ref, *, seq):
    tm = x_ref.shape[0]
    x = x_ref[...]
    hb = _rms(x, g_ref[...]).astype(BF16)

    def proj(off, width):
        return _dot(hb, w_ref[:, off:off + width])

    c, s_lo, s_hi = c_ref[...], slo_ref[...], shi_ref[...]
    rep = NSA_WIDTH // LANES
    q = proj(_OFF_Q, NSA_WIDTH)
    qt_ref[...] = (q * ATTN_SCALE).T
    qrot = _rope(q, jnp.concatenate([c] * rep, axis=1), jnp.concatenate([s_lo] * rep, axis=1),
                 jnp.concatenate([s_hi] * rep, axis=1))
    qrt_ref[...] = (qrot * (ATTN_SCALE * LOG2_E)).T.astype(BF16)
    kvc_ref[...] = proj(_OFF_KVC, 2 * KV_WIDTH)
    ks = _rope(proj(_OFF_KS, KV_WIDTH), c, s_lo, s_hi)
    kw = _rope(proj(_OFF_KW, KV_WIDTH), c, s_lo, s_hi)
    pos = lax.rem(pl.program_id(0) * tm, seq) + lax.broadcasted_iota(jnp.int32, (tm, 1), 0)
    blk = lax.shift_right_logical(pos, SLC_SHIFT)
    blk_onehot = jnp.where(lax.broadcasted_iota(jnp.int32, (1, HEAD_DIM), 1) == blk, 1.0, 0.0)
    for g in range(NSA_KV_GROUPS):
        cols = slice(g * HEAD_DIM, (g + 1) * HEAD_DIM)
        ksel_ref[g] = jnp.concatenate([ks[:, cols], blk_onehot], axis=1).astype(BF16)
        kwin_ref[g] = kw[:, cols].astype(BF16)
    aug_rows = lax.broadcasted_iota(jnp.int32, (V_ROWS - HEAD_DIM, LANES), 0)
    ones_row = jnp.where(aug_rows == 0, 1.0, 0.0).astype(BF16)
    for p, off in enumerate((_OFF_VS, _OFF_VW)):
        vt = proj(off, KV_WIDTH).T.astype(BF16)
        for g in range(NSA_KV_GROUPS):
            for ch in range(tm // LANES):
                idx = p * NSA_KV_GROUPS + g
                vt_ref[idx, ch, 0:HEAD_DIM, :] = vt[g * HEAD_DIM:(g + 1) * HEAD_DIM, ch * LANES:(ch + 1) * LANES]
                vt_ref[idx, ch, HEAD_DIM:V_ROWS, :] = ones_row
    gt_ref[...] = _sigmoid(proj(_OFF_GATE, LANES)).T[:NSA_KV_GROUPS * GATE_ROWS]
    sgu_ref[...] = proj(_OFF_SGU, 2 * SGU_WIDTH)
    s5_ref[...] = proj(_OFF_S5, S5_WIDTH)
    for k in range(3):
        bg_ref[:, k * D_MODEL:(k + 1) * D_MODEL] = _sigmoid(
            proj(_OFF_BG + k * D_MODEL, D_MODEL)).astype(BF16)


_SRC_GATE = NSA_WIDTH + 6 * KV_WIDTH
_SRC_REST = _SRC_GATE + 3 * NSA_HEADS


def _w_in_kernel(w_ref, o_ref):
    o_ref[:, 0:_OFF_GATE] = w_ref[:, 0:_OFF_GATE].astype(BF16)
    rows = w_ref.shape[0]
    gates = []
    for g in range(NSA_KV_GROUPS):
        gates += [w_ref[:, _SRC_GATE + g * GATE_COLS:_SRC_GATE + (g + 1) * GATE_COLS],
                  jnp.zeros((rows, GATE_ROWS - GATE_COLS), F32)]
    gates.append(jnp.zeros((rows, LANES - NSA_KV_GROUPS * GATE_ROWS), F32))
    o_ref[:, _OFF_GATE:_OFF_SGU] = jnp.concatenate(gates, axis=1).astype(BF16)
    o_ref[:, _OFF_SGU:_IN_W] = w_ref[:, _SRC_REST:_SRC_REST + (_IN_W - _OFF_SGU)].astype(BF16)


def _prep_in_proj(w_in):
    layers, d, width = w_in.shape
    tr = LANES
    return pl.pallas_call(
        _w_in_kernel, grid=(layers, d // tr),
        in_specs=[pl.BlockSpec((None, tr, width), lambda l, i: (l, i, 0))],
        out_specs=pl.BlockSpec((None, tr, _IN_W), lambda l, i: (l, i, 0)),
        out_shape=jax.ShapeDtypeStruct((layers, d, _IN_W), BF16),
        compiler_params=_cparams("parallel", "parallel"), name="w_in_layout",
    )(w_in)


def _layer_spec(a, layer):
    return pl.BlockSpec((None,) + a.shape[1:], lambda *_: (layer,) + (0,) * (a.ndim - 1))


def _rope_tables(positions):
    inv_freq = ROPE_THETA ** (-jnp.arange(ROPE_HALF, dtype=F32) / ROPE_HALF)
    ang = positions.astype(F32).reshape(-1, 1) * inv_freq
    cos, sin = jnp.cos(ang), jnp.sin(ang)
    t = ang.shape[0]
    rest = HEAD_DIM - ROPE_DIM
    c = jnp.concatenate([cos, cos, jnp.ones((t, rest), F32)], axis=1)
    s_lo = jnp.concatenate([-sin, jnp.zeros((t, HEAD_DIM - ROPE_HALF), F32)], axis=1)
    s_hi = jnp.concatenate([jnp.zeros((t, ROPE_HALF), F32), sin, jnp.zeros((t, rest), F32)], axis=1)
    rep = LANES // HEAD_DIM
    return tuple(jnp.concatenate([a] * rep, axis=1) for a in (c, s_lo, s_hi))


def _in_proj(x2, g, w_all, layer, tables, seq, tm):
    t = x2.shape[0]
    row = lambda width: pl.BlockSpec((tm, width), lambda i: (i, 0))
    col = lambda height: pl.BlockSpec((height, tm), lambda i: (0, i))
    full = lambda a: pl.BlockSpec(a.shape, lambda i: (0,) * a.ndim)
    n_kv = 2 * NSA_KV_GROUPS
    out_shape = (
        jax.ShapeDtypeStruct((NSA_WIDTH, t), F32),
        jax.ShapeDtypeStruct((NSA_WIDTH, t), BF16),
        jax.ShapeDtypeStruct((t, 2 * KV_WIDTH), F32),
        jax.ShapeDtypeStruct((NSA_KV_GROUPS, t, 2 * HEAD_DIM), BF16),
        jax.ShapeDtypeStruct((NSA_KV_GROUPS, t, HEAD_DIM), BF16),
        jax.ShapeDtypeStruct((n_kv, t // LANES, V_ROWS, LANES), BF16),
        jax.ShapeDtypeStruct((NSA_KV_GROUPS * GATE_ROWS, t), F32),
        jax.ShapeDtypeStruct((t, 2 * SGU_WIDTH), F32),
        jax.ShapeDtypeStruct((seq, (t // seq) * S5_WIDTH), F32),
        jax.ShapeDtypeStruct((t, 3 * D_MODEL), BF16),
    )
    seq_tiles = seq // tm
    time_major = pl.BlockSpec((tm, S5_WIDTH), lambda i: (i % seq_tiles, i // seq_tiles))
    out_specs = (col(NSA_WIDTH), col(NSA_WIDTH), row(2 * KV_WIDTH),
                 pl.BlockSpec((NSA_KV_GROUPS, tm, 2 * HEAD_DIM), lambda i: (0, i, 0)),
                 pl.BlockSpec((NSA_KV_GROUPS, tm, HEAD_DIM), lambda i: (0, i, 0)),
                 pl.BlockSpec((n_kv, tm // LANES, V_ROWS, LANES), lambda i: (0, i, 0, 0)),
                 col(NSA_KV_GROUPS * GATE_ROWS), row(2 * SGU_WIDTH), time_major, row(3 * D_MODEL))
    return pl.pallas_call(
        functools.partial(_in_proj_kernel, seq=seq), grid=(t // tm,),
        in_specs=[row(D_MODEL), full(g), _layer_spec(w_all, layer), row(LANES), row(LANES), row(LANES)],
        out_specs=out_specs, out_shape=out_shape,
        compiler_params=_cparams("parallel"), name="in_proj",
    )(x2, g, w_all, *tables)


def _compress_kernel(r_ref, wa_ref, wb_ref, pos_ref, w1_ref, w2_ref, w2t_ref, kc_ref, vct_ref):
    rb = r_ref[0].astype(BF16)
    n = rb.shape[0]
    a = _dot(rb, wa_ref[...])
    b = _dot(rb, wb_ref[...])
    b = pltpu.roll(b, n - 1, 0)
    rows = lax.broadcasted_iota(jnp.int32, b.shape, 0)
    b = jnp.where(rows == n - 1, 0.0, b)
    for m in range(2 * NSA_KV_GROUPS):
        typ, g = divmod(m, NSA_KV_GROUPS)
        bias = _dot(pos_ref[typ].astype(BF16), w1_ref[typ])
        hid = a[:, m * CMP_HIDDEN:(m + 1) * CMP_HIDDEN] + b[:, m * CMP_HIDDEN:(m + 1) * CMP_HIDDEN] + bias
        act = _gelu(hid).astype(BF16)
        if typ == 0:
            kc_ref[0, g] = _dot(act, w2_ref[...])
        else:
            vct_ref[0, g] = _dot_nt(w2t_ref[...], act)


def _prep_compress(cmp_pos_k, cmp_pos_v, k_w1, k_w2, v_w1, v_w2):
    streams = 2 * NSA_KV_GROUPS
    w1 = jnp.stack([k_w1, v_w1]).reshape(2, CMP_LEN, HEAD_DIM, CMP_HIDDEN)

    def expand(half):
        out = jnp.zeros((CMP_STRIDE, streams, HEAD_DIM, streams, CMP_HIDDEN), F32)
        for m in range(streams):
            out = out.at[:, m, :, m, :].set(w1[m // NSA_KV_GROUPS, half * CMP_STRIDE:(half + 1) * CMP_STRIDE])
        return out.reshape(CMP_STRIDE * streams * HEAD_DIM, streams * CMP_HIDDEN).astype(BF16)

    pos = jnp.stack([cmp_pos_k, cmp_pos_v]).reshape(2, 1, CMP_LEN * HEAD_DIM)
    return (expand(0), expand(1), pos, jnp.stack([k_w1, v_w1]).astype(BF16),
            k_w2.astype(BF16), v_w2.T.astype(BF16))


def _compress(kvc, prep, batch, seq):
    n = seq // CMP_STRIDE
    r = kvc.reshape(batch, n, CMP_STRIDE * 2 * KV_WIDTH)
    full = lambda a: pl.BlockSpec(a.shape, lambda b: (0,) * a.ndim)
    return pl.pallas_call(
        _compress_kernel, grid=(batch,),
        in_specs=[pl.BlockSpec((1, n, r.shape[-1]), lambda b: (b, 0, 0))] + [full(a) for a in prep],
        out_specs=(pl.BlockSpec((1, NSA_KV_GROUPS, n, HEAD_DIM), lambda b: (b, 0, 0, 0)),
                   pl.BlockSpec((1, NSA_KV_GROUPS, HEAD_DIM, n), lambda b: (b, 0, 0, 0))),
        out_shape=(jax.ShapeDtypeStruct((batch, NSA_KV_GROUPS, n, HEAD_DIM), F32),
                   jax.ShapeDtypeStruct((batch, NSA_KV_GROUPS, HEAD_DIM, n), F32)),
        compiler_params=_cparams("parallel"), name="compress_kv",
    )(r, *prep)


def _attn_kernel(qt_ref, qrt_ref, gt_ref, kc_ref, vct_ref, ks_ref, vst_ref, kw_ref, vwt_ref, ovt_ref,
                 o_ref, qa_sc, m_sc, acc_sc, oc_sc, s_sc, p_sc, a_sc, *, tq, tk, seq):
    n_cmp = kc_ref.shape[2]
    n_slc = seq // SLC_LEN
    heads = HEADS_PER_GROUP
    t0 = pl.program_id(2) * tq
    t_lane = t0 + lax.broadcasted_iota(jnp.int32, (1, tq), 1)

    def lane_tile(a):
        return jnp.concatenate([a] * heads, axis=1)

    def head_cols(ref):
        return jnp.concatenate([ref[r * HEAD_DIM:(r + 1) * HEAD_DIM, :] for r in range(heads)], axis=1)

    q_hi, q_lo = _split_bf16(head_cols(qt_ref))
    qr = head_cols(qrt_ref)

    def compressed_and_select(nc, ns, all_selected, forced_distinct):
        kc_hi, kc_lo = _split_bf16(kc_ref[0, 0, 0:nc, :])
        s = _dot(kc_hi, q_hi) + _dot(kc_lo, q_hi) + _dot(kc_hi, q_lo)
        cmp_end = lax.broadcasted_iota(jnp.int32, (nc, 1), 0) * CMP_STRIDE + (CMP_LEN - 1)
        m_c = cmp_end <= lane_tile(t_lane)
        s = jnp.where(m_c, s, NEG_INF)
        e = jnp.where(m_c, jnp.exp(s - jnp.max(s, axis=0, keepdims=True)), 0.0)
        p = e / jnp.maximum(jnp.sum(e, axis=0, keepdims=True), 1e-30)
        oc_sc[...] = _dot(vct_ref[0, 0, :, 0:nc].astype(BF16), p.astype(BF16))

        j_idx = lax.broadcasted_iota(jnp.int32, (ns, 1), 0)
        blk_t = lax.shift_right_logical(t_lane, SLC_SHIFT)
        causal = j_idx <= blk_t
        if all_selected:
            picked = causal
        else:
            p_sum = p[:, 0:tq]
            for r in range(1, heads):
                p_sum = p_sum + p[:, r * tq:(r + 1) * tq]
            ps_hi, ps_lo = _split_bf16(p_sum)
            imp = _dot(ovt_ref[0:ns, 0:nc], ps_hi) + _dot(ovt_ref[0:ns, 0:nc], ps_lo)
            forced = (j_idx == 0) | (j_idx == blk_t) | (j_idx == blk_t - 1)
            j_f32 = j_idx.astype(F32)
            if forced_distinct:
                taken = jnp.where(forced, 1.0, 0.0)
                imp = jnp.where(forced, -jnp.inf, jnp.where(causal, imp, NEG_INF))
                rounds = SLC_TOP - 3
            else:
                taken = jnp.zeros((ns, tq), F32)
                imp = jnp.where(causal, jnp.where(forced, FORCE_SCORE, imp), NEG_INF)
                rounds = min(SLC_TOP, ns)
            for _ in range(rounds):
                best = jnp.max(imp, axis=0, keepdims=True)
                first = jnp.min(jnp.where(imp == best, j_f32, float(ns)), axis=0, keepdims=True)
                hit = j_f32 == first
                taken = jnp.where(hit, 1.0, taken)
                imp = jnp.where(hit, -jnp.inf, imp)
            picked = (taken > 0.5) & causal
        bias = jnp.where(picked, 0.0, NEG_INF)
        if ns < HEAD_DIM:
            bias = jnp.concatenate([bias, jnp.zeros((HEAD_DIM - ns, tq), F32)], axis=0)
        qa_sc[...] = jnp.concatenate([qr, lane_tile(bias.astype(BF16))], axis=0)

    nq = seq // tq
    qi = pl.program_id(2)

    def sizes(tiles):
        nc = min(n_cmp, -(-max(tiles * tq // CMP_STRIDE, 1) // LANES) * LANES)
        ns = min(n_slc, -(-max(tiles * tq // SLC_LEN, 1) // SUBLANES) * SUBLANES)
        return nc, ns

    bounds = sorted({0, min(nq, SLC_TOP * SLC_LEN // tq), min(nq, max(nq // 2, 1)), nq})
    for lo, hi in zip(bounds[:-1], bounds[1:]):
        nc, ns = sizes(hi)
        variant = functools.partial(compressed_and_select, nc, ns, hi * tq <= SLC_TOP * SLC_LEN,
                                    lo * tq >= 2 * SLC_LEN)
        pl.when((qi >= lo) & (qi < hi))(variant)

    m_sc[...] = jnp.full(m_sc.shape, NEG_INF, F32)
    acc_sc[...] = jnp.zeros(acc_sc.shape, F32)
    p_sc[1] = jnp.zeros(p_sc.shape[1:], BF16)
    a_sc[1] = jnp.ones(a_sc.shape[1:], F32)
    lanes_per_tile = tk // LANES

    def scores(j, slot):
        k0 = j * tk if isinstance(j, int) else pl.multiple_of(j * tk, tk)
        s_sc[slot] = _dot(ks_ref[0, pl.ds(k0, tk), :], qa_sc[...])

    def softmax(j, slot, diagonal):
        s = s_sc[slot]
        if diagonal:
            key_pos = j * tk + lax.broadcasted_iota(jnp.int32, (tk, 1), 0)
            s = s + lane_tile(jnp.where(key_pos <= t_lane, 0.0, NEG_INF))
        m_old = m_sc[...]
        m_new = jnp.maximum(m_old, jnp.max(s, axis=0, keepdims=True))
        m_sc[...] = m_new
        a_sc[slot] = jnp.exp2(m_old - m_new)
        p_sc[slot] = jnp.exp2(s - m_new).astype(BF16)

    def accumulate(j, slot):
        tile = jnp.maximum(j, 0) * lanes_per_tile
        v_t = jnp.concatenate([vst_ref[0, tile + c] for c in range(lanes_per_tile)], axis=1)
        acc_sc[...] = a_sc[slot] * acc_sc[...] + _dot(v_t, p_sc[slot])

    def pipe_step(i, i_is_odd):
        a, b = (1, 0) if i_is_odd else (0, 1)
        accumulate(i - 2, a)
        softmax(i - 1, b, False)
        scores(i, a)

    n_kv = (t0 + tq + tk - 1) // tk
    scores(0, 0)

    def two_steps(u, carry):
        pipe_step(2 * u + 1, True)
        pipe_step(2 * u + 2, False)
        return carry

    lax.fori_loop(0, (n_kv - 1) // 2, two_steps, 0)

    @pl.when(((n_kv - 1) & 1) == 1)
    def _():
        pipe_step(n_kv - 1, True)

    last = (n_kv - 1) & 1
    accumulate(n_kv - 2, 1 - last)
    softmax(n_kv - 1, last, True)
    accumulate(n_kv - 1, last)
    o_slc = acc_sc[0:HEAD_DIM, :] / acc_sc[HEAD_DIM:HEAD_DIM + 1, :]
    o_cmp = oc_sc[...]

    slab = min(WINDOW + tq, seq)
    st = pl.multiple_of(jnp.clip(t0 - WINDOW, 0, seq - slab), LANES)
    s = _dot(kw_ref[0, pl.ds(st, slab), :], qa_sc[0:HEAD_DIM, :])
    diff = t_lane - (st + lax.broadcasted_iota(jnp.int32, (slab, 1), 0))
    s = s + lane_tile(jnp.where((diff >= 0) & (diff < WINDOW), 0.0, NEG_INF))
    p = jnp.exp2(s - jnp.max(s, axis=0, keepdims=True))
    st_tile = st // LANES
    vw_t = jnp.concatenate([vwt_ref[0, st_tile + c] for c in range(slab // LANES)], axis=1)
    pv = _dot(vw_t, p.astype(BF16))
    o_win = pv[0:HEAD_DIM] / pv[HEAD_DIM:HEAD_DIM + 1]

    gates = gt_ref[...]
    outs = []
    for r in range(heads):
        cols = slice(r * tq, (r + 1) * tq)
        outs.append(gates[3 * r:3 * r + 1, :] * o_cmp[:, cols] + gates[3 * r + 1:3 * r + 2, :] * o_slc[:, cols]
                    + gates[3 * r + 2:3 * r + 3, :] * o_win[:, cols])
    o_ref[...] = jnp.concatenate(outs, axis=0).T.astype(o_ref.dtype)


def _overlap_matrix_t(n_cmp, n_slc):
    c0 = np.arange(n_cmp)[None, :] * CMP_STRIDE
    s0 = np.arange(n_slc)[:, None] * SLC_LEN
    ov = np.clip(np.minimum(c0 + CMP_LEN, s0 + SLC_LEN) - np.maximum(c0, s0), 0, None) / CMP_LEN
    ov[:, n_cmp - 1] = 0.0
    return jnp.asarray(ov, BF16)


def _attention(qt, qrt, gt, kc, vct, ksel, kwin, vt, batch, seq, tq, tk):
    assert tk % tq == 0 and seq // SLC_LEN <= HEAD_DIM
    n_cmp = seq // CMP_STRIDE
    n_slc = seq // SLC_LEN
    nq = seq // tq
    ksel = ksel.reshape(NSA_KV_GROUPS, batch, seq, 2 * HEAD_DIM)
    kwin = kwin.reshape(NSA_KV_GROUPS, batch, seq, HEAD_DIM)
    vt = vt.reshape(2 * NSA_KV_GROUPS, batch, seq // LANES, V_ROWS, LANES)
    ovt = _overlap_matrix_t(n_cmp, n_slc)
    qspec = pl.BlockSpec((GROUP_WIDTH, tq), lambda b, g, i: (g, b * nq + i))
    k_spec = lambda width: pl.BlockSpec((None, 1, seq, width), lambda b, g, i: (g, b, 0, 0))
    v_spec = lambda p: pl.BlockSpec((None, 1, seq // LANES, V_ROWS, LANES),
                                    lambda b, g, i: (p * NSA_KV_GROUPS + g, b, 0, 0, 0))
    kernel = functools.partial(_attn_kernel, tq=tq, tk=tk, seq=seq)
    return pl.pallas_call(
        kernel, grid=(batch, NSA_KV_GROUPS, nq),
        in_specs=[qspec, qspec, pl.BlockSpec((GATE_ROWS, tq), lambda b, g, i: (g, b * nq + i)),
                  pl.BlockSpec((1, 1, n_cmp, HEAD_DIM), lambda b, g, i: (b, g, 0, 0)),
                  pl.BlockSpec((1, 1, HEAD_DIM, n_cmp), lambda b, g, i: (b, g, 0, 0)),
                  k_spec(2 * HEAD_DIM), v_spec(0), k_spec(HEAD_DIM), v_spec(1),
                  pl.BlockSpec(ovt.shape, lambda b, g, i: (0, 0))],
        out_specs=pl.BlockSpec((tq, GROUP_WIDTH), lambda b, g, i: (b * nq + i, g)),
        out_shape=jax.ShapeDtypeStruct((batch * seq, NSA_WIDTH), BF16),
        scratch_shapes=[pltpu.VMEM((2 * HEAD_DIM, HEADS_PER_GROUP * tq), BF16),
                        pltpu.VMEM((1, HEADS_PER_GROUP * tq), F32),
                        pltpu.VMEM((V_ROWS, HEADS_PER_GROUP * tq), F32),
                        pltpu.VMEM((HEAD_DIM, HEADS_PER_GROUP * tq), F32),
                        pltpu.VMEM((2, tk, HEADS_PER_GROUP * tq), F32),
                        pltpu.VMEM((2, tk, HEADS_PER_GROUP * tq), BF16),
                        pltpu.VMEM((2, 1, HEADS_PER_GROUP * tq), F32)],
        compiler_params=_cparams("parallel", "parallel", "arbitrary"), name="nsa_attention",
    )(qt, qrt, gt, kc, vct, ksel, vt, kwin, vt, ovt)


def _sgu_kernel(uv_ref, g_ref, w_ref, b_ref, o_ref, *, chunks):
    z = _gelu(uv_ref[...])
    u = z[:, :SGU_WIDTH]
    v = _rms(z[:, SGU_WIDTH:], g_ref[...])
    gw = SGU_WIDTH // SGU_GROUPS
    rows = lax.broadcasted_iota(jnp.int32, (SGU_CHUNK, SGU_GROUPS * SGU_CHUNK), 0)
    cols = lax.broadcasted_iota(jnp.int32, (SGU_CHUNK, SGU_GROUPS * SGU_CHUNK), 1)
    w = jnp.where((cols & (SGU_CHUNK - 1)) <= rows, w_ref[...], 0.0).astype(BF16)
    grp_r = lax.broadcasted_iota(jnp.int32, (SGU_GROUPS * SGU_CHUNK, SGU_WIDTH), 0) // SGU_CHUNK
    grp_c = lax.broadcasted_iota(jnp.int32, (SGU_GROUPS * SGU_CHUNK, SGU_WIDTH), 1) // gw
    for c in range(chunks):
        vc = v[c * SGU_CHUNK:(c + 1) * SGU_CHUNK].astype(BF16)
        v_bd = jnp.where(grp_r == grp_c, jnp.concatenate([vc] * SGU_GROUPS, axis=0), jnp.zeros((), BF16))
        mixed = _dot(w, v_bd) + b_ref[...]
        o_ref[c * SGU_CHUNK:(c + 1) * SGU_CHUNK, :] = (u[c * SGU_CHUNK:(c + 1) * SGU_CHUNK] * mixed).astype(o_ref.dtype)


def _sgu(sgu_in, norm_g, w_s, b_s, tm):
    t = sgu_in.shape[0]
    w_cat = jnp.transpose(w_s, (1, 0, 2)).reshape(SGU_CHUNK, SGU_GROUPS * SGU_CHUNK)
    bias = jnp.repeat(b_s.T, SGU_WIDTH // SGU_GROUPS, axis=1)
    g = norm_g.reshape(1, SGU_WIDTH)
    full = lambda a: pl.BlockSpec(a.shape, lambda i: (0,) * a.ndim)
    return pl.pallas_call(
        functools.partial(_sgu_kernel, chunks=tm // SGU_CHUNK), grid=(t // tm,),
        in_specs=[pl.BlockSpec((tm, 2 * SGU_WIDTH), lambda i: (i, 0)), full(g), full(w_cat), full(bias)],
        out_specs=pl.BlockSpec((tm, SGU_WIDTH), lambda i: (i, 0)),
        out_shape=jax.ShapeDtypeStruct((t, SGU_WIDTH), BF16),
        compiler_params=_cparams("parallel"), name="sgu",
    )(sgu_in, g, w_cat, bias)


def _s5_kernel(x_ref, bre_ref, bim_ref, are_ref, aim_ref, cre_ref, cim_ref, d_ref, gw_ref, gb_ref,
               o_ref, hre_sc, him_sc, ure_sc, uim_sc, *, steps, batch):
    @pl.when(pl.program_id(0) == 0)
    def _():
        hre_sc[...] = jnp.zeros(hre_sc.shape, F32)
        him_sc[...] = jnp.zeros(him_sc.shape, F32)

    x = x_ref[...]
    xb = x.astype(BF16)
    ure_sc[...] = _dot(xb, bre_ref[...])
    uim_sc[...] = _dot(xb, bim_ref[...])
    a_re = jnp.broadcast_to(are_ref[...], (batch, S5_LANES))
    a_im = jnp.broadcast_to(aim_ref[...], (batch, S5_LANES))

    def step(t, carry):
        h_re, h_im = carry
        rows = pl.ds(pl.multiple_of(t * batch, batch), batch)
        n_re = a_re * h_re - a_im * h_im + ure_sc[rows, :]
        n_im = a_re * h_im + a_im * h_re + uim_sc[rows, :]
        ure_sc[rows, :] = n_re
        uim_sc[rows, :] = n_im
        return n_re, n_im

    h_re, h_im = lax.fori_loop(0, steps, step, (hre_sc[...], him_sc[...]))
    hre_sc[...] = h_re
    him_sc[...] = h_im
    y = _dot(ure_sc[...].astype(BF16), cre_ref[...]) - _dot(uim_sc[...].astype(BF16), cim_ref[...])
    y = _gelu(y + d_ref[...] * x)
    y = y * _sigmoid(_dot(y.astype(BF16), gw_ref[...]) + gb_ref[...])
    o_ref[...] = y.astype(o_ref.dtype)


def _block_diag(blocks):
    g, r, c = blocks.shape
    eye = jnp.eye(g, dtype=blocks.dtype)
    return (blocks[:, :, None, :] * eye[:, None, :, None]).reshape(g * r, g * c)


def _prep_s5(a_re, a_im, log_step, b_re, b_im, c_re, c_im):
    step = jnp.exp(log_step)[:, None]
    mag = jnp.exp(a_re * step)
    abar_re, abar_im = mag * jnp.cos(a_im * step), mag * jnp.sin(a_im * step)
    den = a_re * a_re + a_im * a_im
    nr, ni = abar_re - 1.0, abar_im
    coef_re = (nr * a_re + ni * a_im) / den
    coef_im = (ni * a_re - nr * a_im) / den
    bbar_re = coef_re[..., None] * b_re - coef_im[..., None] * b_im
    bbar_im = coef_re[..., None] * b_im + coef_im[..., None] * b_re
    to_in = lambda b: _block_diag(jnp.transpose(b, (0, 2, 1))).astype(BF16)
    to_out = lambda c: _block_diag(jnp.transpose(c, (0, 2, 1))).astype(BF16)
    return (to_in(bbar_re), to_in(bbar_im), abar_re.reshape(1, S5_LANES), abar_im.reshape(1, S5_LANES),
            to_out(c_re), to_out(c_im))


def _s5(xs, prep, d, glu_w, glu_b, batch, seq, steps):
    bre, bim, are, aim, cre, cim = prep
    x_tm = xs.reshape(seq * batch, S5_WIDTH)
    rows = steps * batch
    d2, gb2, gwb = d.reshape(1, S5_WIDTH), glu_b.reshape(1, S5_WIDTH), glu_w.astype(BF16)
    full = lambda a: pl.BlockSpec(a.shape, lambda i: (0,) * a.ndim)
    y = pl.pallas_call(
        functools.partial(_s5_kernel, steps=steps, batch=batch), grid=(seq // steps,),
        in_specs=[pl.BlockSpec((rows, S5_WIDTH), lambda i: (i, 0)), full(bre), full(bim), full(are), full(aim),
                  full(cre), full(cim), full(d2), full(gwb), full(gb2)],
        out_specs=pl.BlockSpec((rows, S5_WIDTH), lambda i: (i, 0)),
        out_shape=jax.ShapeDtypeStruct((seq * batch, S5_WIDTH), BF16),
        scratch_shapes=[pltpu.VMEM((batch, S5_LANES), F32), pltpu.VMEM((batch, S5_LANES), F32),
                        pltpu.VMEM((rows, S5_LANES), F32), pltpu.VMEM((rows, S5_LANES), F32)],
        compiler_params=_cparams("arbitrary"), name="s5_scan",
    )(x_tm, bre, bim, are, aim, cre, cim, d2, gwb, gb2)
    return y.reshape(seq, batch * S5_WIDTH)


def _merge_kernel(x_ref, a_ref, b_ref, c_ref, bg_ref, wa_ref, wb_ref, wc_ref, wm_ref, g_ref,
                  wrh_ref, wrl_ref, br_ref, xo_ref, h_ref, route_ref):
    y_a = _dot(a_ref[...], wa_ref[...])
    y_b = _dot(b_ref[...], wb_ref[...])
    y_c = _dot(c_ref[...], wc_ref[...])
    merged = (bg_ref[:, :D_MODEL].astype(F32) * y_a + bg_ref[:, D_MODEL:2 * D_MODEL].astype(F32) * y_b
              + bg_ref[:, 2 * D_MODEL:].astype(F32) * y_c)
    x = x_ref[...] + _dot(merged.astype(BF16), wm_ref[...])
    xo_ref[...] = x
    h = _rms(x, g_ref[...])
    h_ref[...] = h

    h_hi, h_lo = _split_bf16(h)
    logits = (_dot_nt(wrh_ref[...], h_hi) + _dot_nt(wrl_ref[...], h_hi) + _dot_nt(wrh_ref[...], h_lo))
    logits = logits[0:ROUTER_ROWS] + br_ref[...]
    row = lax.broadcasted_iota(jnp.int32, (ROUTER_ROWS, 1), 0).astype(F32)
    big = float(ROUTER_ROWS)
    is_grp = row < MOE_GROUPS
    gl = jnp.where(is_grp, logits, -jnp.inf)
    gmax = jnp.max(gl, axis=0, keepdims=True)
    grp = jnp.min(jnp.where(gl == gmax, row, big), axis=0, keepdims=True)
    grp_w = 1.0 / jnp.sum(jnp.where(is_grp, jnp.exp(logits - gmax), 0.0), axis=0, keepdims=True)
    e_row = row - MOE_GROUPS
    in_grp = (e_row >= 0) & (jnp.floor(e_row * (1.0 / MOE_EXPERTS_PER_GROUP)) == grp)
    emax = jnp.max(jnp.where(in_grp, logits, -jnp.inf), axis=0, keepdims=True)
    ee = jnp.where(in_grp, jnp.exp(logits - emax), 0.0)
    prob = jnp.where(in_grp, ee / jnp.sum(ee, axis=0, keepdims=True), -1.0)
    p1 = jnp.max(prob, axis=0, keepdims=True)
    j1 = jnp.min(jnp.where(prob == p1, row, big), axis=0, keepdims=True)
    prob2 = jnp.where(row == j1, -1.0, prob)
    p2 = jnp.max(prob2, axis=0, keepdims=True)
    j2 = jnp.min(jnp.where(prob2 == p2, row, big), axis=0, keepdims=True)
    psum = p1 + p2
    out_row = lax.broadcasted_iota(jnp.int32, (LANES, 1), 0)
    route_t = jnp.where(out_row == 0, grp_w * p1 / psum, 0.0)
    route_t = jnp.where(out_row == 1, grp_w * p2 / psum, route_t)
    route_t = jnp.where(out_row == 2, j1 - MOE_GROUPS, route_t)
    route_t = jnp.where(out_row == 3, j2 - MOE_GROUPS, route_t)
    route_ref[...] = route_t.T


def _merge(x2, attn, sgu, s5, bg, ws, layer, norm_g, rgw, rgb, rew, reb, tm):
    t = x2.shape[0]
    seq_tiles = s5.shape[0] // tm
    wr = jnp.concatenate([rgw, rew, jnp.zeros((D_MODEL, LANES - MOE_GROUPS - MOE_EXPERTS), F32)], axis=1).T
    wr_hi = wr.astype(BF16)
    wr_lo = (wr - wr_hi.astype(F32)).astype(BF16)
    br = jnp.concatenate([rgb, reb, jnp.zeros((ROUTER_ROWS - MOE_GROUPS - MOE_EXPERTS,), F32)]).reshape(ROUTER_ROWS, 1)
    g = norm_g.reshape(1, D_MODEL)
    row = lambda width: pl.BlockSpec((tm, width), lambda i: (i, 0))
    full = lambda a: pl.BlockSpec(a.shape, lambda i: (0,) * a.ndim)
    return pl.pallas_call(
        _merge_kernel, grid=(t // tm,),
        in_specs=[row(D_MODEL), row(NSA_WIDTH), row(SGU_WIDTH),
                  pl.BlockSpec((tm, S5_WIDTH), lambda i: (i % seq_tiles, i // seq_tiles)),
                  row(3 * D_MODEL)]
                 + [_layer_spec(w, layer) for w in ws] + [full(g), full(wr_hi), full(wr_lo), full(br)],
        out_specs=(row(D_MODEL), row(D_MODEL), row(LANES)),
        out_shape=(jax.ShapeDtypeStruct((t, D_MODEL), F32), jax.ShapeDtypeStruct((t, D_MODEL), F32),
                   jax.ShapeDtypeStruct((t, LANES), F32)),
        compiler_params=_cparams("parallel"), name="merge_route",
    )(x2, attn, sgu, s5, bg, *ws, g, wr_hi, wr_lo, br)


def _start_row_gather(src_hbm, rows_of, dst_buf, sem, n):
    for r in range(n):
        pltpu.make_async_copy(src_hbm.at[pl.ds(rows_of(r), 1)], dst_buf.at[pl.ds(r, 1)], sem).start(priority=r % 2)


def _wait_row_gather(src_hbm, dst_buf, sem, n):
    pltpu.make_async_copy(src_hbm.at[pl.ds(0, n)], dst_buf, sem).wait()


def _plan_kernel(route_ref, dest_ref, cnt_ref, carry_sc, *, bm):
    phase, i = pl.program_id(0), pl.program_id(1)
    tp = route_ref.shape[0]
    lane = lax.broadcasted_iota(jnp.int32, (1, LANES), 1)
    route = route_ref[...]
    oh0 = jnp.where(lane == route[:, 2:3].astype(jnp.int32), 1.0, 0.0)
    oh1 = jnp.where(lane == route[:, 3:4].astype(jnp.int32), 1.0, 0.0)
    both = oh0 + oh1
    col_sum = jnp.broadcast_to(jnp.sum(both, axis=0, keepdims=True), carry_sc.shape)

    @pl.when((phase == 0) & (i == 0))
    def _():
        carry_sc[...] = jnp.zeros(carry_sc.shape, F32)

    @pl.when(phase == 0)
    def _():
        carry_sc[...] = carry_sc[...] + col_sum

    @pl.when((phase == 1) & (i == 0))
    def _():
        counts = carry_sc[...]
        cnt_ref[...] = counts
        padded = jnp.floor((counts + (bm - 1)) * (1.0 / bm)) * bm
        incl = padded
        shift = 1
        while shift < LANES:
            incl = incl + jnp.where(lane >= shift, pltpu.roll(incl, shift, 1), 0.0)
            shift *= 2
        carry_sc[...] = incl - padded

    @pl.when(phase == 1)
    def _():
        r_idx = lax.broadcasted_iota(jnp.int32, (tp, tp), 0)
        c_idx = lax.broadcasted_iota(jnp.int32, (tp, tp), 1)
        earlier = jnp.where(c_idx < r_idx, 1.0, 0.0).astype(BF16)
        base = _dot(earlier, both.astype(BF16)) + carry_sc[0:1, :]
        d0 = jnp.sum(oh0 * base, axis=1, keepdims=True)
        d1 = jnp.sum(oh1 * base, axis=1, keepdims=True)
        dest_ref[...] = jnp.where(lane == 0, d0, jnp.where(lane == 1, d1, 0.0)).astype(jnp.int32)
        carry_sc[...] = carry_sc[...] + col_sum


def _dispatch_plan(route, bm, tp):
    t = route.shape[0]
    steps = t // tp
    dest, counts = pl.pallas_call(
        functools.partial(_plan_kernel, bm=bm), grid=(2, steps),
        in_specs=[pl.BlockSpec((tp, LANES), lambda p, i: (i, 0))],
        out_specs=(pl.BlockSpec((tp, LANES), lambda p, i: (i * p, 0)),
                   pl.BlockSpec((SUBLANES, LANES), lambda p, i: (0, 0))),
        out_shape=(jax.ShapeDtypeStruct((t, LANES), jnp.int32), jax.ShapeDtypeStruct((SUBLANES, LANES), F32)),
        scratch_shapes=[pltpu.VMEM((SUBLANES, LANES), F32)],
        compiler_params=_cparams("arbitrary", "arbitrary"), name="moe_plan",
    )(route)
    nb = t * MOE_TOPK // bm + MOE_EXPERTS
    nblk = (counts[0, :MOE_EXPERTS].astype(jnp.int32) + (bm - 1)) // bm
    blk_end = jnp.cumsum(nblk)
    blk_exp = jnp.sum((blk_end[None, :] <= jnp.arange(nb, dtype=jnp.int32)[:, None]).astype(jnp.int32), axis=1)
    return dest[:, :MOE_TOPK], jnp.minimum(blk_exp, MOE_EXPERTS - 1), nb


def _scatter_kernel(rows_ref, h_ref, zero_hbm, x_hbm, sem, *, tm):
    del zero_hbm
    n = MOE_TOPK * tm
    for r in range(n):
        pltpu.make_async_copy(h_ref.at[pl.ds(r // MOE_TOPK, 1)], x_hbm.at[pl.ds(rows_ref[0, 0, r], 1)],
                              sem).start(priority=r % 2)
    for _ in range(MOE_TOPK):
        pltpu.make_async_copy(h_ref, x_hbm.at[pl.ds(0, tm)], sem).wait()


def _scatter_rows(h, dest, x_buf, tm):
    t = h.shape[0]
    steps = t // tm
    rows = dest.reshape(steps, 1, MOE_TOPK * tm)
    return pl.pallas_call(
        functools.partial(_scatter_kernel, tm=tm), grid=(steps,),
        in_specs=[pl.BlockSpec((1, 1, MOE_TOPK * tm), lambda i: (i, 0, 0), memory_space=pltpu.SMEM),
                  pl.BlockSpec((tm, D_MODEL), lambda i: (i, 0)),
                  pl.BlockSpec(memory_space=pl.ANY)],
        out_specs=pl.BlockSpec(memory_space=pl.ANY),
        out_shape=jax.ShapeDtypeStruct(x_buf.shape, F32),
        scratch_shapes=[pltpu.SemaphoreType.DMA(())],
        input_output_aliases={2: 0},
        compiler_params=_cparams("arbitrary"), name="moe_dispatch",
    )(rows, h, x_buf)


def _moe_kernel(blk_exp_ref, x_ref, wg_ref, wu_ref, wd_ref, y_ref, wg_sc, wu_sc, wd_sc):
    i = pl.program_id(0)
    new_expert = (i == 0) | (blk_exp_ref[i] != blk_exp_ref[jnp.maximum(i - 1, 0)])

    @pl.when(new_expert)
    def _():
        wg_sc[...] = wg_ref[0].astype(BF16)
        wu_sc[...] = wu_ref[0].astype(BF16)
        wd_sc[...] = wd_ref[0].astype(BF16)

    xb = x_ref[...].astype(BF16)
    gate = _dot(xb, wg_sc[...])
    hid = gate * _sigmoid(gate) * _dot(xb, wu_sc[...])
    y_ref[...] = _dot(hid.astype(BF16), wd_sc[...])


def _experts(h, route, x_buf, expert_ws, layer, bm, tp, tm):
    dest, blk_exp, nb = _dispatch_plan(route, bm, tp)
    if x_buf is None:
        x_buf = jnp.zeros((nb * bm, D_MODEL), F32)
    x_buf = _scatter_rows(h, dest, x_buf, tm)
    wspec = lambda shape: pl.BlockSpec((None, 1) + shape, lambda i, be: (layer, be[i], 0, 0))
    grid_spec = pltpu.PrefetchScalarGridSpec(
        num_scalar_prefetch=1, grid=(nb,),
        in_specs=[pl.BlockSpec((bm, D_MODEL), lambda i, be: (i, 0)),
                  wspec((D_MODEL, EXPERT_HIDDEN)), wspec((D_MODEL, EXPERT_HIDDEN)),
                  wspec((EXPERT_HIDDEN, D_MODEL))],
        out_specs=pl.BlockSpec((bm, D_MODEL), lambda i, be: (i, 0)),
        scratch_shapes=[pltpu.VMEM((D_MODEL, EXPERT_HIDDEN), BF16), pltpu.VMEM((D_MODEL, EXPERT_HIDDEN), BF16),
                        pltpu.VMEM((EXPERT_HIDDEN, D_MODEL), BF16)])
    y_blocks = pl.pallas_call(
        _moe_kernel, grid_spec=grid_spec,
        out_shape=jax.ShapeDtypeStruct((nb * bm, D_MODEL), F32),
        compiler_params=_cparams("arbitrary"), name="moe_experts",
    )(blk_exp, x_buf, *expert_ws)
    return y_blocks, dest.reshape(-1), x_buf


def _combine_kernel(first_ref, next_ref, x_ref, route_ref, g_ref, y_hbm, o_ref, ybuf, sem, *, tm, final_norm):
    i = pl.program_id(0)
    cur = lax.rem(i, 2)
    nxt = 1 - cur
    n = MOE_TOPK * tm

    def row_of(ref):
        return lambda r: ref[0, 0, r]

    @pl.when(i == 0)
    def _():
        _start_row_gather(y_hbm, row_of(first_ref), ybuf.at[0], sem.at[0], n)

    _start_row_gather(y_hbm, row_of(next_ref), ybuf.at[nxt], sem.at[nxt], n)
    _wait_row_gather(y_hbm, ybuf.at[cur], sem.at[cur], n)
    route = route_ref[...]
    x = x_ref[...] + route[:, 0:1] * ybuf[cur, 0:tm, :] + route[:, 1:2] * ybuf[cur, tm:n, :]
    o_ref[...] = _rms(x, g_ref[...]) if final_norm else x

    @pl.when(i == pl.num_programs(0) - 1)
    def _():
        _wait_row_gather(y_hbm, ybuf.at[nxt], sem.at[nxt], n)


def _combine(x2, y_blocks, dest, route, g, final_norm, tm):
    t = x2.shape[0]
    steps = t // tm
    rows = jnp.transpose(dest.reshape(steps, tm, MOE_TOPK), (0, 2, 1)).reshape(steps, 1, MOE_TOPK * tm)
    g2 = g.reshape(1, D_MODEL)
    row = lambda width: pl.BlockSpec((tm, width), lambda i: (i, 0))
    smem_rows = lambda index_map: pl.BlockSpec((1, 1, MOE_TOPK * tm), index_map, memory_space=pltpu.SMEM)
    return pl.pallas_call(
        functools.partial(_combine_kernel, tm=tm, final_norm=final_norm), grid=(steps,),
        in_specs=[smem_rows(lambda i: (0, 0, 0)), smem_rows(lambda i: (jnp.minimum(i + 1, steps - 1), 0, 0)),
                  row(D_MODEL), row(LANES), pl.BlockSpec((1, D_MODEL), lambda i: (0, 0)),
                  pl.BlockSpec(memory_space=pl.ANY)],
        out_specs=row(D_MODEL), out_shape=jax.ShapeDtypeStruct((t, D_MODEL), F32),
        scratch_shapes=[pltpu.VMEM((2, MOE_TOPK * tm, D_MODEL), F32), pltpu.SemaphoreType.DMA((2,))],
        compiler_params=_cparams("arbitrary"), name="moe_combine",
    )(rows, rows, x2, route, g2, y_blocks)


def _tiles(batch, seq):
    t = batch * seq
    return dict(
        tm_proj=min(256, t), tq=min(256, seq), tk=min(256, seq), tm_sgu=min(512, t),
        s5_steps=min(128, seq), tm_merge=min(256, t), bm=256, tp_plan=min(512, t), tm_scat=min(256, t),
        tm_comb=min(256, t))


def kernel(x, positions, norm_mix_g, w_in, cmp_pos_k, cmp_pos_v, cmp_k_w1, cmp_k_w2, cmp_v_w1, cmp_v_w2, w_attn_o, sgu_norm_g, sgu_w, sgu_b, w_sgu_o, s5_a_re, s5_a_im, s5_log_step, s5_b_re, s5_b_im, s5_c_re, s5_c_im, s5_d, s5_glu_w, s5_glu_b, w_s5_o, w_mix_o, norm_ffn_g, router_group_w, router_group_b, router_expert_w, router_expert_b, expert_w_gate, expert_w_up, expert_w_down, norm_final_g):
    batch, seq, _ = x.shape
    depth = w_in.shape[0]
    cfg = _tiles(batch, seq)
    tables = _rope_tables(positions)
    x2 = x.reshape(batch * seq, D_MODEL)
    x_buf = None
    w_in_all = _prep_in_proj(w_in)
    merge_ws = tuple(w.astype(BF16) for w in (w_attn_o, w_sgu_o, w_s5_o, w_mix_o))
    expert_ws = (expert_w_gate, expert_w_up, expert_w_down)
    for l in range(depth):
        qt, qrt, kvc, ksel, kwin, vt, gt, sgu_in, s5_in, bg = _in_proj(
            x2, norm_mix_g[l].reshape(1, D_MODEL), w_in_all, l, tables, seq, cfg["tm_proj"])
        kc, vct = _compress(kvc, _prep_compress(cmp_pos_k[l], cmp_pos_v[l], cmp_k_w1[l], cmp_k_w2[l],
                                                cmp_v_w1[l], cmp_v_w2[l]), batch, seq)
        attn = _attention(qt, qrt, gt, kc, vct, ksel, kwin, vt, batch, seq, cfg["tq"], cfg["tk"])
        sgu = _sgu(sgu_in, sgu_norm_g[l], sgu_w[l], sgu_b[l], cfg["tm_sgu"])
        s5 = _s5(s5_in, _prep_s5(s5_a_re[l], s5_a_im[l], s5_log_step[l], s5_b_re[l], s5_b_im[l],
                                 s5_c_re[l], s5_c_im[l]),
                 s5_d[l], s5_glu_w[l], s5_glu_b[l], batch, seq, cfg["s5_steps"])
        x_mid, h, route = _merge(x2, attn, sgu, s5, bg, merge_ws, l,
                                 norm_ffn_g[l], router_group_w[l], router_group_b[l],
                                 router_expert_w[l], router_expert_b[l], cfg["tm_merge"])
        y_blocks, dest, x_buf = _experts(h, route, x_buf, expert_ws, l, cfg["bm"], cfg["tp_plan"], cfg["tm_scat"])
        x2 = _combine(x_mid, y_blocks, dest, route, norm_final_g, l == depth - 1, cfg["tm_comb"])
    return x2.reshape(batch, seq, D_MODEL)
```

```python
import functools
import math

import numpy as np
import jax
import jax.numpy as jnp
from jax import lax
from jax.experimental import pallas as pl
from jax.experimental.pallas import tpu as pltpu

F32 = jnp.float32
BF16 = jnp.bfloat16

D_MODEL = 1024
HEAD_DIM = 64
NSA_HEADS = 8
NSA_KV_GROUPS = 2
HEADS_PER_GROUP = NSA_HEADS // NSA_KV_GROUPS
GROUP_WIDTH = HEADS_PER_GROUP * HEAD_DIM
NSA_WIDTH = NSA_HEADS * HEAD_DIM
KV_WIDTH = NSA_KV_GROUPS * HEAD_DIM
ROPE_DIM = HEAD_DIM // 4
ROPE_HALF = ROPE_DIM // 2
ROPE_THETA = 500000.0
CMP_LEN = 32
CMP_STRIDE = 16
CMP_HIDDEN = 128
SLC_LEN = 64
SLC_SHIFT = int(math.log2(SLC_LEN))
SLC_TOP = 16
WINDOW = 512
FORCE_SCORE = 1.0e4
NEG_INF = -1.0e30
SGU_WIDTH = 256
SGU_GROUPS = 4
SGU_CHUNK = 128
S5_WIDTH = 256
S5_GROUP_CH = 16
S5_GROUPS = S5_WIDTH // S5_GROUP_CH
S5_STATE = 64
S5_LANES = S5_GROUPS * S5_STATE
MOE_GROUPS = 4
MOE_EXPERTS_PER_GROUP = 8
MOE_EXPERTS = MOE_GROUPS * MOE_EXPERTS_PER_GROUP
MOE_TOPK = 2
EXPERT_HIDDEN = 512
ROUTER_ROWS = 40
RMS_EPS = 1e-6
ATTN_SCALE = HEAD_DIM ** -0.5
LOG2_E = math.log2(math.e)

LANES = 128
SUBLANES = 8
VMEM_LIMIT_BYTES = 56 * 1024 * 1024

BF16_SUBLANES = 16
V_ROWS = HEAD_DIM + BF16_SUBLANES
GATE_COLS = HEADS_PER_GROUP * 3
GATE_ROWS = 16

_OFF_Q = 0
_OFF_KVC = _OFF_Q + NSA_WIDTH
_OFF_KS = _OFF_KVC + 2 * KV_WIDTH
_OFF_VS = _OFF_KS + KV_WIDTH
_OFF_KW = _OFF_VS + KV_WIDTH
_OFF_VW = _OFF_KW + KV_WIDTH
_OFF_GATE = _OFF_VW + KV_WIDTH
_OFF_SGU = _OFF_GATE + LANES
_OFF_S5 = _OFF_SGU + 2 * SGU_WIDTH
_OFF_BG = _OFF_S5 + S5_WIDTH
_IN_W = _OFF_BG + 3 * D_MODEL


def _cparams(*sem):
    return pltpu.CompilerParams(dimension_semantics=sem, vmem_limit_bytes=VMEM_LIMIT_BYTES)


def _gelu(x):
    return 0.5 * x * (1.0 + jnp.tanh(math.sqrt(2.0 / math.pi) * (x + 0.044715 * (x * x * x))))


def _sigmoid(x):
    return 1.0 / (1.0 + jnp.exp(-x))


def _dot(a, b):
    return jnp.dot(a, b, preferred_element_type=F32)


def _dot_nt(a, b):
    return lax.dot_general(a, b, (((1,), (1,)), ((), ())), preferred_element_type=F32)


def _split_bf16(x):
    hi = x.astype(BF16)
    lo = (x - hi.astype(F32)).astype(BF16)
    return hi, lo


def _rms(x, g):
    return x * lax.rsqrt(jnp.mean(x * x, axis=-1, keepdims=True) + RMS_EPS) * g


def _rope(x, c, s_lo, s_hi):
    n = x.shape[-1]
    return x * c + pltpu.roll(x, n - ROPE_HALF, 1) * s_lo + pltpu.roll(x, ROPE_HALF, 1) * s_hi


def _in_proj_kernel(x_ref, g_ref, w_ref, rope_ref,
                    qt_ref, qrt_ref, kvc_ref, ksel_ref, kwin_ref, vt_ref, gt_ref, sgu_ref, s5_ref, bg_ref, *, seq):
    tm = x_ref.shape[0]
    x = x_ref[...]
    hb = _rms(x, g_ref[...]).astype(BF16)

    def proj(off, width):
        return _dot(hb, w_ref[:, off:off + width])

    c, s_lo, s_hi = _expand_rope_table(rope_ref[...])
    rep = NSA_WIDTH // LANES
    q = proj(_OFF_Q, NSA_WIDTH)
    qt_ref[...] = (q * ATTN_SCALE).T
    qrot = _rope(q, jnp.concatenate([c] * rep, axis=1), jnp.concatenate([s_lo] * rep, axis=1),
                 jnp.concatenate([s_hi] * rep, axis=1))
    qrt_ref[...] = (qrot * (ATTN_SCALE * LOG2_E)).T.astype(BF16)
    kvc_ref[...] = proj(_OFF_KVC, 2 * KV_WIDTH)
    ks = _rope(proj(_OFF_KS, KV_WIDTH), c, s_lo, s_hi)
    kw = _rope(proj(_OFF_KW, KV_WIDTH), c, s_lo, s_hi)
    pos = lax.rem(pl.program_id(0) * tm, seq) + lax.broadcasted_iota(jnp.int32, (tm, 1), 0)
    blk = lax.shift_right_logical(pos, SLC_SHIFT)
    blk_onehot = jnp.where(lax.broadcasted_iota(jnp.int32, (1, HEAD_DIM), 1) == blk, 1.0, 0.0)
    for g in range(NSA_KV_GROUPS):
        cols = slice(g * HEAD_DIM, (g + 1) * HEAD_DIM)
        ksel_ref[g] = jnp.concatenate([ks[:, cols], blk_onehot], axis=1).astype(BF16)
        kwin_ref[g] = kw[:, cols].astype(BF16)
    aug_rows = lax.broadcasted_iota(jnp.int32, (V_ROWS - HEAD_DIM, LANES), 0)
    ones_row = jnp.where(aug_rows == 0, 1.0, 0.0).astype(BF16)
    for p, off in enumerate((_OFF_VS, _OFF_VW)):
        vt = proj(off, KV_WIDTH).T.astype(BF16)
        for g in range(NSA_KV_GROUPS):
            for ch in range(tm // LANES):
                idx = p * NSA_KV_GROUPS + g
                vt_ref[idx, ch, 0:HEAD_DIM, :] = vt[g * HEAD_DIM:(g + 1) * HEAD_DIM, ch * LANES:(ch + 1) * LANES]
                vt_ref[idx, ch, HEAD_DIM:V_ROWS, :] = ones_row
    gt_ref[...] = _sigmoid(proj(_OFF_GATE, LANES)).T[:NSA_KV_GROUPS * GATE_ROWS]
    sgu_ref[...] = proj(_OFF_SGU, 2 * SGU_WIDTH)
    s5_ref[...] = proj(_OFF_S5, S5_WIDTH)
    for k in range(3):
        bg_ref[:, k * D_MODEL:(k + 1) * D_MODEL] = _sigmoid(
            proj(_OFF_BG + k * D_MODEL, D_MODEL)).astype(BF16)


_SRC_GATE = NSA_WIDTH + 6 * KV_WIDTH
_SRC_REST = _SRC_GATE + 3 * NSA_HEADS


def _w_in_kernel(w_ref, o_ref):
    o_ref[:, 0:_OFF_GATE] = w_ref[:, 0:_OFF_GATE].astype(BF16)
    rows = w_ref.shape[0]
    gates = []
    for g in range(NSA_KV_GROUPS):
        gates += [w_ref[:, _SRC_GATE + g * GATE_COLS:_SRC_GATE + (g + 1) * GATE_COLS],
                  jnp.zeros((rows, GATE_ROWS - GATE_COLS), F32)]
    gates.append(jnp.zeros((rows, LANES - NSA_KV_GROUPS * GATE_ROWS), F32))
    o_ref[:, _OFF_GATE:_OFF_SGU] = jnp.concatenate(gates, axis=1).astype(BF16)
    o_ref[:, _OFF_SGU:_IN_W] = w_ref[:, _SRC_REST:_SRC_REST + (_IN_W - _OFF_SGU)].astype(BF16)


def _prep_in_proj(w_in):
    layers, d, width = w_in.shape
    tr = LANES
    return pl.pallas_call(
        _w_in_kernel, grid=(layers, d // tr),
        in_specs=[pl.BlockSpec((None, tr, width), lambda l, i: (l, i, 0))],
        out_specs=pl.BlockSpec((None, tr, _IN_W), lambda l, i: (l, i, 0)),
        out_shape=jax.ShapeDtypeStruct((layers, d, _IN_W), BF16),
        compiler_params=_cparams("parallel", "parallel"), name="w_in_layout",
    )(w_in)


def _layer_spec(a, layer):
    return pl.BlockSpec((None,) + a.shape[1:], lambda *_: (layer,) + (0,) * (a.ndim - 1))


def _rope_tables(positions):
    inv_freq = ROPE_THETA ** (-jnp.arange(ROPE_HALF, dtype=F32) / ROPE_HALF)
    ang = positions.astype(F32).reshape(-1, 1) * inv_freq
    cos, sin = jnp.cos(ang), jnp.sin(ang)
    t = ang.shape[0]
    return jnp.concatenate([cos, cos, -sin, sin, jnp.zeros((t, LANES - 2 * ROPE_DIM), F32)], axis=1)


def _expand_rope_table(tab):
    lane = lax.broadcasted_iota(jnp.int32, (1, LANES), 1)
    in_head = lane & (HEAD_DIM - 1)
    shifted = pltpu.roll(tab, LANES - ROPE_DIM, 1)
    first = lane < ROPE_DIM
    c = jnp.where(first, tab, 0.0)
    s_lo = jnp.where(lane < ROPE_HALF, shifted, 0.0)
    s_hi = jnp.where(first & (lane >= ROPE_HALF), shifted, 0.0)
    spread = lambda a: a + pltpu.roll(a, HEAD_DIM, 1)
    return spread(c) + jnp.where(in_head >= ROPE_DIM, 1.0, 0.0), spread(s_lo), spread(s_hi)


def _in_proj(x2, g, w_all, layer, tables, seq, tm):
    t = x2.shape[0]
    row = lambda width: pl.BlockSpec((tm, width), lambda i: (i, 0))
    col = lambda height: pl.BlockSpec((height, tm), lambda i: (0, i))
    full = lambda a: pl.BlockSpec(a.shape, lambda i: (0,) * a.ndim)
    n_kv = 2 * NSA_KV_GROUPS
    out_shape = (
        jax.ShapeDtypeStruct((NSA_WIDTH, t), F32),
        jax.ShapeDtypeStruct((NSA_WIDTH, t), BF16),
        jax.ShapeDtypeStruct((t, 2 * KV_WIDTH), F32),
        jax.ShapeDtypeStruct((NSA_KV_GROUPS, t, 2 * HEAD_DIM), BF16),
        jax.ShapeDtypeStruct((NSA_KV_GROUPS, t, HEAD_DIM), BF16),
        jax.ShapeDtypeStruct((n_kv, t // LANES, V_ROWS, LANES), BF16),
        jax.ShapeDtypeStruct((NSA_KV_GROUPS * GATE_ROWS, t), F32),
        jax.ShapeDtypeStruct((t, 2 * SGU_WIDTH), F32),
        jax.ShapeDtypeStruct((seq, (t // seq) * S5_WIDTH), F32),
        jax.ShapeDtypeStruct((t, 3 * D_MODEL), BF16),
    )
    seq_tiles = seq // tm
    time_major = pl.BlockSpec((tm, S5_WIDTH), lambda i: (i % seq_tiles, i // seq_tiles))
    out_specs = (col(NSA_WIDTH), col(NSA_WIDTH), row(2 * KV_WIDTH),
                 pl.BlockSpec((NSA_KV_GROUPS, tm, 2 * HEAD_DIM), lambda i: (0, i, 0)),
                 pl.BlockSpec((NSA_KV_GROUPS, tm, HEAD_DIM), lambda i: (0, i, 0)),
                 pl.BlockSpec((n_kv, tm // LANES, V_ROWS, LANES), lambda i: (0, i, 0, 0)),
                 col(NSA_KV_GROUPS * GATE_ROWS), row(2 * SGU_WIDTH), time_major, row(3 * D_MODEL))
    return pl.pallas_call(
        functools.partial(_in_proj_kernel, seq=seq), grid=(t // tm,),
        in_specs=[row(D_MODEL), full(g), _layer_spec(w_all, layer), row(LANES)],
        out_specs=out_specs, out_shape=out_shape,
        compiler_params=_cparams("parallel"), name="in_proj",
    )(x2, g, w_all, tables)


def _compress_kernel(rk_ref, rv_ref, wa_ref, wb_ref, pos_ref, w1_ref, w2_ref, w2t_ref, kc_ref, vct_ref):
    n = rk_ref.shape[0] // CMP_STRIDE
    a = jnp.zeros((n, wa_ref.shape[1]), F32)
    b = jnp.zeros((n, wb_ref.shape[1]), F32)
    for l in range(CMP_STRIDE):
        for half, ref in enumerate((rk_ref, rv_ref)):
            rows_l = ref[pl.ds(l, n, stride=CMP_STRIDE), :].astype(BF16)
            w_rows = slice((2 * l + half) * KV_WIDTH, (2 * l + half + 1) * KV_WIDTH)
            a = a + _dot(rows_l, wa_ref[w_rows, :])
            b = b + _dot(rows_l, wb_ref[w_rows, :])
    b = pltpu.roll(b, n - 1, 0)
    rows = lax.broadcasted_iota(jnp.int32, b.shape, 0)
    b = jnp.where(rows == n - 1, 0.0, b)
    for m in range(2 * NSA_KV_GROUPS):
        typ, g = divmod(m, NSA_KV_GROUPS)
        bias = _dot(pos_ref[typ].astype(BF16), w1_ref[typ])
        hid = a[:, m * CMP_HIDDEN:(m + 1) * CMP_HIDDEN] + b[:, m * CMP_HIDDEN:(m + 1) * CMP_HIDDEN] + bias
        act = _gelu(hid).astype(BF16)
        if typ == 0:
            kc_ref[0, g] = _dot(act, w2_ref[...])
        else:
            vct_ref[0, g] = _dot_nt(w2t_ref[...], act)


def _prep_compress(cmp_pos_k, cmp_pos_v, k_w1, k_w2, v_w1, v_w2):
    streams = 2 * NSA_KV_GROUPS
    w1 = jnp.stack([k_w1, v_w1]).reshape(2, CMP_LEN, HEAD_DIM, CMP_HIDDEN)

    def expand(half):
        out = jnp.zeros((CMP_STRIDE, streams, HEAD_DIM, streams, CMP_HIDDEN), F32)
        for m in range(streams):
            out = out.at[:, m, :, m, :].set(w1[m // NSA_KV_GROUPS, half * CMP_STRIDE:(half + 1) * CMP_STRIDE])
        return out.reshape(CMP_STRIDE * streams * HEAD_DIM, streams * CMP_HIDDEN).astype(BF16)

    pos = jnp.stack([cmp_pos_k, cmp_pos_v]).reshape(2, 1, CMP_LEN * HEAD_DIM)
    return (expand(0), expand(1), pos, jnp.stack([k_w1, v_w1]).astype(BF16),
            k_w2.astype(BF16), v_w2.T.astype(BF16))


def _compress(kvc, prep, batch, seq):
    n = seq // CMP_STRIDE
    full = lambda a: pl.BlockSpec(a.shape, lambda b: (0,) * a.ndim)
    return pl.pallas_call(
        _compress_kernel, grid=(batch,),
        in_specs=[pl.BlockSpec((seq, KV_WIDTH), lambda b: (b, 0)), pl.BlockSpec((seq, KV_WIDTH), lambda b: (b, 1))]
                 + [full(a) for a in prep],
        out_specs=(pl.BlockSpec((1, NSA_KV_GROUPS, n, HEAD_DIM), lambda b: (b, 0, 0, 0)),
                   pl.BlockSpec((1, NSA_KV_GROUPS, HEAD_DIM, n), lambda b: (b, 0, 0, 0))),
        out_shape=(jax.ShapeDtypeStruct((batch, NSA_KV_GROUPS, n, HEAD_DIM), F32),
                   jax.ShapeDtypeStruct((batch, NSA_KV_GROUPS, HEAD_DIM, n), F32)),
        compiler_params=_cparams("parallel"), name="compress_kv",
    )(kvc, kvc, *prep)


def _attn_kernel(qt_ref, qrt_ref, gt_ref, kc_ref, vct_ref, ks_ref, vst_ref, kw_ref, vwt_ref, ovt_ref,
                 o_ref, qa_sc, m_sc, acc_sc, oc_sc, s_sc, p_sc, a_sc, *, tq, tk, seq):
    n_cmp = kc_ref.shape[2]
    n_slc = seq // SLC_LEN
    heads = HEADS_PER_GROUP
    t0 = pl.program_id(2) * tq
    t_lane = t0 + lax.broadcasted_iota(jnp.int32, (1, tq), 1)

    def lane_tile(a):
        return jnp.concatenate([a] * heads, axis=1)

    def head_cols(ref):
        return jnp.concatenate([ref[r * HEAD_DIM:(r + 1) * HEAD_DIM, :] for r in range(heads)], axis=1)

    q_hi, q_lo = _split_bf16(head_cols(qt_ref))
    qr = head_cols(qrt_ref)

    def compressed_and_select(nc, ns, all_selected, forced_distinct):
        kc_hi, kc_lo = _split_bf16(kc_ref[0, 0, 0:nc, :])
        s = _dot(kc_hi, q_hi) + _dot(kc_lo, q_hi) + _dot(kc_hi, q_lo)
        cmp_end = lax.broadcasted_iota(jnp.int32, (nc, 1), 0) * CMP_STRIDE + (CMP_LEN - 1)
        m_c = cmp_end <= lane_tile(t_lane)
        s = jnp.where(m_c, s, NEG_INF)
        e = jnp.where(m_c, jnp.exp(s - jnp.max(s, axis=0, keepdims=True)), 0.0)
        p = e / jnp.maximum(jnp.sum(e, axis=0, keepdims=True), 1e-30)
        oc_sc[...] = _dot(vct_ref[0, 0, :, 0:nc].astype(BF16), p.astype(BF16))

        j_idx = lax.broadcasted_iota(jnp.int32, (ns, 1), 0)
        blk_t = lax.shift_right_logical(t_lane, SLC_SHIFT)
        causal = j_idx <= blk_t
        if all_selected:
            picked = causal
        else:
            p_sum = p[:, 0:tq]
            for r in range(1, heads):
                p_sum = p_sum + p[:, r * tq:(r + 1) * tq]
            ps_hi, ps_lo = _split_bf16(p_sum)
            imp = _dot(ovt_ref[0:ns, 0:nc], ps_hi) + _dot(ovt_ref[0:ns, 0:nc], ps_lo)
            forced = (j_idx == 0) | (j_idx == blk_t) | (j_idx == blk_t - 1)
            j_f32 = j_idx.astype(F32)
            if forced_distinct:
                taken = jnp.where(forced, 1.0, 0.0)
                imp = jnp.where(forced, -jnp.inf, jnp.where(causal, imp, NEG_INF))
                rounds = SLC_TOP - 3
            else:
                taken = jnp.zeros((ns, tq), F32)
                imp = jnp.where(causal, jnp.where(forced, FORCE_SCORE, imp), NEG_INF)
                rounds = min(SLC_TOP, ns)
            for _ in range(rounds):
                best = jnp.max(imp, axis=0, keepdims=True)
                first = jnp.min(jnp.where(imp == best, j_f32, float(ns)), axis=0, keepdims=True)
                hit = j_f32 == first
                taken = jnp.where(hit, 1.0, taken)
                imp = jnp.where(hit, -jnp.inf, imp)
            picked = (taken > 0.5) & causal
        bias = jnp.where(picked, 0.0, NEG_INF)
        if ns < HEAD_DIM:
            bias = jnp.concatenate([bias, jnp.zeros((HEAD_DIM - ns, tq), F32)], axis=0)
        qa_sc[...] = jnp.concatenate([qr, lane_tile(bias.astype(BF16))], axis=0)

    nq = seq // tq
    qi = pl.program_id(2)

    def sizes(tiles):
        nc = min(n_cmp, -(-max(tiles * tq // CMP_STRIDE, 1) // LANES) * LANES)
        ns = min(n_slc, -(-max(tiles * tq // SLC_LEN, 1) // SUBLANES) * SUBLANES)
        return nc, ns

    bounds = sorted({0, min(nq, SLC_TOP * SLC_LEN // tq), min(nq, max(nq // 2, 1)), nq})
    for lo, hi in zip(bounds[:-1], bounds[1:]):
        nc, ns = sizes(hi)
        variant = functools.partial(compressed_and_select, nc, ns, hi * tq <= SLC_TOP * SLC_LEN,
                                    lo * tq >= 2 * SLC_LEN)
        pl.when((qi >= lo) & (qi < hi))(variant)

    m_sc[...] = jnp.full(m_sc.shape, NEG_INF, F32)
    acc_sc[...] = jnp.zeros(acc_sc.shape, F32)
    p_sc[1] = jnp.zeros(p_sc.shape[1:], BF16)
    a_sc[1] = jnp.ones(a_sc.shape[1:], F32)
    lanes_per_tile = tk // LANES

    def scores(j, slot):
        k0 = j * tk if isinstance(j, int) else pl.multiple_of(j * tk, tk)
        s_sc[slot] = _dot(ks_ref[0, pl.ds(k0, tk), :], qa_sc[...])

    def softmax(j, slot, diagonal):
        s = s_sc[slot]
        if diagonal:
            key_pos = j * tk + lax.broadcasted_iota(jnp.int32, (tk, 1), 0)
            s = s + lane_tile(jnp.where(key_pos <= t_lane, 0.0, NEG_INF))
        m_old = m_sc[...]
        m_new = jnp.maximum(m_old, jnp.max(s, axis=0, keepdims=True))
        m_sc[...] = m_new
        a_sc[slot] = jnp.exp2(m_old - m_new)
        p_sc[slot] = jnp.exp2(s - m_new).astype(BF16)

    def accumulate(j, slot):
        tile = jnp.maximum(j, 0) * lanes_per_tile
        v_t = jnp.concatenate([vst_ref[0, tile + c] for c in range(lanes_per_tile)], axis=1)
        acc_sc[...] = a_sc[slot] * acc_sc[...] + _dot(v_t, p_sc[slot])

    def pipe_step(i, i_is_odd):
        a, b = (1, 0) if i_is_odd else (0, 1)
        accumulate(i - 2, a)
        softmax(i - 1, b, False)
        scores(i, a)

    n_kv = (t0 + tq + tk - 1) // tk
    scores(0, 0)

    def two_steps(u, carry):
        pipe_step(2 * u + 1, True)
        pipe_step(2 * u + 2, False)
        return carry

    lax.fori_loop(0, (n_kv - 1) // 2, two_steps, 0)

    @pl.when(((n_kv - 1) & 1) == 1)
    def _():
        pipe_step(n_kv - 1, True)

    last = (n_kv - 1) & 1
    accumulate(n_kv - 2, 1 - last)
    softmax(n_kv - 1, last, True)
    accumulate(n_kv - 1, last)
    o_slc = acc_sc[0:HEAD_DIM, :] / acc_sc[HEAD_DIM:HEAD_DIM + 1, :]
    o_cmp = oc_sc[...]

    slab = min(WINDOW + tq, seq)
    st = pl.multiple_of(jnp.clip(t0 - WINDOW, 0, seq - slab), LANES)
    s = _dot(kw_ref[0, pl.ds(st, slab), :], qa_sc[0:HEAD_DIM, :])
    diff = t_lane - (st + lax.broadcasted_iota(jnp.int32, (slab, 1), 0))
    s = s + lane_tile(jnp.where((diff >= 0) & (diff < WINDOW), 0.0, NEG_INF))
    p = jnp.exp2(s - jnp.max(s, axis=0, keepdims=True))
    st_tile = st // LANES
    vw_t = jnp.concatenate([vwt_ref[0, st_tile + c] for c in range(slab // LANES)], axis=1)
    pv = _dot(vw_t, p.astype(BF16))
    o_win = pv[0:HEAD_DIM] / pv[HEAD_DIM:HEAD_DIM + 1]

    gates = gt_ref[...]
    outs = []
    for r in range(heads):
        cols = slice(r * tq, (r + 1) * tq)
        outs.append(gates[3 * r:3 * r + 1, :] * o_cmp[:, cols] + gates[3 * r + 1:3 * r + 2, :] * o_slc[:, cols]
                    + gates[3 * r + 2:3 * r + 3, :] * o_win[:, cols])
    o_ref[...] = jnp.concatenate(outs, axis=0).T.astype(o_ref.dtype)


def _overlap_matrix_t(n_cmp, n_slc):
    c0 = np.arange(n_cmp)[None, :] * CMP_STRIDE
    s0 = np.arange(n_slc)[:, None] * SLC_LEN
    ov = np.clip(np.minimum(c0 + CMP_LEN, s0 + SLC_LEN) - np.maximum(c0, s0), 0, None) / CMP_LEN
    ov[:, n_cmp - 1] = 0.0
    return jnp.asarray(ov, BF16)


def _attention(qt, qrt, gt, kc, vct, ksel, kwin, vt, batch, seq, tq, tk):
    assert tk % tq == 0 and seq // SLC_LEN <= HEAD_DIM
    n_cmp = seq // CMP_STRIDE
    n_slc = seq // SLC_LEN
    nq = seq // tq
    ksel = ksel.reshape(NSA_KV_GROUPS, batch, seq, 2 * HEAD_DIM)
    kwin = kwin.reshape(NSA_KV_GROUPS, batch, seq, HEAD_DIM)
    vt = vt.reshape(2 * NSA_KV_GROUPS, batch, seq // LANES, V_ROWS, LANES)
    ovt = _overlap_matrix_t(n_cmp, n_slc)
    qspec = pl.BlockSpec((GROUP_WIDTH, tq), lambda b, g, i: (g, b * nq + i))
    k_spec = lambda width: pl.BlockSpec((None, 1, seq, width), lambda b, g, i: (g, b, 0, 0))
    v_spec = lambda p: pl.BlockSpec((None, 1, seq // LANES, V_ROWS, LANES),
                                    lambda b, g, i: (p * NSA_KV_GROUPS + g, b, 0, 0, 0))
    kernel = functools.partial(_attn_kernel, tq=tq, tk=tk, seq=seq)
    return pl.pallas_call(
        kernel, grid=(batch, NSA_KV_GROUPS, nq),
        in_specs=[qspec, qspec, pl.BlockSpec((GATE_ROWS, tq), lambda b, g, i: (g, b * nq + i)),
                  pl.BlockSpec((1, 1, n_cmp, HEAD_DIM), lambda b, g, i: (b, g, 0, 0)),
                  pl.BlockSpec((1, 1, HEAD_DIM, n_cmp), lambda b, g, i: (b, g, 0, 0)),
                  k_spec(2 * HEAD_DIM), v_spec(0), k_spec(HEAD_DIM), v_spec(1),
                  pl.BlockSpec(ovt.shape, lambda b, g, i: (0, 0))],
        out_specs=pl.BlockSpec((tq, GROUP_WIDTH), lambda b, g, i: (b * nq + i, g)),
        out_shape=jax.ShapeDtypeStruct((batch * seq, NSA_WIDTH), BF16),
        scratch_shapes=[pltpu.VMEM((2 * HEAD_DIM, HEADS_PER_GROUP * tq), BF16),
                        pltpu.VMEM((1, HEADS_PER_GROUP * tq), F32),
                        pltpu.VMEM((V_ROWS, HEADS_PER_GROUP * tq), F32),
                        pltpu.VMEM((HEAD_DIM, HEADS_PER_GROUP * tq), F32),
                        pltpu.VMEM((2, tk, HEADS_PER_GROUP * tq), F32),
                        pltpu.VMEM((2, tk, HEADS_PER_GROUP * tq), BF16),
                        pltpu.VMEM((2, 1, HEADS_PER_GROUP * tq), F32)],
        compiler_params=_cparams("parallel", "parallel", "arbitrary"), name="nsa_attention",
    )(qt, qrt, gt, kc, vct, ksel, vt, kwin, vt, ovt)


def _sgu_kernel(uv_ref, g_ref, w_ref, b_ref, o_ref, *, chunks):
    z = _gelu(uv_ref[...])
    u = z[:, :SGU_WIDTH]
    v = _rms(z[:, SGU_WIDTH:], g_ref[...])
    gw = SGU_WIDTH // SGU_GROUPS
    rows = lax.broadcasted_iota(jnp.int32, (SGU_CHUNK, SGU_GROUPS * SGU_CHUNK), 0)
    cols = lax.broadcasted_iota(jnp.int32, (SGU_CHUNK, SGU_GROUPS * SGU_CHUNK), 1)
    w = jnp.where((cols & (SGU_CHUNK - 1)) <= rows, w_ref[...], 0.0).astype(BF16)
    grp_r = lax.broadcasted_iota(jnp.int32, (SGU_GROUPS * SGU_CHUNK, SGU_WIDTH), 0) // SGU_CHUNK
    grp_c = lax.broadcasted_iota(jnp.int32, (SGU_GROUPS * SGU_CHUNK, SGU_WIDTH), 1) // gw
    for c in range(chunks):
        vc = v[c * SGU_CHUNK:(c + 1) * SGU_CHUNK].astype(BF16)
        v_bd = jnp.where(grp_r == grp_c, jnp.concatenate([vc] * SGU_GROUPS, axis=0), jnp.zeros((), BF16))
        mixed = _dot(w, v_bd) + b_ref[...]
        o_ref[c * SGU_CHUNK:(c + 1) * SGU_CHUNK, :] = (u[c * SGU_CHUNK:(c + 1) * SGU_CHUNK] * mixed).astype(o_ref.dtype)


def _sgu(sgu_in, norm_g, w_s, b_s, tm):
    t = sgu_in.shape[0]
    w_cat = jnp.transpose(w_s, (1, 0, 2)).reshape(SGU_CHUNK, SGU_GROUPS * SGU_CHUNK)
    bias = jnp.repeat(b_s.T, SGU_WIDTH // SGU_GROUPS, axis=1)
    g = norm_g.reshape(1, SGU_WIDTH)
    full = lambda a: pl.BlockSpec(a.shape, lambda i: (0,) * a.ndim)
    return pl.pallas_call(
        functools.partial(_sgu_kernel, chunks=tm // SGU_CHUNK), grid=(t // tm,),
        in_specs=[pl.BlockSpec((tm, 2 * SGU_WIDTH), lambda i: (i, 0)), full(g), full(w_cat), full(bias)],
        out_specs=pl.BlockSpec((tm, SGU_WIDTH), lambda i: (i, 0)),
        out_shape=jax.ShapeDtypeStruct((t, SGU_WIDTH), BF16),
        compiler_params=_cparams("parallel"), name="sgu",
    )(sgu_in, g, w_cat, bias)


def _s5_kernel(x_ref, bre_ref, bim_ref, are_ref, aim_ref, cre_ref, cim_ref, d_ref, gw_ref, gb_ref,
               o_ref, hre_sc, him_sc, ure_sc, uim_sc, *, steps, batch):
    @pl.when(pl.program_id(0) == 0)
    def _():
        hre_sc[...] = jnp.zeros(hre_sc.shape, F32)
        him_sc[...] = jnp.zeros(him_sc.shape, F32)

    x = x_ref[...]
    xb = x.astype(BF16)
    ure_sc[...] = _dot(xb, bre_ref[...])
    uim_sc[...] = _dot(xb, bim_ref[...])
    a_re = jnp.broadcast_to(are_ref[...], (batch, S5_LANES))
    a_im = jnp.broadcast_to(aim_ref[...], (batch, S5_LANES))

    def step(t, carry):
        h_re, h_im = carry
        rows = pl.ds(pl.multiple_of(t * batch, batch), batch)
        n_re = a_re * h_re - a_im * h_im + ure_sc[rows, :]
        n_im = a_re * h_im + a_im * h_re + uim_sc[rows, :]
        ure_sc[rows, :] = n_re
        uim_sc[rows, :] = n_im
        return n_re, n_im

    h_re, h_im = lax.fori_loop(0, steps, step, (hre_sc[...], him_sc[...]))
    hre_sc[...] = h_re
    him_sc[...] = h_im
    y = _dot(ure_sc[...].astype(BF16), cre_ref[...]) - _dot(uim_sc[...].astype(BF16), cim_ref[...])
    y = _gelu(y + d_ref[...] * x)
    y = y * _sigmoid(_dot(y.astype(BF16), gw_ref[...]) + gb_ref[...])
    o_ref[...] = y.astype(o_ref.dtype)


def _block_diag(blocks):
    g, r, c = blocks.shape
    eye = jnp.eye(g, dtype=blocks.dtype)
    return (blocks[:, :, None, :] * eye[:, None, :, None]).reshape(g * r, g * c)


def _prep_s5(a_re, a_im, log_step, b_re, b_im, c_re, c_im):
    step = jnp.exp(log_step)[:, None]
    mag = jnp.exp(a_re * step)
    abar_re, abar_im = mag * jnp.cos(a_im * step), mag * jnp.sin(a_im * step)
    den = a_re * a_re + a_im * a_im
    nr, ni = abar_re - 1.0, abar_im
    coef_re = (nr * a_re + ni * a_im) / den
    coef_im = (ni * a_re - nr * a_im) / den
    bbar_re = coef_re[..., None] * b_re - coef_im[..., None] * b_im
    bbar_im = coef_re[..., None] * b_im + coef_im[..., None] * b_re
    to_in = lambda b: _block_diag(jnp.transpose(b, (0, 2, 1))).astype(BF16)
    to_out = lambda c: _block_diag(jnp.transpose(c, (0, 2, 1))).astype(BF16)
    return (to_in(bbar_re), to_in(bbar_im), abar_re.reshape(1, S5_LANES), abar_im.reshape(1, S5_LANES),
            to_out(c_re), to_out(c_im))


def _s5(xs, prep, d, glu_w, glu_b, batch, seq, steps):
    bre, bim, are, aim, cre, cim = prep
    x_tm = xs.reshape(seq * batch, S5_WIDTH)
    rows = steps * batch
    d2, gb2, gwb = d.reshape(1, S5_WIDTH), glu_b.reshape(1, S5_WIDTH), glu_w.astype(BF16)
    full = lambda a: pl.BlockSpec(a.shape, lambda i: (0,) * a.ndim)
    y = pl.pallas_call(
        functools.partial(_s5_kernel, steps=steps, batch=batch), grid=(seq // steps,),
        in_specs=[pl.BlockSpec((rows, S5_WIDTH), lambda i: (i, 0)), full(bre), full(bim), full(are), full(aim),
                  full(cre), full(cim), full(d2), full(gwb), full(gb2)],
        out_specs=pl.BlockSpec((rows, S5_WIDTH), lambda i: (i, 0)),
        out_shape=jax.ShapeDtypeStruct((seq * batch, S5_WIDTH), BF16),
        scratch_shapes=[pltpu.VMEM((batch, S5_LANES), F32), pltpu.VMEM((batch, S5_LANES), F32),
                        pltpu.VMEM((rows, S5_LANES), F32), pltpu.VMEM((rows, S5_LANES), F32)],
        compiler_params=_cparams("arbitrary"), name="s5_scan",
    )(x_tm, bre, bim, are, aim, cre, cim, d2, gwb, gb2)
    return y.reshape(seq, batch * S5_WIDTH)


def _merge_kernel(x_ref, a_ref, b_ref, c_ref, bg_ref, wa_ref, wb_ref, wc_ref, wm_ref, g_ref,
                  wrh_ref, wrl_ref, br_ref, xo_ref, h_ref, route_ref):
    y_a = _dot(a_ref[...], wa_ref[...])
    y_b = _dot(b_ref[...], wb_ref[...])
    y_c = _dot(c_ref[...], wc_ref[...])
    merged = (bg_ref[:, :D_MODEL].astype(F32) * y_a + bg_ref[:, D_MODEL:2 * D_MODEL].astype(F32) * y_b
              + bg_ref[:, 2 * D_MODEL:].astype(F32) * y_c)
    x = x_ref[...] + _dot(merged.astype(BF16), wm_ref[...])
    xo_ref[...] = x
    h = _rms(x, g_ref[...])
    h_ref[...] = h

    h_hi, h_lo = _split_bf16(h)
    logits = (_dot_nt(wrh_ref[...], h_hi) + _dot_nt(wrl_ref[...], h_hi) + _dot_nt(wrh_ref[...], h_lo))
    logits = logits[0:ROUTER_ROWS] + br_ref[...]
    row = lax.broadcasted_iota(jnp.int32, (ROUTER_ROWS, 1), 0).astype(F32)
    big = float(ROUTER_ROWS)
    is_grp = row < MOE_GROUPS
    gl = jnp.where(is_grp, logits, -jnp.inf)
    gmax = jnp.max(gl, axis=0, keepdims=True)
    grp = jnp.min(jnp.where(gl == gmax, row, big), axis=0, keepdims=True)
    grp_w = 1.0 / jnp.sum(jnp.where(is_grp, jnp.exp(logits - gmax), 0.0), axis=0, keepdims=True)
    e_row = row - MOE_GROUPS
    in_grp = (e_row >= 0) & (jnp.floor(e_row * (1.0 / MOE_EXPERTS_PER_GROUP)) == grp)
    emax = jnp.max(jnp.where(in_grp, logits, -jnp.inf), axis=0, keepdims=True)
    ee = jnp.where(in_grp, jnp.exp(logits - emax), 0.0)
    prob = jnp.where(in_grp, ee / jnp.sum(ee, axis=0, keepdims=True), -1.0)
    p1 = jnp.max(prob, axis=0, keepdims=True)
    j1 = jnp.min(jnp.where(prob == p1, row, big), axis=0, keepdims=True)
    prob2 = jnp.where(row == j1, -1.0, prob)
    p2 = jnp.max(prob2, axis=0, keepdims=True)
    j2 = jnp.min(jnp.where(prob2 == p2, row, big), axis=0, keepdims=True)
    psum = p1 + p2
    out_row = lax.broadcasted_iota(jnp.int32, (LANES, 1), 0)
    route_t = jnp.where(out_row == 0, grp_w * p1 / psum, 0.0)
    route_t = jnp.where(out_row == 1, grp_w * p2 / psum, route_t)
    route_t = jnp.where(out_row == 2, j1 - MOE_GROUPS, route_t)
    route_t = jnp.where(out_row == 3, j2 - MOE_GROUPS, route_t)
    route_ref[...] = route_t.T


def _merge(x2, attn, sgu, s5, bg, ws, layer, norm_g, rgw, rgb, rew, reb, tm):
    t = x2.shape[0]
    seq_tiles = s5.shape[0] // tm
    wr = jnp.concatenate([rgw, rew, jnp.zeros((D_MODEL, LANES - MOE_GROUPS - MOE_EXPERTS), F32)], axis=1).T
    wr_hi = wr.astype(BF16)
    wr_lo = (wr - wr_hi.astype(F32)).astype(BF16)
    br = jnp.concatenate([rgb, reb, jnp.zeros((ROUTER_ROWS - MOE_GROUPS - MOE_EXPERTS,), F32)]).reshape(ROUTER_ROWS, 1)
    g = norm_g.reshape(1, D_MODEL)
    row = lambda width: pl.BlockSpec((tm, width), lambda i: (i, 0))
    full = lambda a: pl.BlockSpec(a.shape, lambda i: (0,) * a.ndim)
    return pl.pallas_call(
        _merge_kernel, grid=(t // tm,),
        in_specs=[row(D_MODEL), row(NSA_WIDTH), row(SGU_WIDTH),
                  pl.BlockSpec((tm, S5_WIDTH), lambda i: (i % seq_tiles, i // seq_tiles)),
                  row(3 * D_MODEL)]
                 + [_layer_spec(w, layer) for w in ws] + [full(g), full(wr_hi), full(wr_lo), full(br)],
        out_specs=(row(D_MODEL), row(D_MODEL), row(LANES)),
        out_shape=(jax.ShapeDtypeStruct((t, D_MODEL), F32), jax.ShapeDtypeStruct((t, D_MODEL), F32),
                   jax.ShapeDtypeStruct((t, LANES), F32)),
        compiler_params=_cparams("parallel"), name="merge_route",
    )(x2, attn, sgu, s5, bg, *ws, g, wr_hi, wr_lo, br)


def _start_row_gather(src_hbm, rows_of, dst_buf, sem, n):
    for r in range(n):
        pltpu.make_async_copy(src_hbm.at[pl.ds(rows_of(r), 1)], dst_buf.at[pl.ds(r, 1)], sem).start(priority=r % 2)


def _wait_row_gather(src_hbm, dst_buf, sem, n):
    pltpu.make_async_copy(src_hbm.at[pl.ds(0, n)], dst_buf, sem).wait()


def _plan_kernel(route_ref, dest_ref, cnt_ref, carry_sc, *, bm):
    phase, i = pl.program_id(0), pl.program_id(1)
    tp = route_ref.shape[0]
    lane = lax.broadcasted_iota(jnp.int32, (1, LANES), 1)
    route = route_ref[...]
    oh0 = jnp.where(lane == route[:, 2:3].astype(jnp.int32), 1.0, 0.0)
    oh1 = jnp.where(lane == route[:, 3:4].astype(jnp.int32), 1.0, 0.0)
    both = oh0 + oh1
    col_sum = jnp.broadcast_to(jnp.sum(both, axis=0, keepdims=True), carry_sc.shape)

    @pl.when((phase == 0) & (i == 0))
    def _():
        carry_sc[...] = jnp.zeros(carry_sc.shape, F32)

    @pl.when(phase == 0)
    def _():
        carry_sc[...] = carry_sc[...] + col_sum

    @pl.when((phase == 1) & (i == 0))
    def _():
        counts = carry_sc[...]
        cnt_ref[...] = counts
        padded = jnp.floor((counts + (bm - 1)) * (1.0 / bm)) * bm
        incl = padded
        shift = 1
        while shift < LANES:
            incl = incl + jnp.where(lane >= shift, pltpu.roll(incl, shift, 1), 0.0)
            shift *= 2
        carry_sc[...] = incl - padded

    @pl.when(phase == 1)
    def _():
        r_idx = lax.broadcasted_iota(jnp.int32, (tp, tp), 0)
        c_idx = lax.broadcasted_iota(jnp.int32, (tp, tp), 1)
        earlier = jnp.where(c_idx < r_idx, 1.0, 0.0).astype(BF16)
        base = _dot(earlier, both.astype(BF16)) + carry_sc[0:1, :]
        d0 = jnp.sum(oh0 * base, axis=1, keepdims=True)
        d1 = jnp.sum(oh1 * base, axis=1, keepdims=True)
        dest_ref[...] = jnp.where(lane == 0, d0, jnp.where(lane == 1, d1, 0.0)).astype(jnp.int32)
        carry_sc[...] = carry_sc[...] + col_sum


def _dispatch_plan(route, bm, tp):
    t = route.shape[0]
    steps = t // tp
    dest, counts = pl.pallas_call(
        functools.partial(_plan_kernel, bm=bm), grid=(2, steps),
        in_specs=[pl.BlockSpec((tp, LANES), lambda p, i: (i, 0))],
        out_specs=(pl.BlockSpec((tp, LANES), lambda p, i: (i * p, 0)),
                   pl.BlockSpec((SUBLANES, LANES), lambda p, i: (0, 0))),
        out_shape=(jax.ShapeDtypeStruct((t, LANES), jnp.int32), jax.ShapeDtypeStruct((SUBLANES, LANES), F32)),
        scratch_shapes=[pltpu.VMEM((SUBLANES, LANES), F32)],
        compiler_params=_cparams("arbitrary", "arbitrary"), name="moe_plan",
    )(route)
    nb = t * MOE_TOPK // bm + MOE_EXPERTS
    nblk = (counts[0, :MOE_EXPERTS].astype(jnp.int32) + (bm - 1)) // bm
    blk_end = jnp.cumsum(nblk)
    blk_exp = jnp.sum((blk_end[None, :] <= jnp.arange(nb, dtype=jnp.int32)[:, None]).astype(jnp.int32), axis=1)
    return dest[:, :MOE_TOPK], jnp.minimum(blk_exp, MOE_EXPERTS - 1), nb


def _scatter_kernel(rows_ref, h_ref, zero_hbm, x_hbm, sem, *, tm):
    del zero_hbm
    n = MOE_TOPK * tm
    for r in range(n):
        pltpu.make_async_copy(h_ref.at[pl.ds(r // MOE_TOPK, 1)], x_hbm.at[pl.ds(rows_ref[0, 0, r], 1)],
                              sem).start(priority=r % 2)
    for _ in range(MOE_TOPK):
        pltpu.make_async_copy(h_ref, x_hbm.at[pl.ds(0, tm)], sem).wait()


def _scatter_rows(h, dest, x_buf, tm):
    t = h.shape[0]
    steps = t // tm
    rows = dest.reshape(steps, 1, MOE_TOPK * tm)
    return pl.pallas_call(
        functools.partial(_scatter_kernel, tm=tm), grid=(steps,),
        in_specs=[pl.BlockSpec((1, 1, MOE_TOPK * tm), lambda i: (i, 0, 0), memory_space=pltpu.SMEM),
                  pl.BlockSpec((tm, D_MODEL), lambda i: (i, 0)),
                  pl.BlockSpec(memory_space=pl.ANY)],
        out_specs=pl.BlockSpec(memory_space=pl.ANY),
        out_shape=jax.ShapeDtypeStruct(x_buf.shape, F32),
        scratch_shapes=[pltpu.SemaphoreType.DMA(())],
        input_output_aliases={2: 0},
        compiler_params=_cparams("arbitrary"), name="moe_dispatch",
    )(rows, h, x_buf)


def _moe_kernel(blk_exp_ref, x_ref, wg_ref, wu_ref, wd_ref, y_ref, wg_sc, wu_sc, wd_sc):
    i = pl.program_id(0)
    new_expert = (i == 0) | (blk_exp_ref[i] != blk_exp_ref[jnp.maximum(i - 1, 0)])

    @pl.when(new_expert)
    def _():
        wg_sc[...] = wg_ref[0].astype(BF16)
        wu_sc[...] = wu_ref[0].astype(BF16)
        wd_sc[...] = wd_ref[0].astype(BF16)

    xb = x_ref[...].astype(BF16)
    gate = _dot(xb, wg_sc[...])
    hid = gate * _sigmoid(gate) * _dot(xb, wu_sc[...])
    y_ref[...] = _dot(hid.astype(BF16), wd_sc[...])


def _experts(h, route, x_buf, expert_ws, layer, bm, tp, tm):
    dest, blk_exp, nb = _dispatch_plan(route, bm, tp)
    if x_buf is None:
        x_buf = jnp.zeros((nb * bm, D_MODEL), F32)
    x_buf = _scatter_rows(h, dest, x_buf, tm)
    wspec = lambda shape: pl.BlockSpec((None, 1) + shape, lambda i, be: (layer, be[i], 0, 0))
    grid_spec = pltpu.PrefetchScalarGridSpec(
        num_scalar_prefetch=1, grid=(nb,),
        in_specs=[pl.BlockSpec((bm, D_MODEL), lambda i, be: (i, 0)),
                  wspec((D_MODEL, EXPERT_HIDDEN)), wspec((D_MODEL, EXPERT_HIDDEN)),
                  wspec((EXPERT_HIDDEN, D_MODEL))],
        out_specs=pl.BlockSpec((bm, D_MODEL), lambda i, be: (i, 0)),
        scratch_shapes=[pltpu.VMEM((D_MODEL, EXPERT_HIDDEN), BF16), pltpu.VMEM((D_MODEL, EXPERT_HIDDEN), BF16),
                        pltpu.VMEM((EXPERT_HIDDEN, D_MODEL), BF16)])
    y_blocks = pl.pallas_call(
        _moe_kernel, grid_spec=grid_spec,
        out_shape=jax.ShapeDtypeStruct((nb * bm, D_MODEL), F32),
        compiler_params=_cparams("arbitrary"), name="moe_experts",
    )(blk_exp, x_buf, *expert_ws)
    return y_blocks, dest.reshape(-1), x_buf


def _combine_kernel(first_ref, next_ref, x_ref, route_ref, g_ref, y_hbm, o_ref, ybuf, sem, *, tm, final_norm):
    i = pl.program_id(0)
    cur = lax.rem(i, 2)
    nxt = 1 - cur
    n = MOE_TOPK * tm

    def row_of(ref):
        return lambda r: ref[0, 0, r]

    @pl.when(i == 0)
    def _():
        _start_row_gather(y_hbm, row_of(first_ref), ybuf.at[0], sem.at[0], n)

    _start_row_gather(y_hbm, row_of(next_ref), ybuf.at[nxt], sem.at[nxt], n)
    _wait_row_gather(y_hbm, ybuf.at[cur], sem.at[cur], n)
    route = route_ref[...]
    x = x_ref[...] + route[:, 0:1] * ybuf[cur, 0:tm, :] + route[:, 1:2] * ybuf[cur, tm:n, :]
    o_ref[...] = _rms(x, g_ref[...]) if final_norm else x

    @pl.when(i == pl.num_programs(0) - 1)
    def _():
        _wait_row_gather(y_hbm, ybuf.at[nxt], sem.at[nxt], n)


def _combine(x2, y_blocks, dest, route, g, final_norm, tm):
    t = x2.shape[0]
    steps = t // tm
    rows = jnp.transpose(dest.reshape(steps, tm, MOE_TOPK), (0, 2, 1)).reshape(steps, 1, MOE_TOPK * tm)
    g2 = g.reshape(1, D_MODEL)
    row = lambda width: pl.BlockSpec((tm, width), lambda i: (i, 0))
    smem_rows = lambda index_map: pl.BlockSpec((1, 1, MOE_TOPK * tm), index_map, memory_space=pltpu.SMEM)
    return pl.pallas_call(
        functools.partial(_combine_kernel, tm=tm, final_norm=final_norm), grid=(steps,),
        in_specs=[smem_rows(lambda i: (0, 0, 0)), smem_rows(lambda i: (jnp.minimum(i + 1, steps - 1), 0, 0)),
                  row(D_MODEL), row(LANES), pl.BlockSpec((1, D_MODEL), lambda i: (0, 0)),
                  pl.BlockSpec(memory_space=pl.ANY)],
        out_specs=row(D_MODEL), out_shape=jax.ShapeDtypeStruct((t, D_MODEL), F32),
        scratch_shapes=[pltpu.VMEM((2, MOE_TOPK * tm, D_MODEL), F32), pltpu.SemaphoreType.DMA((2,))],
        compiler_params=_cparams("arbitrary"), name="moe_combine",
    )(rows, rows, x2, route, g2, y_blocks)


def _tiles(batch, seq):
    t = batch * seq
    return dict(
        tm_proj=min(256, t), tq=min(256, seq), tk=min(256, seq), tm_sgu=min(512, t),
        s5_steps=min(128, seq), tm_merge=min(256, t), bm=256, tp_plan=min(512, t), tm_scat=min(256, t),
        tm_comb=min(256, t))


def kernel(x, positions, norm_mix_g, w_in, cmp_pos_k, cmp_pos_v, cmp_k_w1, cmp_k_w2, cmp_v_w1, cmp_v_w2, w_attn_o, sgu_norm_g, sgu_w, sgu_b, w_sgu_o, s5_a_re, s5_a_im, s5_log_step, s5_b_re, s5_b_im, s5_c_re, s5_c_im, s5_d, s5_glu_w, s5_glu_b, w_s5_o, w_mix_o, norm_ffn_g, router_group_w, router_group_b, router_expert_w, router_expert_b, expert_w_gate, expert_w_up, expert_w_down, norm_final_g):
    batch, seq, _ = x.shape
    depth = w_in.shape[0]
    cfg = _tiles(batch, seq)
    tables = _rope_tables(positions)
    x2 = x.reshape(batch * seq, D_MODEL)
    x_buf = None
    w_in_all = _prep_in_proj(w_in)
    merge_ws = tuple(w.astype(BF16) for w in (w_attn_o, w_sgu_o, w_s5_o, w_mix_o))
    expert_ws = (expert_w_gate, expert_w_up, expert_w_down)
    for l in range(depth):
        qt, qrt, kvc, ksel, kwin, vt, gt, sgu_in, s5_in, bg = _in_proj(
            x2, norm_mix_g[l].reshape(1, D_MODEL), w_in_all, l, tables, seq, cfg["tm_proj"])
        kc, vct = _compress(kvc, _prep_compress(cmp_pos_k[l], cmp_pos_v[l], cmp_k_w1[l], cmp_k_w2[l],
                                                cmp_v_w1[l], cmp_v_w2[l]), batch, seq)
        attn = _attention(qt, qrt, gt, kc, vct, ksel, kwin, vt, batch, seq, cfg["tq"], cfg["tk"])
        sgu = _sgu(sgu_in, sgu_norm_g[l], sgu_w[l], sgu_b[l], cfg["tm_sgu"])
        s5 = _s5(s5_in, _prep_s5(s5_a_re[l], s5_a_im[l], s5_log_step[l], s5_b_re[l], s5_b_im[l],
                                 s5_c_re[l], s5_c_im[l]),
                 s5_d[l], s5_glu_w[l], s5_glu_b[l], batch, seq, cfg["s5_steps"])
        x_mid, h, route = _merge(x2, attn, sgu, s5, bg, merge_ws, l,
                                 norm_ffn_g[l], router_group_w[l], router_group_b[l],
                                 router_expert_w[l], router_expert_b[l], cfg["tm_merge"])
        y_blocks, dest, x_buf = _experts(h, route, x_buf, expert_ws, l, cfg["bm"], cfg["tp_plan"], cfg["tm_scat"])
        x2 = _combine(x_mid, y_blocks, dest, route, norm_final_g, l == depth - 1, cfg["tm_comb"])
    return x2.reshape(batch, seq, D_MODEL)
```

```python
import functools
import math

import numpy as np
import jax
import jax.numpy as jnp
from jax import lax
from jax.experimental import pallas as pl
from jax.experimental.pallas import tpu as pltpu

F32 = jnp.float32
BF16 = jnp.bfloat16

D_MODEL = 1024
HEAD_DIM = 64
NSA_HEADS = 8
NSA_KV_GROUPS = 2
HEADS_PER_GROUP = NSA_HEADS // NSA_KV_GROUPS
GROUP_WIDTH = HEADS_PER_GROUP * HEAD_DIM
NSA_WIDTH = NSA_HEADS * HEAD_DIM
KV_WIDTH = NSA_KV_GROUPS * HEAD_DIM
ROPE_DIM = HEAD_DIM // 4
ROPE_HALF = ROPE_DIM // 2
ROPE_THETA = 500000.0
CMP_LEN = 32
CMP_STRIDE = 16
CMP_HIDDEN = 128
SLC_LEN = 64
SLC_SHIFT = int(math.log2(SLC_LEN))
SLC_TOP = 16
WINDOW = 512
FORCE_SCORE = 1.0e4
NEG_INF = -1.0e30
SGU_WIDTH = 256
SGU_GROUPS = 4
SGU_CHUNK = 128
S5_WIDTH = 256
S5_GROUP_CH = 16
S5_GROUPS = S5_WIDTH // S5_GROUP_CH
S5_STATE = 64
S5_LANES = S5_GROUPS * S5_STATE
MOE_GROUPS = 4
MOE_EXPERTS_PER_GROUP = 8
MOE_EXPERTS = MOE_GROUPS * MOE_EXPERTS_PER_GROUP
MOE_TOPK = 2
EXPERT_HIDDEN = 512
ROUTER_ROWS = 40
RMS_EPS = 1e-6
ATTN_SCALE = HEAD_DIM ** -0.5
LOG2_E = math.log2(math.e)

LANES = 128
SUBLANES = 8
VMEM_LIMIT_BYTES = 56 * 1024 * 1024

BF16_SUBLANES = 16
V_ROWS = HEAD_DIM + BF16_SUBLANES
GATE_COLS = HEADS_PER_GROUP * 3
GATE_ROWS = 16

_OFF_Q = 0
_OFF_KVC = _OFF_Q + NSA_WIDTH
_OFF_KS = _OFF_KVC + 2 * KV_WIDTH
_OFF_VS = _OFF_KS + KV_WIDTH
_OFF_KW = _OFF_VS + KV_WIDTH
_OFF_VW = _OFF_KW + KV_WIDTH
_OFF_GATE = _OFF_VW + KV_WIDTH
_OFF_SGU = _OFF_GATE + LANES
_OFF_S5 = _OFF_SGU + 2 * SGU_WIDTH
_OFF_BG = _OFF_S5 + S5_WIDTH
_IN_W = _OFF_BG + 3 * D_MODEL


def _cparams(*sem):
    return pltpu.CompilerParams(dimension_semantics=sem, vmem_limit_bytes=VMEM_LIMIT_BYTES)


def _gelu(x):
    return 0.5 * x * (1.0 + jnp.tanh(math.sqrt(2.0 / math.pi) * (x + 0.044715 * (x * x * x))))


def _sigmoid(x):
    return 1.0 / (1.0 + jnp.exp(-x))


def _dot(a, b):
    return jnp.dot(a, b, preferred_element_type=F32)


def _dot_nt(a, b):
    return lax.dot_general(a, b, (((1,), (1,)), ((), ())), preferred_element_type=F32)


def _split_bf16(x):
    hi = x.astype(BF16)
    lo = (x - hi.astype(F32)).astype(BF16)
    return hi, lo


def _rms(x, g):
    return x * lax.rsqrt(jnp.mean(x * x, axis=-1, keepdims=True) + RMS_EPS) * g


def _rope(x, c, s_lo, s_hi):
    n = x.shape[-1]
    return x * c + pltpu.roll(x, n - ROPE_HALF, 1) * s_lo + pltpu.roll(x, ROPE_HALF, 1) * s_hi


def _in_proj_kernel(x_ref, g_ref, w_ref, rope_ref,
                    qt_ref, qrt_ref, kvc_ref, ksel_ref, kwin_ref, vt_ref, gt_ref, sgu_ref, s5_ref, bg_ref, *, seq):
    tm = x_ref.shape[0]
    x = x_ref[...]
    hb = _rms(x, g_ref[...]).astype(BF16)

    def proj(off, width):
        return _dot(hb, w_ref[:, off:off + width])

    c, s_lo, s_hi = _expand_rope_table(rope_ref[...])
    rep = NSA_WIDTH // LANES
    q = proj(_OFF_Q, NSA_WIDTH)
    qt_ref[...] = (q * ATTN_SCALE).T
    qrot = _rope(q, jnp.concatenate([c] * rep, axis=1), jnp.concatenate([s_lo] * rep, axis=1),
                 jnp.concatenate([s_hi] * rep, axis=1))
    qrt_ref[...] = (qrot * (ATTN_SCALE * LOG2_E)).T.astype(BF16)
    kvc_ref[...] = proj(_OFF_KVC, 2 * KV_WIDTH)
    ks = _rope(proj(_OFF_KS, KV_WIDTH), c, s_lo, s_hi)
    kw = _rope(proj(_OFF_KW, KV_WIDTH), c, s_lo, s_hi)
    pos = lax.rem(pl.program_id(0) * tm, seq) + lax.broadcasted_iota(jnp.int32, (tm, 1), 0)
    blk = lax.shift_right_logical(pos, SLC_SHIFT)
    blk_onehot = jnp.where(lax.broadcasted_iota(jnp.int32, (1, HEAD_DIM), 1) == blk, 1.0, 0.0)
    for g in range(NSA_KV_GROUPS):
        cols = slice(g * HEAD_DIM, (g + 1) * HEAD_DIM)
        ksel_ref[g] = jnp.concatenate([ks[:, cols], blk_onehot], axis=1).astype(BF16)
        kwin_ref[g] = kw[:, cols].astype(BF16)
    aug_rows = lax.broadcasted_iota(jnp.int32, (V_ROWS - HEAD_DIM, LANES), 0)
    ones_row = jnp.where(aug_rows == 0, 1.0, 0.0).astype(BF16)
    for p, off in enumerate((_OFF_VS, _OFF_VW)):
        vt = proj(off, KV_WIDTH).T.astype(BF16)
        for g in range(NSA_KV_GROUPS):
            for ch in range(tm // LANES):
                idx = p * NSA_KV_GROUPS + g
                vt_ref[idx, ch, 0:HEAD_DIM, :] = vt[g * HEAD_DIM:(g + 1) * HEAD_DIM, ch * LANES:(ch + 1) * LANES]
                vt_ref[idx, ch, HEAD_DIM:V_ROWS, :] = ones_row
    gt_ref[...] = _sigmoid(proj(_OFF_GATE, LANES)).T[:NSA_KV_GROUPS * GATE_ROWS]
    sgu_ref[...] = proj(_OFF_SGU, 2 * SGU_WIDTH)
    s5_ref[...] = proj(_OFF_S5, S5_WIDTH)
    for k in range(3):
        bg_ref[:, k * D_MODEL:(k + 1) * D_MODEL] = _sigmoid(
            proj(_OFF_BG + k * D_MODEL, D_MODEL)).astype(BF16)


_SRC_GATE = NSA_WIDTH + 6 * KV_WIDTH
_SRC_REST = _SRC_GATE + 3 * NSA_HEADS


def _w_in_kernel(w_ref, o_ref):
    o_ref[:, 0:_OFF_GATE] = w_ref[:, 0:_OFF_GATE].astype(BF16)
    rows = w_ref.shape[0]
    gates = []
    for g in range(NSA_KV_GROUPS):
        gates += [w_ref[:, _SRC_GATE + g * GATE_COLS:_SRC_GATE + (g + 1) * GATE_COLS],
                  jnp.zeros((rows, GATE_ROWS - GATE_COLS), F32)]
    gates.append(jnp.zeros((rows, LANES - NSA_KV_GROUPS * GATE_ROWS), F32))
    o_ref[:, _OFF_GATE:_OFF_SGU] = jnp.concatenate(gates, axis=1).astype(BF16)
    o_ref[:, _OFF_SGU:_IN_W] = w_ref[:, _SRC_REST:_SRC_REST + (_IN_W - _OFF_SGU)].astype(BF16)


def _prep_in_proj(w_in):
    layers, d, width = w_in.shape
    tr = LANES
    return pl.pallas_call(
        _w_in_kernel, grid=(layers, d // tr),
        in_specs=[pl.BlockSpec((None, tr, width), lambda l, i: (l, i, 0))],
        out_specs=pl.BlockSpec((None, tr, _IN_W), lambda l, i: (l, i, 0)),
        out_shape=jax.ShapeDtypeStruct((layers, d, _IN_W), BF16),
        compiler_params=_cparams("parallel", "parallel"), name="w_in_layout",
    )(w_in)


def _layer_spec(a, layer):
    return pl.BlockSpec((None,) + a.shape[1:], lambda *_: (layer,) + (0,) * (a.ndim - 1))


def _rope_tables(positions):
    inv_freq = ROPE_THETA ** (-jnp.arange(ROPE_HALF, dtype=F32) / ROPE_HALF)
    ang = positions.astype(F32).reshape(-1, 1) * inv_freq
    cos, sin = jnp.cos(ang), jnp.sin(ang)
    t = ang.shape[0]
    return jnp.concatenate([cos, cos, -sin, sin, jnp.zeros((t, LANES - 2 * ROPE_DIM), F32)], axis=1)


def _expand_rope_table(tab):
    lane = lax.broadcasted_iota(jnp.int32, (1, LANES), 1)
    in_head = lane & (HEAD_DIM - 1)
    shifted = pltpu.roll(tab, LANES - ROPE_DIM, 1)
    first = lane < ROPE_DIM
    c = jnp.where(first, tab, 0.0)
    s_lo = jnp.where(lane < ROPE_HALF, shifted, 0.0)
    s_hi = jnp.where(first & (lane >= ROPE_HALF), shifted, 0.0)
    spread = lambda a: a + pltpu.roll(a, HEAD_DIM, 1)
    return spread(c) + jnp.where(in_head >= ROPE_DIM, 1.0, 0.0), spread(s_lo), spread(s_hi)


def _in_proj(x2, g, w_all, layer, tables, seq, tm):
    t = x2.shape[0]
    row = lambda width: pl.BlockSpec((tm, width), lambda i: (i, 0))
    col = lambda height: pl.BlockSpec((height, tm), lambda i: (0, i))
    full = lambda a: pl.BlockSpec(a.shape, lambda i: (0,) * a.ndim)
    n_kv = 2 * NSA_KV_GROUPS
    out_shape = (
        jax.ShapeDtypeStruct((NSA_WIDTH, t), F32),
        jax.ShapeDtypeStruct((NSA_WIDTH, t), BF16),
        jax.ShapeDtypeStruct((t, 2 * KV_WIDTH), F32),
        jax.ShapeDtypeStruct((NSA_KV_GROUPS, t, 2 * HEAD_DIM), BF16),
        jax.ShapeDtypeStruct((NSA_KV_GROUPS, t, HEAD_DIM), BF16),
        jax.ShapeDtypeStruct((n_kv, t // LANES, V_ROWS, LANES), BF16),
        jax.ShapeDtypeStruct((NSA_KV_GROUPS * GATE_ROWS, t), F32),
        jax.ShapeDtypeStruct((t, 2 * SGU_WIDTH), F32),
        jax.ShapeDtypeStruct((seq, (t // seq) * S5_WIDTH), F32),
        jax.ShapeDtypeStruct((t, 3 * D_MODEL), BF16),
    )
    seq_tiles = seq // tm
    time_major = pl.BlockSpec((tm, S5_WIDTH), lambda i: (i % seq_tiles, i // seq_tiles))
    out_specs = (col(NSA_WIDTH), col(NSA_WIDTH), row(2 * KV_WIDTH),
                 pl.BlockSpec((NSA_KV_GROUPS, tm, 2 * HEAD_DIM), lambda i: (0, i, 0)),
                 pl.BlockSpec((NSA_KV_GROUPS, tm, HEAD_DIM), lambda i: (0, i, 0)),
                 pl.BlockSpec((n_kv, tm // LANES, V_ROWS, LANES), lambda i: (0, i, 0, 0)),
                 col(NSA_KV_GROUPS * GATE_ROWS), row(2 * SGU_WIDTH), time_major, row(3 * D_MODEL))
    return pl.pallas_call(
        functools.partial(_in_proj_kernel, seq=seq), grid=(t // tm,),
        in_specs=[row(D_MODEL), full(g), _layer_spec(w_all, layer), row(LANES)],
        out_specs=out_specs, out_shape=out_shape,
        compiler_params=_cparams("parallel"), name="in_proj",
    )(x2, g, w_all, tables)


def _compress_kernel(rk_ref, rv_ref, wa_ref, wb_ref, pos_ref, w1_ref, w2_ref, w2t_ref, kc_ref, vct_ref):
    n = rk_ref.shape[0] // CMP_STRIDE
    a = jnp.zeros((n, wa_ref.shape[1]), F32)
    b = jnp.zeros((n, wb_ref.shape[1]), F32)
    for l in range(CMP_STRIDE):
        for half, ref in enumerate((rk_ref, rv_ref)):
            rows_l = ref[pl.ds(l, n, stride=CMP_STRIDE), :].astype(BF16)
            w_rows = slice((2 * l + half) * KV_WIDTH, (2 * l + half + 1) * KV_WIDTH)
            a = a + _dot(rows_l, wa_ref[w_rows, :])
            b = b + _dot(rows_l, wb_ref[w_rows, :])
    b = pltpu.roll(b, n - 1, 0)
    rows = lax.broadcasted_iota(jnp.int32, b.shape, 0)
    b = jnp.where(rows == n - 1, 0.0, b)
    for m in range(2 * NSA_KV_GROUPS):
        typ, g = divmod(m, NSA_KV_GROUPS)
        bias = _dot(pos_ref[typ].astype(BF16), w1_ref[typ])
        hid = a[:, m * CMP_HIDDEN:(m + 1) * CMP_HIDDEN] + b[:, m * CMP_HIDDEN:(m + 1) * CMP_HIDDEN] + bias
        act = _gelu(hid).astype(BF16)
        if typ == 0:
            kc_ref[0, g] = _dot(act, w2_ref[...])
        else:
            vct_ref[0, g] = _dot_nt(w2t_ref[...], act)


def _prep_compress(cmp_pos_k, cmp_pos_v, k_w1, k_w2, v_w1, v_w2):
    streams = 2 * NSA_KV_GROUPS
    w1 = jnp.stack([k_w1, v_w1]).reshape(2, CMP_LEN, HEAD_DIM, CMP_HIDDEN)

    def expand(half):
        out = jnp.zeros((CMP_STRIDE, streams, HEAD_DIM, streams, CMP_HIDDEN), F32)
        for m in range(streams):
            out = out.at[:, m, :, m, :].set(w1[m // NSA_KV_GROUPS, half * CMP_STRIDE:(half + 1) * CMP_STRIDE])
        return out.reshape(CMP_STRIDE * streams * HEAD_DIM, streams * CMP_HIDDEN).astype(BF16)

    pos = jnp.stack([cmp_pos_k, cmp_pos_v]).reshape(2, 1, CMP_LEN * HEAD_DIM)
    return (expand(0), expand(1), pos, jnp.stack([k_w1, v_w1]).astype(BF16),
            k_w2.astype(BF16), v_w2.T.astype(BF16))


def _compress(kvc, prep, batch, seq):
    n = seq // CMP_STRIDE
    full = lambda a: pl.BlockSpec(a.shape, lambda b: (0,) * a.ndim)
    return pl.pallas_call(
        _compress_kernel, grid=(batch,),
        in_specs=[pl.BlockSpec((seq, KV_WIDTH), lambda b: (b, 0)), pl.BlockSpec((seq, KV_WIDTH), lambda b: (b, 1))]
                 + [full(a) for a in prep],
        out_specs=(pl.BlockSpec((1, NSA_KV_GROUPS, n, HEAD_DIM), lambda b: (b, 0, 0, 0)),
                   pl.BlockSpec((1, NSA_KV_GROUPS, HEAD_DIM, n), lambda b: (b, 0, 0, 0))),
        out_shape=(jax.ShapeDtypeStruct((batch, NSA_KV_GROUPS, n, HEAD_DIM), F32),
                   jax.ShapeDtypeStruct((batch, NSA_KV_GROUPS, HEAD_DIM, n), F32)),
        compiler_params=_cparams("parallel"), name="compress_kv",
    )(kvc, kvc, *prep)


def _attn_kernel(qt_ref, qrt_ref, gt_ref, kc_ref, vct_ref, ks_ref, vst_ref, kw_ref, vwt_ref, ovt_ref,
                 o_ref, qa_sc, m_sc, acc_sc, oc_sc, s_sc, p_sc, a_sc, *, tq, tk, seq):
    n_cmp = kc_ref.shape[2]
    n_slc = seq // SLC_LEN
    heads = HEADS_PER_GROUP
    t0 = pl.program_id(2) * tq
    t_lane = t0 + lax.broadcasted_iota(jnp.int32, (1, tq), 1)

    def lane_tile(a):
        return jnp.concatenate([a] * heads, axis=1)

    def head_cols(ref):
        return jnp.concatenate([ref[r * HEAD_DIM:(r + 1) * HEAD_DIM, :] for r in range(heads)], axis=1)

    q_hi, q_lo = _split_bf16(head_cols(qt_ref))
    qr = head_cols(qrt_ref)

    def compressed_and_select(nc, ns, all_selected, forced_distinct):
        kc_hi, kc_lo = _split_bf16(kc_ref[0, 0, 0:nc, :])
        s = _dot(kc_hi, q_hi) + _dot(kc_lo, q_hi) + _dot(kc_hi, q_lo)
        cmp_end = lax.broadcasted_iota(jnp.int32, (nc, 1), 0) * CMP_STRIDE + (CMP_LEN - 1)
        m_c = cmp_end <= lane_tile(t_lane)
        s = jnp.where(m_c, s, NEG_INF)
        e = jnp.where(m_c, jnp.exp(s - jnp.max(s, axis=0, keepdims=True)), 0.0)
        p = e / jnp.maximum(jnp.sum(e, axis=0, keepdims=True), 1e-30)
        oc_sc[...] = _dot(vct_ref[0, 0, :, 0:nc].astype(BF16), p.astype(BF16))

        j_idx = lax.broadcasted_iota(jnp.int32, (ns, 1), 0)
        blk_t = lax.shift_right_logical(t_lane, SLC_SHIFT)
        causal = j_idx <= blk_t
        if all_selected:
            picked = causal
        else:
            p_sum = p[:, 0:tq]
            for r in range(1, heads):
                p_sum = p_sum + p[:, r * tq:(r + 1) * tq]
            ps_hi, ps_lo = _split_bf16(p_sum)
            imp = _dot(ovt_ref[0:ns, 0:nc], ps_hi) + _dot(ovt_ref[0:ns, 0:nc], ps_lo)
            forced = (j_idx == 0) | (j_idx == blk_t) | (j_idx == blk_t - 1)
            j_f32 = j_idx.astype(F32)
            if forced_distinct:
                taken = jnp.where(forced, 1.0, 0.0)
                imp = jnp.where(forced, -jnp.inf, jnp.where(causal, imp, NEG_INF))
                rounds = SLC_TOP - 3
            else:
                taken = jnp.zeros((ns, tq), F32)
                imp = jnp.where(causal, jnp.where(forced, FORCE_SCORE, imp), NEG_INF)
                rounds = min(SLC_TOP, ns)
            for _ in range(rounds):
                best = jnp.max(imp, axis=0, keepdims=True)
                first = jnp.min(jnp.where(imp == best, j_f32, float(ns)), axis=0, keepdims=True)
                hit = j_f32 == first
                taken = jnp.where(hit, 1.0, taken)
                imp = jnp.where(hit, -jnp.inf, imp)
            picked = (taken > 0.5) & causal
        bias = jnp.where(picked, 0.0, NEG_INF)
        if ns < HEAD_DIM:
            bias = jnp.concatenate([bias, jnp.zeros((HEAD_DIM - ns, tq), F32)], axis=0)
        qa_sc[...] = jnp.concatenate([qr, lane_tile(bias.astype(BF16))], axis=0)

    nq = seq // tq
    qi = pl.program_id(2)

    def sizes(tiles):
        nc = min(n_cmp, -(-max(tiles * tq // CMP_STRIDE, 1) // LANES) * LANES)
        ns = min(n_slc, -(-max(tiles * tq // SLC_LEN, 1) // SUBLANES) * SUBLANES)
        return nc, ns

    bounds = sorted({0, min(nq, SLC_TOP * SLC_LEN // tq), min(nq, max(nq // 2, 1)), nq})
    for lo, hi in zip(bounds[:-1], bounds[1:]):
        nc, ns = sizes(hi)
        variant = functools.partial(compressed_and_select, nc, ns, hi * tq <= SLC_TOP * SLC_LEN,
                                    lo * tq >= 2 * SLC_LEN)
        pl.when((qi >= lo) & (qi < hi))(variant)

    m_sc[...] = jnp.full(m_sc.shape, NEG_INF, F32)
    acc_sc[...] = jnp.zeros(acc_sc.shape, F32)
    p_sc[1] = jnp.zeros(p_sc.shape[1:], BF16)
    a_sc[1] = jnp.ones(a_sc.shape[1:], F32)
    lanes_per_tile = tk // LANES

    def scores(j, slot):
        k0 = j * tk if isinstance(j, int) else pl.multiple_of(j * tk, tk)
        s_sc[slot] = _dot(ks_ref[0, pl.ds(k0, tk), :], qa_sc[...])

    def softmax(j, slot, diagonal):
        s = s_sc[slot]
        if diagonal:
            key_pos = j * tk + lax.broadcasted_iota(jnp.int32, (tk, 1), 0)
            s = s + lane_tile(jnp.where(key_pos <= t_lane, 0.0, NEG_INF))
        m_old = m_sc[...]
        m_new = jnp.maximum(m_old, jnp.max(s, axis=0, keepdims=True))
        m_sc[...] = m_new
        a_sc[slot] = jnp.exp2(m_old - m_new)
        p_sc[slot] = jnp.exp2(s - m_new).astype(BF16)

    def accumulate(j, slot):
        tile = jnp.maximum(j, 0) * lanes_per_tile
        v_t = jnp.concatenate([vst_ref[0, tile + c] for c in range(lanes_per_tile)], axis=1)
        acc_sc[...] = a_sc[slot] * acc_sc[...] + _dot(v_t, p_sc[slot])

    def pipe_step(i, i_is_odd):
        a, b = (1, 0) if i_is_odd else (0, 1)
        accumulate(i - 2, a)
        softmax(i - 1, b, False)
        scores(i, a)

    n_kv = (t0 + tq + tk - 1) // tk
    scores(0, 0)

    def two_steps(u, carry):
        pipe_step(2 * u + 1, True)
        pipe_step(2 * u + 2, False)
        return carry

    lax.fori_loop(0, (n_kv - 1) // 2, two_steps, 0)

    @pl.when(((n_kv - 1) & 1) == 1)
    def _():
        pipe_step(n_kv - 1, True)

    last = (n_kv - 1) & 1
    accumulate(n_kv - 2, 1 - last)
    softmax(n_kv - 1, last, True)
    accumulate(n_kv - 1, last)
    o_slc = acc_sc[0:HEAD_DIM, :] / acc_sc[HEAD_DIM:HEAD_DIM + 1, :]
    o_cmp = oc_sc[...]

    slab = min(WINDOW + tq, seq)
    st = pl.multiple_of(jnp.clip(t0 - WINDOW, 0, seq - slab), LANES)
    s = _dot(kw_ref[0, pl.ds(st, slab), :], qa_sc[0:HEAD_DIM, :])
    diff = t_lane - (st + lax.broadcasted_iota(jnp.int32, (slab, 1), 0))
    s = s + lane_tile(jnp.where((diff >= 0) & (diff < WINDOW), 0.0, NEG_INF))
    p = jnp.exp2(s - jnp.max(s, axis=0, keepdims=True))
    st_tile = st // LANES
    vw_t = jnp.concatenate([vwt_ref[0, st_tile + c] for c in range(slab // LANES)], axis=1)
    pv = _dot(vw_t, p.astype(BF16))
    o_win = pv[0:HEAD_DIM] / pv[HEAD_DIM:HEAD_DIM + 1]

    gates = gt_ref[...]
    outs = []
    for r in range(heads):
        cols = slice(r * tq, (r + 1) * tq)
        outs.append(gates[3 * r:3 * r + 1, :] * o_cmp[:, cols] + gates[3 * r + 1:3 * r + 2, :] * o_slc[:, cols]
                    + gates[3 * r + 2:3 * r + 3, :] * o_win[:, cols])
    o_ref[...] = jnp.concatenate(outs, axis=0).T.astype(o_ref.dtype)


def _overlap_matrix_t(n_cmp, n_slc):
    c0 = np.arange(n_cmp)[None, :] * CMP_STRIDE
    s0 = np.arange(n_slc)[:, None] * SLC_LEN
    ov = np.clip(np.minimum(c0 + CMP_LEN, s0 + SLC_LEN) - np.maximum(c0, s0), 0, None) / CMP_LEN
    ov[:, n_cmp - 1] = 0.0
    return jnp.asarray(ov, BF16)


def _attention(qt, qrt, gt, kc, vct, ksel, kwin, vt, batch, seq, tq, tk):
    assert tk % tq == 0 and seq // SLC_LEN <= HEAD_DIM
    n_cmp = seq // CMP_STRIDE
    n_slc = seq // SLC_LEN
    nq = seq // tq
    ksel = ksel.reshape(NSA_KV_GROUPS, batch, seq, 2 * HEAD_DIM)
    kwin = kwin.reshape(NSA_KV_GROUPS, batch, seq, HEAD_DIM)
    vt = vt.reshape(2 * NSA_KV_GROUPS, batch, seq // LANES, V_ROWS, LANES)
    ovt = _overlap_matrix_t(n_cmp, n_slc)
    qspec = pl.BlockSpec((GROUP_WIDTH, tq), lambda b, g, i: (g, b * nq + i))
    k_spec = lambda width: pl.BlockSpec((None, 1, seq, width), lambda b, g, i: (g, b, 0, 0))
    v_spec = lambda p: pl.BlockSpec((None, 1, seq // LANES, V_ROWS, LANES),
                                    lambda b, g, i: (p * NSA_KV_GROUPS + g, b, 0, 0, 0))
    kernel = functools.partial(_attn_kernel, tq=tq, tk=tk, seq=seq)
    return pl.pallas_call(
        kernel, grid=(batch, NSA_KV_GROUPS, nq),
        in_specs=[qspec, qspec, pl.BlockSpec((GATE_ROWS, tq), lambda b, g, i: (g, b * nq + i)),
                  pl.BlockSpec((1, 1, n_cmp, HEAD_DIM), lambda b, g, i: (b, g, 0, 0)),
                  pl.BlockSpec((1, 1, HEAD_DIM, n_cmp), lambda b, g, i: (b, g, 0, 0)),
                  k_spec(2 * HEAD_DIM), v_spec(0), k_spec(HEAD_DIM), v_spec(1),
                  pl.BlockSpec(ovt.shape, lambda b, g, i: (0, 0))],
        out_specs=pl.BlockSpec((tq, GROUP_WIDTH), lambda b, g, i: (b * nq + i, g)),
        out_shape=jax.ShapeDtypeStruct((batch * seq, NSA_WIDTH), BF16),
        scratch_shapes=[pltpu.VMEM((2 * HEAD_DIM, HEADS_PER_GROUP * tq), BF16),
                        pltpu.VMEM((1, HEADS_PER_GROUP * tq), F32),
                        pltpu.VMEM((V_ROWS, HEADS_PER_GROUP * tq), F32),
                        pltpu.VMEM((HEAD_DIM, HEADS_PER_GROUP * tq), F32),
                        pltpu.VMEM((2, tk, HEADS_PER_GROUP * tq), F32),
                        pltpu.VMEM((2, tk, HEADS_PER_GROUP * tq), BF16),
                        pltpu.VMEM((2, 1, HEADS_PER_GROUP * tq), F32)],
        compiler_params=_cparams("parallel", "parallel", "arbitrary"), name="nsa_attention",
    )(qt, qrt, gt, kc, vct, ksel, vt, kwin, vt, ovt)


def _sgu_kernel(uv_ref, g_ref, w_ref, b_ref, o_ref, *, chunks):
    z = _gelu(uv_ref[...])
    u = z[:, :SGU_WIDTH]
    v = _rms(z[:, SGU_WIDTH:], g_ref[...])
    gw = SGU_WIDTH // SGU_GROUPS
    rows = lax.broadcasted_iota(jnp.int32, (SGU_CHUNK, SGU_GROUPS * SGU_CHUNK), 0)
    cols = lax.broadcasted_iota(jnp.int32, (SGU_CHUNK, SGU_GROUPS * SGU_CHUNK), 1)
    w = jnp.where((cols & (SGU_CHUNK - 1)) <= rows, w_ref[...], 0.0).astype(BF16)
    grp_r = lax.broadcasted_iota(jnp.int32, (SGU_GROUPS * SGU_CHUNK, SGU_WIDTH), 0) // SGU_CHUNK
    grp_c = lax.broadcasted_iota(jnp.int32, (SGU_GROUPS * SGU_CHUNK, SGU_WIDTH), 1) // gw
    for c in range(chunks):
        vc = v[c * SGU_CHUNK:(c + 1) * SGU_CHUNK].astype(BF16)
        v_bd = jnp.where(grp_r == grp_c, jnp.concatenate([vc] * SGU_GROUPS, axis=0), jnp.zeros((), BF16))
        mixed = _dot(w, v_bd) + b_ref[...]
        o_ref[c * SGU_CHUNK:(c + 1) * SGU_CHUNK, :] = (u[c * SGU_CHUNK:(c + 1) * SGU_CHUNK] * mixed).astype(o_ref.dtype)


def _sgu(sgu_in, norm_g, w_s, b_s, tm):
    t = sgu_in.shape[0]
    w_cat = jnp.transpose(w_s, (1, 0, 2)).reshape(SGU_CHUNK, SGU_GROUPS * SGU_CHUNK)
    bias = jnp.repeat(b_s.T, SGU_WIDTH // SGU_GROUPS, axis=1)
    g = norm_g.reshape(1, SGU_WIDTH)
    full = lambda a: pl.BlockSpec(a.shape, lambda i: (0,) * a.ndim)
    return pl.pallas_call(
        functools.partial(_sgu_kernel, chunks=tm // SGU_CHUNK), grid=(t // tm,),
        in_specs=[pl.BlockSpec((tm, 2 * SGU_WIDTH), lambda i: (i, 0)), full(g), full(w_cat), full(bias)],
        out_specs=pl.BlockSpec((tm, SGU_WIDTH), lambda i: (i, 0)),
        out_shape=jax.ShapeDtypeStruct((t, SGU_WIDTH), BF16),
        compiler_params=_cparams("parallel"), name="sgu",
    )(sgu_in, g, w_cat, bias)


def _s5_kernel(x_ref, bre_ref, bim_ref, are_ref, aim_ref, cre_ref, cim_ref, d_ref, gw_ref, gb_ref,
               o_ref, hre_sc, him_sc, ure_sc, uim_sc, *, steps, batch):
    @pl.when(pl.program_id(0) == 0)
    def _():
        hre_sc[...] = jnp.zeros(hre_sc.shape, F32)
        him_sc[...] = jnp.zeros(him_sc.shape, F32)

    x = x_ref[...]
    xb = x.astype(BF16)
    ure_sc[...] = _dot(xb, bre_ref[...])
    uim_sc[...] = _dot(xb, bim_ref[...])
    a_re = jnp.broadcast_to(are_ref[...], (batch, S5_LANES))
    a_im = jnp.broadcast_to(aim_ref[...], (batch, S5_LANES))

    def step(t, carry):
        h_re, h_im = carry
        rows = pl.ds(pl.multiple_of(t * batch, batch), batch)
        n_re = a_re * h_re - a_im * h_im + ure_sc[rows, :]
        n_im = a_re * h_im + a_im * h_re + uim_sc[rows, :]
        ure_sc[rows, :] = n_re
        uim_sc[rows, :] = n_im
        return n_re, n_im

    h_re, h_im = lax.fori_loop(0, steps, step, (hre_sc[...], him_sc[...]))
    hre_sc[...] = h_re
    him_sc[...] = h_im
    y = _dot(ure_sc[...].astype(BF16), cre_ref[...]) - _dot(uim_sc[...].astype(BF16), cim_ref[...])
    y = _gelu(y + d_ref[...] * x)
    y = y * _sigmoid(_dot(y.astype(BF16), gw_ref[...]) + gb_ref[...])
    o_ref[...] = y.astype(o_ref.dtype)


def _block_diag(blocks):
    g, r, c = blocks.shape
    eye = jnp.eye(g, dtype=blocks.dtype)
    return (blocks[:, :, None, :] * eye[:, None, :, None]).reshape(g * r, g * c)


def _prep_s5(a_re, a_im, log_step, b_re, b_im, c_re, c_im):
    step = jnp.exp(log_step)[:, None]
    mag = jnp.exp(a_re * step)
    abar_re, abar_im = mag * jnp.cos(a_im * step), mag * jnp.sin(a_im * step)
    den = a_re * a_re + a_im * a_im
    nr, ni = abar_re - 1.0, abar_im
    coef_re = (nr * a_re + ni * a_im) / den
    coef_im = (ni * a_re - nr * a_im) / den
    bbar_re = coef_re[..., None] * b_re - coef_im[..., None] * b_im
    bbar_im = coef_re[..., None] * b_im + coef_im[..., None] * b_re
    to_in = lambda b: _block_diag(jnp.transpose(b, (0, 2, 1))).astype(BF16)
    to_out = lambda c: _block_diag(jnp.transpose(c, (0, 2, 1))).astype(BF16)
    return (to_in(bbar_re), to_in(bbar_im), abar_re.reshape(1, S5_LANES), abar_im.reshape(1, S5_LANES),
            to_out(c_re), to_out(c_im))


def _s5(xs, prep, d, glu_w, glu_b, batch, seq, steps):
    bre, bim, are, aim, cre, cim = prep
    x_tm = xs.reshape(seq * batch, S5_WIDTH)
    rows = steps * batch
    d2, gb2, gwb = d.reshape(1, S5_WIDTH), glu_b.reshape(1, S5_WIDTH), glu_w.astype(BF16)
    full = lambda a: pl.BlockSpec(a.shape, lambda i: (0,) * a.ndim)
    y = pl.pallas_call(
        functools.partial(_s5_kernel, steps=steps, batch=batch), grid=(seq // steps,),
        in_specs=[pl.BlockSpec((rows, S5_WIDTH), lambda i: (i, 0)), full(bre), full(bim), full(are), full(aim),
                  full(cre), full(cim), full(d2), full(gwb), full(gb2)],
        out_specs=pl.BlockSpec((rows, S5_WIDTH), lambda i: (i, 0)),
        out_shape=jax.ShapeDtypeStruct((seq * batch, S5_WIDTH), BF16),
        scratch_shapes=[pltpu.VMEM((batch, S5_LANES), F32), pltpu.VMEM((batch, S5_LANES), F32),
                        pltpu.VMEM((rows, S5_LANES), F32), pltpu.VMEM((rows, S5_LANES), F32)],
        compiler_params=_cparams("arbitrary"), name="s5_scan",
    )(x_tm, bre, bim, are, aim, cre, cim, d2, gwb, gb2)
    return y.reshape(seq, batch * S5_WIDTH)


def _merge_kernel(x_ref, a_ref, b_ref, c_ref, bg_ref, wa_ref, wb_ref, wc_ref, wm_ref, g_ref,
                  wrh_ref, wrl_ref, br_ref, xo_ref, h_ref, route_ref):
    y_a = _dot(a_ref[...], wa_ref[...])
    y_b = _dot(b_ref[...], wb_ref[...])
    y_c = _dot(c_ref[...], wc_ref[...])
    merged = (bg_ref[:, :D_MODEL].astype(F32) * y_a + bg_ref[:, D_MODEL:2 * D_MODEL].astype(F32) * y_b
              + bg_ref[:, 2 * D_MODEL:].astype(F32) * y_c)
    x = x_ref[...] + _dot(merged.astype(BF16), wm_ref[...])
    xo_ref[...] = x
    h = _rms(x, g_ref[...])
    h_ref[...] = h

    h_hi, h_lo = _split_bf16(h)
    logits = (_dot_nt(wrh_ref[...], h_hi) + _dot_nt(wrl_ref[...], h_hi) + _dot_nt(wrh_ref[...], h_lo))
    logits = logits[0:ROUTER_ROWS] + br_ref[...]
    row = lax.broadcasted_iota(jnp.int32, (ROUTER_ROWS, 1), 0).astype(F32)
    big = float(ROUTER_ROWS)
    is_grp = row < MOE_GROUPS
    gl = jnp.where(is_grp, logits, -jnp.inf)
    gmax = jnp.max(gl, axis=0, keepdims=True)
    grp = jnp.min(jnp.where(gl == gmax, row, big), axis=0, keepdims=True)
    grp_w = 1.0 / jnp.sum(jnp.where(is_grp, jnp.exp(logits - gmax), 0.0), axis=0, keepdims=True)
    e_row = row - MOE_GROUPS
    in_grp = (e_row >= 0) & (jnp.floor(e_row * (1.0 / MOE_EXPERTS_PER_GROUP)) == grp)
    emax = jnp.max(jnp.where(in_grp, logits, -jnp.inf), axis=0, keepdims=True)
    ee = jnp.where(in_grp, jnp.exp(logits - emax), 0.0)
    prob = jnp.where(in_grp, ee / jnp.sum(ee, axis=0, keepdims=True), -1.0)
    p1 = jnp.max(prob, axis=0, keepdims=True)
    j1 = jnp.min(jnp.where(prob == p1, row, big), axis=0, keepdims=True)
    prob2 = jnp.where(row == j1, -1.0, prob)
    p2 = jnp.max(prob2, axis=0, keepdims=True)
    j2 = jnp.min(jnp.where(prob2 == p2, row, big), axis=0, keepdims=True)
    psum = p1 + p2
    out_row = lax.broadcasted_iota(jnp.int32, (LANES, 1), 0)
    route_t = jnp.where(out_row == 0, grp_w * p1 / psum, 0.0)
    route_t = jnp.where(out_row == 1, grp_w * p2 / psum, route_t)
    route_t = jnp.where(out_row == 2, j1 - MOE_GROUPS, route_t)
    route_t = jnp.where(out_row == 3, j2 - MOE_GROUPS, route_t)
    route_ref[...] = route_t.T


def _merge(x2, attn, sgu, s5, bg, ws, layer, norm_g, rgw, rgb, rew, reb, tm):
    t = x2.shape[0]
    seq_tiles = s5.shape[0] // tm
    wr = jnp.concatenate([rgw, rew, jnp.zeros((D_MODEL, LANES - MOE_GROUPS - MOE_EXPERTS), F32)], axis=1).T
    wr_hi = wr.astype(BF16)
    wr_lo = (wr - wr_hi.astype(F32)).astype(BF16)
    br = jnp.concatenate([rgb, reb, jnp.zeros((ROUTER_ROWS - MOE_GROUPS - MOE_EXPERTS,), F32)]).reshape(ROUTER_ROWS, 1)
    g = norm_g.reshape(1, D_MODEL)
    row = lambda width: pl.BlockSpec((tm, width), lambda i: (i, 0))
    full = lambda a: pl.BlockSpec(a.shape, lambda i: (0,) * a.ndim)
    return pl.pallas_call(
        _merge_kernel, grid=(t // tm,),
        in_specs=[row(D_MODEL), row(NSA_WIDTH), row(SGU_WIDTH),
                  pl.BlockSpec((tm, S5_WIDTH), lambda i: (i % seq_tiles, i // seq_tiles)),
                  row(3 * D_MODEL)]
                 + [_layer_spec(w, layer) for w in ws] + [full(g), full(wr_hi), full(wr_lo), full(br)],
        out_specs=(row(D_MODEL), row(D_MODEL), row(LANES)),
        out_shape=(jax.ShapeDtypeStruct((t, D_MODEL), F32), jax.ShapeDtypeStruct((t, D_MODEL), F32),
                   jax.ShapeDtypeStruct((t, LANES), F32)),
        compiler_params=_cparams("parallel"), name="merge_route",
    )(x2, attn, sgu, s5, bg, *ws, g, wr_hi, wr_lo, br)


def _start_row_gather(src_hbm, rows_of, dst_buf, sem, n):
    for r in range(n):
        pltpu.make_async_copy(src_hbm.at[pl.ds(rows_of(r), 1)], dst_buf.at[pl.ds(r, 1)], sem).start(priority=r % 2)


def _wait_row_gather(src_hbm, dst_buf, sem, n):
    pltpu.make_async_copy(src_hbm.at[pl.ds(0, n)], dst_buf, sem).wait()


def _plan_kernel(route_ref, dest_ref, cnt_ref, carry_sc, *, bm):
    phase, i = pl.program_id(0), pl.program_id(1)
    tp = route_ref.shape[0]
    lane = lax.broadcasted_iota(jnp.int32, (1, LANES), 1)
    route = route_ref[...]
    oh0 = jnp.where(lane == route[:, 2:3].astype(jnp.int32), 1.0, 0.0)
    oh1 = jnp.where(lane == route[:, 3:4].astype(jnp.int32), 1.0, 0.0)
    both = oh0 + oh1
    col_sum = jnp.broadcast_to(jnp.sum(both, axis=0, keepdims=True), carry_sc.shape)

    @pl.when((phase == 0) & (i == 0))
    def _():
        carry_sc[...] = jnp.zeros(carry_sc.shape, F32)

    @pl.when(phase == 0)
    def _():
        carry_sc[...] = carry_sc[...] + col_sum

    @pl.when((phase == 1) & (i == 0))
    def _():
        counts = carry_sc[...]
        cnt_ref[...] = counts
        padded = jnp.floor((counts + (bm - 1)) * (1.0 / bm)) * bm
        incl = padded
        shift = 1
        while shift < LANES:
            incl = incl + jnp.where(lane >= shift, pltpu.roll(incl, shift, 1), 0.0)
            shift *= 2
        carry_sc[...] = incl - padded

    @pl.when(phase == 1)
    def _():
        r_idx = lax.broadcasted_iota(jnp.int32, (tp, tp), 0)
        c_idx = lax.broadcasted_iota(jnp.int32, (tp, tp), 1)
        earlier = jnp.where(c_idx < r_idx, 1.0, 0.0).astype(BF16)
        base = _dot(earlier, both.astype(BF16)) + carry_sc[0:1, :]
        d0 = jnp.sum(oh0 * base, axis=1, keepdims=True)
        d1 = jnp.sum(oh1 * base, axis=1, keepdims=True)
        dest_ref[...] = jnp.where(lane == 0, d0, jnp.where(lane == 1, d1, 0.0)).astype(jnp.int32)
        carry_sc[...] = carry_sc[...] + col_sum


def _dispatch_plan(route, bm, tp):
    t = route.shape[0]
    steps = t // tp
    dest, counts = pl.pallas_call(
        functools.partial(_plan_kernel, bm=bm), grid=(2, steps),
        in_specs=[pl.BlockSpec((tp, LANES), lambda p, i: (i, 0))],
        out_specs=(pl.BlockSpec((tp, LANES), lambda p, i: (i * p, 0)),
                   pl.BlockSpec((SUBLANES, LANES), lambda p, i: (0, 0))),
        out_shape=(jax.ShapeDtypeStruct((t, LANES), jnp.int32), jax.ShapeDtypeStruct((SUBLANES, LANES), F32)),
        scratch_shapes=[pltpu.VMEM((SUBLANES, LANES), F32)],
        compiler_params=_cparams("arbitrary", "arbitrary"), name="moe_plan",
    )(route)
    nb = t * MOE_TOPK // bm + MOE_EXPERTS
    nblk = (counts[0, :MOE_EXPERTS].astype(jnp.int32) + (bm - 1)) // bm
    blk_end = jnp.cumsum(nblk)
    blk_exp = jnp.sum((blk_end[None, :] <= jnp.arange(nb, dtype=jnp.int32)[:, None]).astype(jnp.int32), axis=1)
    return dest[:, :MOE_TOPK], jnp.minimum(blk_exp, MOE_EXPERTS - 1), nb


def _scatter_kernel(rows_ref, h_ref, zero_hbm, x_hbm, sem, *, tm):
    del zero_hbm
    n = MOE_TOPK * tm
    for r in range(n):
        pltpu.make_async_copy(h_ref.at[pl.ds(r // MOE_TOPK, 1)], x_hbm.at[pl.ds(rows_ref[0, 0, r], 1)],
                              sem).start(priority=r % 2)
    for _ in range(MOE_TOPK):
        pltpu.make_async_copy(h_ref, x_hbm.at[pl.ds(0, tm)], sem).wait()


def _scatter_rows(h, dest, x_buf, tm):
    t = h.shape[0]
    steps = t // tm
    rows = dest.reshape(steps, 1, MOE_TOPK * tm)
    return pl.pallas_call(
        functools.partial(_scatter_kernel, tm=tm), grid=(steps,),
        in_specs=[pl.BlockSpec((1, 1, MOE_TOPK * tm), lambda i: (i, 0, 0), memory_space=pltpu.SMEM),
                  pl.BlockSpec((tm, D_MODEL), lambda i: (i, 0)),
                  pl.BlockSpec(memory_space=pl.ANY)],
        out_specs=pl.BlockSpec(memory_space=pl.ANY),
        out_shape=jax.ShapeDtypeStruct(x_buf.shape, F32),
        scratch_shapes=[pltpu.SemaphoreType.DMA(())],
        input_output_aliases={2: 0},
        compiler_params=_cparams("arbitrary"), name="moe_dispatch",
    )(rows, h, x_buf)


def _moe_kernel(blk_exp_ref, x_ref, wg_ref, wu_ref, wd_ref, y_ref, wg_sc, wu_sc, wd_sc):
    i = pl.program_id(0)
    new_expert = (i == 0) | (blk_exp_ref[i] != blk_exp_ref[jnp.maximum(i - 1, 0)])

    @pl.when(new_expert)
    def _():
        wg_sc[...] = wg_ref[0].astype(BF16)
        wu_sc[...] = wu_ref[0].astype(BF16)
        wd_sc[...] = wd_ref[0].astype(BF16)

    xb = x_ref[...].astype(BF16)
    gate = _dot(xb, wg_sc[...])
    hid = gate * _sigmoid(gate) * _dot(xb, wu_sc[...])
    y_ref[...] = _dot(hid.astype(BF16), wd_sc[...])


def _experts(h, route, x_buf, expert_ws, layer, bm, tp, tm):
    dest, blk_exp, nb = _dispatch_plan(route, bm, tp)
    if x_buf is None:
        x_buf = jnp.zeros((nb * bm, D_MODEL), F32)
    x_buf = _scatter_rows(h, dest, x_buf, tm)
    wspec = lambda shape: pl.BlockSpec((None, 1) + shape, lambda i, be: (layer, be[i], 0, 0))
    grid_spec = pltpu.PrefetchScalarGridSpec(
        num_scalar_prefetch=1, grid=(nb,),
        in_specs=[pl.BlockSpec((bm, D_MODEL), lambda i, be: (i, 0)),
                  wspec((D_MODEL, EXPERT_HIDDEN)), wspec((D_MODEL, EXPERT_HIDDEN)),
                  wspec((EXPERT_HIDDEN, D_MODEL))],
        out_specs=pl.BlockSpec((bm, D_MODEL), lambda i, be: (i, 0)),
        scratch_shapes=[pltpu.VMEM((D_MODEL, EXPERT_HIDDEN), BF16), pltpu.VMEM((D_MODEL, EXPERT_HIDDEN), BF16),
                        pltpu.VMEM((EXPERT_HIDDEN, D_MODEL), BF16)])
    y_blocks = pl.pallas_call(
        _moe_kernel, grid_spec=grid_spec,
        out_shape=jax.ShapeDtypeStruct((nb * bm, D_MODEL), F32),
        compiler_params=_cparams("arbitrary"), name="moe_experts",
    )(blk_exp, x_buf, *expert_ws)
    return y_blocks, dest.reshape(-1), x_buf


def _combine_kernel(first_ref, next_ref, x_ref, route_ref, g_ref, y_hbm, o_ref, ybuf, sem, *, tm, final_norm):
    i = pl.program_id(0)
    cur = lax.rem(i, 2)
    nxt = 1 - cur
    n = MOE_TOPK * tm

    def row_of(ref):
        return lambda r: ref[0, 0, r]

    @pl.when(i == 0)
    def _():
        _start_row_gather(y_hbm, row_of(first_ref), ybuf.at[0], sem.at[0], n)

    _start_row_gather(y_hbm, row_of(next_ref), ybuf.at[nxt], sem.at[nxt], n)
    _wait_row_gather(y_hbm, ybuf.at[cur], sem.at[cur], n)
    route = route_ref[...]
    x = x_ref[...] + route[:, 0:1] * ybuf[cur, 0:tm, :] + route[:, 1:2] * ybuf[cur, tm:n, :]
    o_ref[...] = _rms(x, g_ref[...]) if final_norm else x

    @pl.when(i == pl.num_programs(0) - 1)
    def _():
        _wait_row_gather(y_hbm, ybuf.at[nxt], sem.at[nxt], n)


def _combine(x2, y_blocks, dest, route, g, final_norm, tm):
    t = x2.shape[0]
    steps = t // tm
    rows = jnp.transpose(dest.reshape(steps, tm, MOE_TOPK), (0, 2, 1)).reshape(steps, 1, MOE_TOPK * tm)
    g2 = g.reshape(1, D_MODEL)
    row = lambda width: pl.BlockSpec((tm, width), lambda i: (i, 0))
    smem_rows = lambda index_map: pl.BlockSpec((1, 1, MOE_TOPK * tm), index_map, memory_space=pltpu.SMEM)
    return pl.pallas_call(
        functools.partial(_combine_kernel, tm=tm, final_norm=final_norm), grid=(steps,),
        in_specs=[smem_rows(lambda i: (0, 0, 0)), smem_rows(lambda i: (jnp.minimum(i + 1, steps - 1), 0, 0)),
                  row(D_MODEL), row(LANES), pl.BlockSpec((1, D_MODEL), lambda i: (0, 0)),
                  pl.BlockSpec(memory_space=pl.ANY)],
        out_specs=row(D_MODEL), out_shape=jax.ShapeDtypeStruct((t, D_MODEL), F32),
        scratch_shapes=[pltpu.VMEM((2, MOE_TOPK * tm, D_MODEL), F32), pltpu.SemaphoreType.DMA((2,))],
        compiler_params=_cparams("arbitrary"), name="moe_combine",
    )(rows, rows, x2, route, g2, y_blocks)


def _tiles(batch, seq):
    t = batch * seq
    return dict(
        tm_proj=min(256, t), tq=min(256, seq), tk=min(256, seq), tm_sgu=min(512, t),
        s5_steps=min(128, seq), tm_merge=min(256, t), bm=256, tp_plan=min(512, t), tm_scat=min(1024, t),
        tm_comb=min(256, t))


def kernel(x, positions, norm_mix_g, w_in, cmp_pos_k, cmp_pos_v, cmp_k_w1, cmp_k_w2, cmp_v_w1, cmp_v_w2, w_attn_o, sgu_norm_g, sgu_w, sgu_b, w_sgu_o, s5_a_re, s5_a_im, s5_log_step, s5_b_re, s5_b_im, s5_c_re, s5_c_im, s5_d, s5_glu_w, s5_glu_b, w_s5_o, w_mix_o, norm_ffn_g, router_group_w, router_group_b, router_expert_w, router_expert_b, expert_w_gate, expert_w_up, expert_w_down, norm_final_g):
    batch, seq, _ = x.shape
    depth = w_in.shape[0]
    cfg = _tiles(batch, seq)
    tables = _rope_tables(positions)
    x2 = x.reshape(batch * seq, D_MODEL)
    x_buf = None
    w_in_all = _prep_in_proj(w_in)
    merge_ws = tuple(w.astype(BF16) for w in (w_attn_o, w_sgu_o, w_s5_o, w_mix_o))
    expert_ws = (expert_w_gate, expert_w_up, expert_w_down)
    for l in range(depth):
        qt, qrt, kvc, ksel, kwin, vt, gt, sgu_in, s5_in, bg = _in_proj(
            x2, norm_mix_g[l].reshape(1, D_MODEL), w_in_all, l, tables, seq, cfg["tm_proj"])
        kc, vct = _compress(kvc, _prep_compress(cmp_pos_k[l], cmp_pos_v[l], cmp_k_w1[l], cmp_k_w2[l],
                                                cmp_v_w1[l], cmp_v_w2[l]), batch, seq)
        attn = _attention(qt, qrt, gt, kc, vct, ksel, kwin, vt, batch, seq, cfg["tq"], cfg["tk"])
        sgu = _sgu(sgu_in, sgu_norm_g[l], sgu_w[l], sgu_b[l], cfg["tm_sgu"])
        s5 = _s5(s5_in, _prep_s5(s5_a_re[l], s5_a_im[l], s5_log_step[l], s5_b_re[l], s5_b_im[l],
                                 s5_c_re[l], s5_c_im[l]),
                 s5_d[l], s5_glu_w[l], s5_glu_b[l], batch, seq, cfg["s5_steps"])
        x_mid, h, route = _merge(x2, attn, sgu, s5, bg, merge_ws, l,
                                 norm_ffn_g[l], router_group_w[l], router_group_b[l],
                                 router_expert_w[l], router_expert_b[l], cfg["tm_merge"])
        y_blocks, dest, x_buf = _experts(h, route, x_buf, expert_ws, l, cfg["bm"], cfg["tp_plan"], cfg["tm_scat"])
        x2 = _combine(x_mid, y_blocks, dest, route, norm_final_g, l == depth - 1, cfg["tm_comb"])
    return x2.reshape(batch, seq, D_MODEL)
```
